```python
import jax
import jax.numpy as jnp
from jax import lax
import numpy as np

D_MODEL = 4096
BATCH = 2
SEQ = 8192
DEPTH = 1

HEAD_DIM = 128
D_GMLP = D_MODEL // 2
D_NA = D_MODEL - D_GMLP
N_GMLP_HEADS = D_GMLP // HEAD_DIM
N_NA_HEADS = D_NA // HEAD_DIM
D_IN = 2 * D_GMLP + 3 * D_NA
CHUNK = 128
GRID_W = 64
WIN_ROWS_MAX = 8
WIN_COLS = 16
D_FF = 11008
CONV_W = 3
D_PLE = 256
EPS = 1e-6

kernel_name = "hybrid_gmlp_natten_convffn_block"


def rms_norm(x, g):
    xf = x.astype(jnp.float32)
    y = xf * lax.rsqrt(jnp.mean(xf * xf, axis=-1, keepdims=True) + EPS)
    return (y * g.astype(jnp.float32)).astype(x.dtype)


def chunked_spatial_gating(u, v, g_v, w_s, b_s):
    B, S, H, hd = v.shape
    n_chunks = S // CHUNK
    vc = rms_norm(v, g_v).reshape(B, n_chunks, CHUNK, H, hd)
    mixed = jnp.einsum('hij,bnjhd->bnihd', w_s, vc) + b_s.T[None, None, :, :, None]
    return u * mixed.reshape(B, S, H, hd)


def neighbourhood_attention(q, k, v, rpb):
    B, S, H, hd = q.shape
    rows = S // GRID_W
    kr = min(WIN_ROWS_MAX, rows)
    qg = q.reshape(B, rows, GRID_W, H, hd)
    kg = k.reshape(B, rows, GRID_W, H, hd)
    vg = v.reshape(B, rows, GRID_W, H, hd)
    col = jnp.arange(GRID_W)
    col_start = jnp.clip(col - WIN_COLS // 2, 0, GRID_W - WIN_COLS)
    col_idx = col_start[:, None] + jnp.arange(WIN_COLS)[None, :]
    dc = col_idx - col[:, None] + (WIN_COLS - 1)
    rpb_col = rpb[:, :, dc]
    scale = HEAD_DIM ** -0.5

    def one_row(r):
        rs = jnp.clip(r - kr // 2, 0, rows - kr)
        q_r = lax.dynamic_index_in_dim(qg, r, axis=1, keepdims=False)
        k_blk = lax.dynamic_slice_in_dim(kg, rs, kr, axis=1)
        v_blk = lax.dynamic_slice_in_dim(vg, rs, kr, axis=1)
        k_nb = k_blk[:, :, col_idx]
        v_nb = v_blk[:, :, col_idx]
        dr = rs + jnp.arange(kr) - r + (WIN_ROWS_MAX - 1)
        bias = jnp.transpose(rpb_col[:, dr], (0, 2, 1, 3))
        s = jnp.einsum('bqhd,brqchd->bhqrc', q_r, k_nb,
                       preferred_element_type=jnp.float32) * scale
        s = s + bias[None].astype(jnp.float32)
        pr = jax.nn.softmax(s.reshape(B, H, GRID_W, kr * WIN_COLS), axis=-1)
        pr = pr.reshape(B, H, GRID_W, kr, WIN_COLS).astype(v.dtype)
        return jnp.einsum('bhqrc,brqchd->bqhd', pr, v_nb)

    out = lax.map(one_row, jnp.arange(rows))
    return jnp.moveaxis(out, 0, 1).reshape(B, S, H, hd)


def conv_ffn(xn, w_up, conv_w, conv_b, w_down):
    S = xn.shape[1]
    up = xn @ w_up
    half = CONV_W // 2
    padded = jnp.pad(up, ((0, 0), (half, half), (0, 0)))
    c = conv_b + sum(padded[:, j:j + S] * conv_w[j] for j in range(CONV_W))
    gate, val = jnp.split(c, 2, axis=-1)
    return (jax.nn.gelu(gate, approximate=False) * val) @ w_down


def setup_inputs(seed: int = 0) -> dict:
    key = jax.random.key(seed)
    ks = jax.random.split(key, 22)
    f32 = jnp.float32
    L = DEPTH

    def nrm(k, shape, scale):
        return jax.random.normal(k, shape, f32) * scale

    def gain(k, shape):
        return 1.0 + 0.05 * jax.random.normal(k, shape, f32)

    return {
        "x": nrm(ks[0], (BATCH, SEQ, D_MODEL), 1.0),
        "p": nrm(ks[1], (DEPTH, BATCH, SEQ, D_PLE), 1.0),
        "norm_mix_g": gain(ks[2], (L, D_MODEL)),
        "w_in": nrm(ks[3], (L, D_MODEL, D_IN), D_MODEL ** -0.5),
        "gmlp_v_g": gain(ks[4], (L, N_GMLP_HEADS, HEAD_DIM)),
        "gmlp_ws": nrm(ks[5], (L, N_GMLP_HEADS, CHUNK, CHUNK), CHUNK ** -0.5),
        "gmlp_bs": 1.0 + 0.1 * jax.random.normal(ks[6], (L, N_GMLP_HEADS, CHUNK), f32),
        "q_norm_g": gain(ks[7], (L, HEAD_DIM)),
        "k_norm_g": gain(ks[8], (L, HEAD_DIM)),
        "na_rpb": nrm(ks[9], (L, N_NA_HEADS, 2 * WIN_ROWS_MAX - 1, 2 * WIN_COLS - 1), 0.2),
        "out_norm_a_g": gain(ks[10], (L, N_GMLP_HEADS, HEAD_DIM)),
        "out_norm_b_g": gain(ks[11], (L, N_NA_HEADS, HEAD_DIM)),
        "w_out": nrm(ks[12], (L, D_MODEL, D_MODEL), D_MODEL ** -0.5),
        "norm_ffn_g": gain(ks[13], (L, D_MODEL)),
        "w_up": nrm(ks[14], (L, D_MODEL, 2 * D_FF), D_MODEL ** -0.5),
        "conv_w": nrm(ks[15], (L, CONV_W, 2 * D_FF), CONV_W ** -0.5),
        "conv_b": nrm(ks[16], (L, 2 * D_FF), 0.01),
        "w_down": nrm(ks[17], (L, D_FF, D_MODEL), D_FF ** -0.5),
        "norm_ple_g": gain(ks[18], (L, D_MODEL)),
        "w_ple_gate": nrm(ks[19], (L, D_MODEL, D_MODEL), D_MODEL ** -0.5),
        "w_ple_proj": nrm(ks[20], (L, D_PLE, D_MODEL), D_PLE ** -0.5),
        "ple_post_g": gain(ks[21], (L, D_MODEL)),
    }


def reference(x, p, norm_mix_g, w_in, gmlp_v_g, gmlp_ws, gmlp_bs, q_norm_g, k_norm_g,
              na_rpb, out_norm_a_g, out_norm_b_g, w_out, norm_ffn_g, w_up, conv_w,
              conv_b, w_down, norm_ple_g, w_ple_gate, w_ple_proj, ple_post_g):
    B, S, _ = x.shape
    h = x
    for i in range(DEPTH):
        hn = rms_norm(h, norm_mix_g[i])
        z = jnp.einsum('bsd,de->bse', hn, w_in[i])
        zg = jax.nn.gelu(z[..., :2 * D_GMLP], approximate=False)
        u = zg[..., :D_GMLP].reshape(B, S, N_GMLP_HEADS, HEAD_DIM)
        vs = zg[..., D_GMLP:].reshape(B, S, N_GMLP_HEADS, HEAD_DIM)
        q, k, v = jnp.split(z[..., 2 * D_GMLP:], 3, axis=-1)
        q = rms_norm(q.reshape(B, S, N_NA_HEADS, HEAD_DIM), q_norm_g[i])
        k = rms_norm(k.reshape(B, S, N_NA_HEADS, HEAD_DIM), k_norm_g[i])
        v = v.reshape(B, S, N_NA_HEADS, HEAD_DIM)
        a_out = chunked_spatial_gating(u, vs, gmlp_v_g[i], gmlp_ws[i], gmlp_bs[i])
        b_out = neighbourhood_attention(q, k, v, na_rpb[i])
        mix = jnp.concatenate([
            rms_norm(a_out, out_norm_a_g[i]).reshape(B, S, D_GMLP),
            rms_norm(b_out, out_norm_b_g[i]).reshape(B, S, D_NA)], axis=-1)
        h = h + jnp.einsum('bse,ed->bsd', mix, w_out[i])
        h = h + conv_ffn(rms_norm(h, norm_ffn_g[i]), w_up[i], conv_w[i], conv_b[i], w_down[i])
        gate = jax.nn.sigmoid(jnp.einsum('bsd,de->bse', rms_norm(h, norm_ple_g[i]), w_ple_gate[i]))
        e = rms_norm(jnp.einsum('bsk,kd->bsd', p[i], w_ple_proj[i]), ple_post_g[i])
        h = h + gate * e
    return h
```

```python
import functools

import numpy as np
import jax
import jax.numpy as jnp
from jax import lax
from jax.experimental import pallas as pl
from jax.experimental.pallas import tpu as pltpu

F32 = jnp.float32
BF16 = jnp.bfloat16

EPS = 1e-6
HEAD_DIM = 128
CHUNK = 128
GRID_W = 64
WIN_ROWS = 8
WIN_COLS = 16
MASKED = -1e30
HALO = 16
MIB = 1 << 20


def _params(semantics, vmem_mib):
    return pltpu.CompilerParams(dimension_semantics=semantics,
                                vmem_limit_bytes=vmem_mib * MIB)


def _rms(x, g):
    ms = jnp.mean(x * x, axis=-1, keepdims=True)
    return x * lax.rsqrt(ms + EPS) * g


def _gelu(x):
    return 0.5 * x * (1.0 + lax.erf(x * np.float32(np.sqrt(0.5))))


def _rmsnorm_kernel(x_ref, g_ref, o_ref):
    o_ref[...] = _rms(x_ref[...], g_ref[...]).astype(o_ref.dtype)


def _rmsnorm(x, g, bm=512):
    t, d = x.shape
    return pl.pallas_call(
        _rmsnorm_kernel,
        grid=(t // bm,),
        in_specs=[pl.BlockSpec((bm, d), lambda i: (i, 0)),
                  pl.BlockSpec((1, d), lambda i: (0, 0))],
        out_specs=pl.BlockSpec((bm, d), lambda i: (i, 0)),
        out_shape=jax.ShapeDtypeStruct((t, d), BF16),
        compiler_params=_params(("parallel",), 40),
        name="rmsnorm",
    )(x, g.reshape(1, d))


def _inproj_kernel(a_ref, w_ref, o_ref, *, n_gelu_blocks, heads_per_block):
    acc = jnp.dot(a_ref[...], w_ref[...], preferred_element_type=F32)
    j = pl.program_id(1)

    def store(val):
        for hh in range(heads_per_block):
            o_ref[hh] = val[:, hh * HEAD_DIM:(hh + 1) * HEAD_DIM].astype(o_ref.dtype)

    @pl.when(j < n_gelu_blocks)
    def _():
        store(_gelu(acc))

    @pl.when(j >= n_gelu_blocks)
    def _():
        store(acc)


def _inproj(hn, w, n_gelu_cols, bm=1024, bn=1024):
    t, k = hn.shape
    n = w.shape[1]
    hpb = bn // HEAD_DIM
    return pl.pallas_call(
        functools.partial(_inproj_kernel, n_gelu_blocks=n_gelu_cols // bn, heads_per_block=hpb),
        grid=(t // bm, n // bn),
        in_specs=[pl.BlockSpec((bm, k), lambda i, j: (i, 0)),
                  pl.BlockSpec((k, bn), lambda i, j: (0, j))],
        out_specs=pl.BlockSpec((hpb, bm, HEAD_DIM), lambda i, j: (j, i, 0)),
        out_shape=jax.ShapeDtypeStruct((n // HEAD_DIM, t, HEAD_DIM), BF16),
        compiler_params=_params(("parallel", "arbitrary"), 56),
        name="inproj",
    )(hn, w)


def _gmlp_kernel(u_ref, v_ref, ws_ref, bs_ref, gv_ref, og_ref, o_ref, *, n_chunks):
    ws = ws_ref[0]
    bs = bs_ref[0]
    gv = gv_ref[0]
    og = og_ref[0]
    for c in range(n_chunks):
        sl = slice(c * CHUNK, (c + 1) * CHUNK)
        vn = _rms(v_ref[0, sl, :].astype(F32), gv).astype(BF16)
        mixed = jnp.dot(ws, vn, preferred_element_type=F32) + bs
        a = u_ref[0, sl, :].astype(F32) * mixed
        o_ref[sl, :] = _rms(a, og).astype(o_ref.dtype)


def _gmlp(z, ws, bs, gv, og, n_heads, tb=1024):
    t = z.shape[1]
    return pl.pallas_call(
        functools.partial(_gmlp_kernel, n_chunks=tb // CHUNK),
        grid=(n_heads, t // tb),
        in_specs=[pl.BlockSpec((1, tb, HEAD_DIM), lambda h, i: (h, i, 0)),
                  pl.BlockSpec((1, tb, HEAD_DIM), lambda h, i: (n_heads + h, i, 0)),
                  pl.BlockSpec((1, CHUNK, CHUNK), lambda h, i: (h, 0, 0)),
                  pl.BlockSpec((1, CHUNK, 1), lambda h, i: (h, 0, 0)),
                  pl.BlockSpec((1, 1, HEAD_DIM), lambda h, i: (h, 0, 0)),
                  pl.BlockSpec((1, 1, HEAD_DIM), lambda h, i: (h, 0, 0))],
        out_specs=pl.BlockSpec((tb, HEAD_DIM), lambda h, i: (i, h)),
        out_shape=jax.ShapeDtypeStruct((t, n_heads * HEAD_DIM), BF16),
        compiler_params=_params(("parallel", "parallel"), 24),
        name="gmlp",
    )(z, z, ws.astype(BF16), bs.reshape(n_heads, CHUNK, 1),
      gv.reshape(n_heads, 1, HEAD_DIM), og.reshape(n_heads, 1, HEAD_DIM))


def _na_bias_table(rpb):
    col = np.arange(GRID_W)
    cs = np.clip(col - WIN_COLS // 2, 0, GRID_W - WIN_COLS)
    kc = np.arange(GRID_W)
    valid = (kc[None, :] >= cs[:, None]) & (kc[None, :] < cs[:, None] + WIN_COLS)
    dc = np.clip(kc[None, :] - col[:, None] + (WIN_COLS - 1), 0, 2 * WIN_COLS - 2)
    dr = np.arange(WIN_ROWS)[None, :] - np.arange(WIN_ROWS)[:, None] + (WIN_ROWS - 1)
    tbl = rpb[:, dr][:, :, :, dc]
    tbl = jnp.where(valid[None, None, None], tbl, MASKED)
    h = rpb.shape[0]
    return tbl.transpose(0, 1, 3, 2, 4).reshape(h, WIN_ROWS, GRID_W, WIN_ROWS * GRID_W)


def _na_kernel(q_ref, k_ref, v_ref, bias_ref, qg_ref, kg_ref, og_ref, o_ref, qn_ref, kn_ref,
               *, rows, norm_rows):
    qg = qg_ref[...] * np.float32(HEAD_DIM ** -0.5)
    kg = kg_ref[...]
    og = og_ref[0]

    def norm_body(c, carry):
        sl = pl.ds(pl.multiple_of(c * norm_rows, norm_rows), norm_rows)
        qn_ref[sl, :] = _rms(q_ref[0, sl, :].astype(F32), qg).astype(BF16)
        kn_ref[sl, :] = _rms(k_ref[0, sl, :].astype(F32), kg).astype(BF16)
        return carry

    lax.fori_loop(0, rows * GRID_W // norm_rows, norm_body, 0)

    nkeys = WIN_ROWS * GRID_W

    def row_body(r, carry):
        rs = jnp.clip(r - WIN_ROWS // 2, 0, rows - WIN_ROWS)
        cls = r - rs
        qsl = pl.ds(pl.multiple_of(r * GRID_W, GRID_W), GRID_W)
        ksl = pl.ds(pl.multiple_of(rs * GRID_W, GRID_W), nkeys)
        s = lax.dot_general(qn_ref[qsl, :], kn_ref[ksl, :], (((1,), (1,)), ((), ())),
                            preferred_element_type=F32)
        s = s + bias_ref[0, cls]
        m = jnp.max(s, axis=-1, keepdims=True)
        e = jnp.exp(s - m)
        l = jnp.sum(e, axis=-1, keepdims=True)
        o = jnp.dot(e.astype(BF16), v_ref[0, ksl, :], preferred_element_type=F32) / l
        o_ref[qsl, :] = _rms(o, og).astype(o_ref.dtype)
        return carry

    lax.fori_loop(0, rows, row_body, 0)


def _na(z, bias, qg, kg, og, n_heads, batch, seq, head0):
    rows = seq // GRID_W
    return pl.pallas_call(
        functools.partial(_na_kernel, rows=rows, norm_rows=512),
        grid=(batch, n_heads),
        in_specs=[pl.BlockSpec((1, seq, HEAD_DIM), lambda b, h: (head0 + h, b, 0)),
                  pl.BlockSpec((1, seq, HEAD_DIM), lambda b, h: (head0 + n_heads + h, b, 0)),
                  pl.BlockSpec((1, seq, HEAD_DIM), lambda b, h: (head0 + 2 * n_heads + h, b, 0)),
                  pl.BlockSpec((1, WIN_ROWS, GRID_W, WIN_ROWS * GRID_W), lambda b, h: (h, 0, 0, 0)),
                  pl.BlockSpec((1, HEAD_DIM), lambda b, h: (0, 0)),
                  pl.BlockSpec((1, HEAD_DIM), lambda b, h: (0, 0)),
                  pl.BlockSpec((1, 1, HEAD_DIM), lambda b, h: (h, 0, 0))],
        out_specs=pl.BlockSpec((seq, HEAD_DIM), lambda b, h: (b, h)),
        out_shape=jax.ShapeDtypeStruct((batch * seq, n_heads * HEAD_DIM), BF16),
        scratch_shapes=[pltpu.VMEM((seq, HEAD_DIM), BF16), pltpu.VMEM((seq, HEAD_DIM), BF16)],
        compiler_params=_params(("parallel", "parallel"), 40),
        name="natten",
    )(z, z, z, bias, qg.reshape(1, HEAD_DIM), kg.reshape(1, HEAD_DIM),
      og.reshape(n_heads, 1, HEAD_DIM))


def _outproj_kernel(a_ref, b_ref, wa_ref, wb_ref, x_ref, o_ref):
    acc = jnp.dot(a_ref[...], wa_ref[...], preferred_element_type=F32)
    acc = acc + jnp.dot(b_ref[...], wb_ref[...], preferred_element_type=F32)
    o_ref[...] = x_ref[...] + acc


def _outproj(mix_a, mix_b, w, x, bm=1024, bn=512):
    t, ka = mix_a.shape
    kb = mix_b.shape[1]
    n = w.shape[1]
    assert ka == kb
    return pl.pallas_call(
        _outproj_kernel,
        grid=(t // bm, n // bn),
        in_specs=[pl.BlockSpec((bm, ka), lambda i, j: (i, 0)),
                  pl.BlockSpec((bm, kb), lambda i, j: (i, 0)),
                  pl.BlockSpec((ka, bn), lambda i, j: (0, j)),
                  pl.BlockSpec((kb, bn), lambda i, j: (1, j)),
                  pl.BlockSpec((bm, bn), lambda i, j: (i, j))],
        out_specs=pl.BlockSpec((bm, bn), lambda i, j: (i, j)),
        out_shape=jax.ShapeDtypeStruct((t, n), F32),
        compiler_params=_params(("parallel", "arbitrary"), 48),
        name="outproj",
    )(mix_a, mix_b, w, w, x)


def _ffn_up_kernel(xm_ref, xp_ref, xn_ref, wg_ref, wv_ref, cwg_ref, cbg_ref, cwv_ref, cbv_ref,
                   o_ref, xe_ref, *, bm, blocks_per_seq):
    i = pl.program_id(0)
    j = pl.program_id(1)
    pos = i % blocks_per_seq

    @pl.when(j == 0)
    def _():
        xe_ref[HALO:HALO + bm, :] = xm_ref[...]

        @pl.when(pos == 0)
        def _():
            xe_ref[0:HALO, :] = jnp.zeros((HALO, xe_ref.shape[1]), xe_ref.dtype)

        @pl.when(pos != 0)
        def _():
            xe_ref[0:HALO, :] = xp_ref[...]

        @pl.when(pos == blocks_per_seq - 1)
        def _():
            xe_ref[HALO + bm:, :] = jnp.zeros((HALO, xe_ref.shape[1]), xe_ref.dtype)

        @pl.when(pos != blocks_per_seq - 1)
        def _():
            xe_ref[HALO + bm:, :] = xn_ref[...]

    xe = xe_ref[...]
    rows = bm + 2 * HALO

    def conv(w_ref, cw_ref, cb_ref):
        up = jnp.dot(xe, w_ref[...], preferred_element_type=F32)
        prev = pltpu.roll(up, 1, 0)
        nxt = pltpu.roll(up, rows - 1, 0)
        c = cb_ref[...] + (prev * cw_ref[0:1, :] + up * cw_ref[1:2, :] + nxt * cw_ref[2:3, :])
        return c[HALO:HALO + bm, :]

    gate = conv(wg_ref, cwg_ref, cbg_ref)
    val = conv(wv_ref, cwv_ref, cbv_ref)
    o_ref[...] = (_gelu(gate) * val).astype(o_ref.dtype)


def _ffn_up(xn, wg, wv, cwg, cbg, cwv, cbv, seq, bm=1024, bn=256):
    t, k = xn.shape
    n = wg.shape[1]
    hb = bm // HALO
    n_halo_blocks = t // HALO
    return pl.pallas_call(
        functools.partial(_ffn_up_kernel, bm=bm, blocks_per_seq=seq // bm),
        grid=(t // bm, n // bn),
        in_specs=[pl.BlockSpec((bm, k), lambda i, j: (i, 0)),
                  pl.BlockSpec((HALO, k), lambda i, j: (jnp.maximum(i * hb - 1, 0), 0)),
                  pl.BlockSpec((HALO, k), lambda i, j: (jnp.minimum((i + 1) * hb, n_halo_blocks - 1), 0)),
                  pl.BlockSpec((k, bn), lambda i, j: (0, j)),
                  pl.BlockSpec((k, bn), lambda i, j: (0, j)),
                  pl.BlockSpec((3, bn), lambda i, j: (0, j)),
                  pl.BlockSpec((1, bn), lambda i, j: (0, j)),
                  pl.BlockSpec((3, bn), lambda i, j: (0, j)),
                  pl.BlockSpec((1, bn), lambda i, j: (0, j))],
        out_specs=pl.BlockSpec((bm, bn), lambda i, j: (i, j)),
        out_shape=jax.ShapeDtypeStruct((t, n), BF16),
        scratch_shapes=[pltpu.VMEM((bm + 2 * HALO, k), BF16)],
        compiler_params=_params(("parallel", "arbitrary"), 48),
        name="ffn_up",
    )(xn, xn, xn, wg, wv, cwg, cbg, cwv, cbv)


def _ffn_down_kernel(a_ref, w_ref, h_ref, o_ref):
    o_ref[...] = h_ref[...] + jnp.dot(a_ref[...], w_ref[...], preferred_element_type=F32)


def _ffn_down(act, w, h, bm=512, bn=512):
    t, k = act.shape
    n = w.shape[1]
    return pl.pallas_call(
        _ffn_down_kernel,
        grid=(t // bm, n // bn),
        in_specs=[pl.BlockSpec((bm, k), lambda i, j: (i, 0)),
                  pl.BlockSpec((k, bn), lambda i, j: (0, j)),
                  pl.BlockSpec((bm, bn), lambda i, j: (i, j))],
        out_specs=pl.BlockSpec((bm, bn), lambda i, j: (i, j)),
        out_shape=jax.ShapeDtypeStruct((t, n), F32),
        compiler_params=_params(("parallel", "arbitrary"), 56),
        name="ffn_down",
    )(act, w, h)


def _ple_embed_kernel(p_ref, w_ref, g_ref, o_ref):
    e = jnp.dot(p_ref[...].astype(BF16), w_ref[...], preferred_element_type=F32)
    o_ref[...] = _rms(e, g_ref[...]).astype(o_ref.dtype)


def _ple_embed(p, w, g, bm=512):
    t, k = p.shape
    n = w.shape[1]
    return pl.pallas_call(
        _ple_embed_kernel,
        grid=(t // bm,),
        in_specs=[pl.BlockSpec((bm, k), lambda i: (i, 0)),
                  pl.BlockSpec((k, n), lambda i: (0, 0)),
                  pl.BlockSpec((1, n), lambda i: (0, 0))],
        out_specs=pl.BlockSpec((bm, n), lambda i: (i, 0)),
        out_shape=jax.ShapeDtypeStruct((t, n), BF16),
        compiler_params=_params(("parallel",), 40),
        name="ple_embed",
    )(p, w, g.reshape(1, n))


def _ple_gate_kernel(a_ref, w_ref, e_ref, h_ref, o_ref):
    gate = jax.nn.sigmoid(jnp.dot(a_ref[...], w_ref[...], preferred_element_type=F32))
    o_ref[...] = h_ref[...] + gate * e_ref[...].astype(F32)


def _ple_gate(hn, w, e, h, bm=1024, bn=512):
    t, k = hn.shape
    n = w.shape[1]
    return pl.pallas_call(
        _ple_gate_kernel,
        grid=(t // bm, n // bn),
        in_specs=[pl.BlockSpec((bm, k), lambda i, j: (i, 0)),
                  pl.BlockSpec((k, bn), lambda i, j: (0, j)),
                  pl.BlockSpec((bm, bn), lambda i, j: (i, j)),
                  pl.BlockSpec((bm, bn), lambda i, j: (i, j))],
        out_specs=pl.BlockSpec((bm, bn), lambda i, j: (i, j)),
        out_shape=jax.ShapeDtypeStruct((t, n), F32),
        compiler_params=_params(("parallel", "arbitrary"), 48),
        name="ple_gate",
    )(hn, w, e, h)


def kernel(x, p, norm_mix_g, w_in, gmlp_v_g, gmlp_ws, gmlp_bs, q_norm_g, k_norm_g, na_rpb,
           out_norm_a_g, out_norm_b_g, w_out, norm_ffn_g, w_up, conv_w, conv_b, w_down,
           norm_ple_g, w_ple_gate, w_ple_proj, ple_post_g):
    batch, seq, d_model = x.shape
    t = batch * seq
    depth = w_in.shape[0]
    n_a = gmlp_ws.shape[1]
    n_b = na_rpb.shape[1]
    d_ff = w_down.shape[1]
    h = x.reshape(t, d_model)
    for i in range(depth):
        hn = _rmsnorm(h, norm_mix_g[i])
        z = _inproj(hn, w_in[i].astype(BF16), 2 * n_a * HEAD_DIM)
        mix_a = _gmlp(z, gmlp_ws[i], gmlp_bs[i], gmlp_v_g[i], out_norm_a_g[i], n_a)
        mix_b = _na(z, _na_bias_table(na_rpb[i]), q_norm_g[i], k_norm_g[i], out_norm_b_g[i],
                    n_b, batch, seq, 2 * n_a)
        h = _outproj(mix_a, mix_b, w_out[i].astype(BF16), h)
        xn = _rmsnorm(h, norm_ffn_g[i])
        wu = w_up[i].astype(BF16)
        act = _ffn_up(xn, wu[:, :d_ff], wu[:, d_ff:],
                      conv_w[i][:, :d_ff], conv_b[i][:d_ff].reshape(1, d_ff),
                      conv_w[i][:, d_ff:], conv_b[i][d_ff:].reshape(1, d_ff), seq)
        h = _ffn_down(act, w_down[i].astype(BF16), h)
        hn = _rmsnorm(h, norm_ple_g[i])
        e = _ple_embed(p[i].reshape(t, -1), w_ple_proj[i].astype(BF16), ple_post_g[i])
        h = _ple_gate(hn, w_ple_gate[i].astype(BF16), e, h)
    return h.reshape(batch, seq, d_model)
```

```python
import functools

import numpy as np
import jax
import jax.numpy as jnp
from jax import lax
from jax.experimental import pallas as pl
from jax.experimental.pallas import tpu as pltpu

F32 = jnp.float32
BF16 = jnp.bfloat16

EPS = 1e-6
HEAD_DIM = 128
CHUNK = 128
GRID_W = 64
WIN_ROWS = 8
WIN_COLS = 16
NA_G = 4
NA_WIN = NA_G + WIN_ROWS
MASKED = -1e30
HALO = 16
EPI_ROWS = 32
FFN_BN = 256
MIB = 1 << 20


def _params(semantics, vmem_mib, flags=None):
    return pltpu.CompilerParams(dimension_semantics=semantics,
                                vmem_limit_bytes=vmem_mib * MIB, flags=flags)


def _rms(x, g):
    ms = jnp.mean(x * x, axis=-1, keepdims=True)
    return x * lax.rsqrt(ms + EPS) * g


def _gelu(x):
    return 0.5 * x * (1.0 + lax.erf(x * np.float32(np.sqrt(0.5))))


def _rmsnorm_kernel(x_ref, g_ref, o_ref):
    o_ref[...] = _rms(x_ref[...], g_ref[...]).astype(o_ref.dtype)


def _rmsnorm(x, g, bm=512):
    t, d = x.shape
    return pl.pallas_call(
        _rmsnorm_kernel,
        grid=(t // bm,),
        in_specs=[pl.BlockSpec((bm, d), lambda i: (i, 0)),
                  pl.BlockSpec((1, d), lambda i: (0, 0))],
        out_specs=pl.BlockSpec((bm, d), lambda i: (i, 0)),
        out_shape=jax.ShapeDtypeStruct((t, d), BF16),
        compiler_params=_params(("parallel",), 40),
        name="rmsnorm",
    )(x, g.reshape(1, d))


def _inproj_kernel(a_ref, w_ref, o_ref, *, n_gelu_blocks, heads_per_block):
    acc = jnp.dot(a_ref[...], w_ref[...], preferred_element_type=F32)
    j = pl.program_id(1)

    def store(val):
        for hh in range(heads_per_block):
            o_ref[hh] = val[:, hh * HEAD_DIM:(hh + 1) * HEAD_DIM].astype(o_ref.dtype)

    @pl.when(j < n_gelu_blocks)
    def _():
        store(_gelu(acc))

    @pl.when(j >= n_gelu_blocks)
    def _():
        store(acc)


def _inproj(hn, w, n_gelu_cols, bm=1024, bn=1024):
    t, k = hn.shape
    n = w.shape[1]
    hpb = bn // HEAD_DIM
    return pl.pallas_call(
        functools.partial(_inproj_kernel, n_gelu_blocks=n_gelu_cols // bn, heads_per_block=hpb),
        grid=(t // bm, n // bn),
        in_specs=[pl.BlockSpec((bm, k), lambda i, j: (i, 0)),
                  pl.BlockSpec((k, bn), lambda i, j: (0, j))],
        out_specs=pl.BlockSpec((hpb, bm, HEAD_DIM), lambda i, j: (j, i, 0)),
        out_shape=jax.ShapeDtypeStruct((n // HEAD_DIM, t, HEAD_DIM), BF16),
        compiler_params=_params(("parallel", "arbitrary"), 56),
        name="inproj",
    )(hn, w)


def _gmlp_kernel(u_ref, v_ref, ws_ref, bs_ref, gv_ref, og_ref, o_ref, *, n_chunks):
    ws = ws_ref[0]
    bs = bs_ref[0]
    gv = gv_ref[0]
    og = og_ref[0]
    for c in range(n_chunks):
        sl = slice(c * CHUNK, (c + 1) * CHUNK)
        vn = _rms(v_ref[0, sl, :].astype(F32), gv).astype(BF16)
        mixed = jnp.dot(ws, vn, preferred_element_type=F32) + bs
        a = u_ref[0, sl, :].astype(F32) * mixed
        o_ref[sl, :] = _rms(a, og).astype(o_ref.dtype)


def _gmlp(z, ws, bs, gv, og, n_heads, tb=1024):
    t = z.shape[1]
    return pl.pallas_call(
        functools.partial(_gmlp_kernel, n_chunks=tb // CHUNK),
        grid=(n_heads, t // tb),
        in_specs=[pl.BlockSpec((1, tb, HEAD_DIM), lambda h, i: (h, i, 0)),
                  pl.BlockSpec((1, tb, HEAD_DIM), lambda h, i: (n_heads + h, i, 0)),
                  pl.BlockSpec((1, CHUNK, CHUNK), lambda h, i: (h, 0, 0)),
                  pl.BlockSpec((1, CHUNK, 1), lambda h, i: (h, 0, 0)),
                  pl.BlockSpec((1, 1, HEAD_DIM), lambda h, i: (h, 0, 0)),
                  pl.BlockSpec((1, 1, HEAD_DIM), lambda h, i: (h, 0, 0))],
        out_specs=pl.BlockSpec((tb, HEAD_DIM), lambda h, i: (i, h)),
        out_shape=jax.ShapeDtypeStruct((t, n_heads * HEAD_DIM), BF16),
        compiler_params=_params(("parallel", "parallel"), 24),
        name="gmlp",
    )(z, z, ws.astype(BF16), bs.reshape(n_heads, CHUNK, 1),
      gv.reshape(n_heads, 1, HEAD_DIM), og.reshape(n_heads, 1, HEAD_DIM))


def _na_toeplitz(rpb):
    col = np.arange(GRID_W)
    cs = np.clip(col - WIN_COLS // 2, 0, GRID_W - WIN_COLS)
    kc = np.arange(GRID_W)
    valid = (kc[None, :] >= cs[:, None]) & (kc[None, :] < cs[:, None] + WIN_COLS)
    dc = kc[None, :] - col[:, None] + (WIN_COLS - 1)
    onehot = (dc[:, :, None] == np.arange(2 * WIN_COLS - 1)).astype(np.float32)
    t = jnp.einsum('hrd,ckd->hrck', rpb, onehot, precision=lax.Precision.HIGHEST)
    return jnp.where(valid[None, None], t, MASKED)


def _na_group_geometry(rows):
    def geom(g):
        r0 = g * NA_G
        ws = int(np.clip(r0 - WIN_ROWS // 2, 0, rows - NA_WIN))
        rs = [int(np.clip(r0 + i - WIN_ROWS // 2, 0, rows - WIN_ROWS)) for i in range(NA_G)]
        return r0 - ws, tuple(r - ws for r in rs)

    ng = rows // NA_G
    assert ng >= 3 and all(geom(g) == geom(1) for g in range(1, ng - 1))
    return geom(0), geom(1), geom(ng - 1)


def _na_kernel(q_ref, k_ref, v_ref, t_ref, qg_ref, kg_ref, og_ref, o_ref,
               qn_ref, kn_ref, bias_ref, *, rows, norm_rows):
    qg = qg_ref[...] * np.float32(HEAD_DIM ** -0.5)
    kg = kg_ref[...]
    og = og_ref[0]

    masked_tile = jnp.full((GRID_W, GRID_W), MASKED, F32)
    for cls, (roff, rsoff) in enumerate(_na_group_geometry(rows)):
        for ri in range(NA_G):
            for kp in range(NA_WIN // 2):
                pair = []
                for kr in (2 * kp, 2 * kp + 1):
                    in_window = 0 <= kr - rsoff[ri] < WIN_ROWS
                    dr = kr - roff - ri + (WIN_ROWS - 1)
                    pair.append(t_ref[0, dr] if in_window else masked_tile)
                bias_ref[cls, ri * GRID_W:(ri + 1) * GRID_W, kp * 128:(kp + 1) * 128] = (
                    jnp.concatenate(pair, axis=1))

    def norm_body(c, carry):
        sl = pl.ds(pl.multiple_of(c * norm_rows, norm_rows), norm_rows)
        qn_ref[sl, :] = _rms(q_ref[0, sl, :].astype(F32), qg).astype(BF16)
        kn_ref[sl, :] = _rms(k_ref[0, sl, :].astype(F32), kg).astype(BF16)
        return carry

    lax.fori_loop(0, rows * GRID_W // norm_rows, norm_body, 0)

    ng = rows // NA_G
    nq = NA_G * GRID_W
    nk = NA_WIN * GRID_W

    def group_body(g, carry):
        r0 = g * NA_G
        ws = jnp.clip(r0 - WIN_ROWS // 2, 0, rows - NA_WIN)
        cls = (g > 0).astype(jnp.int32) + (g == ng - 1).astype(jnp.int32)
        qsl = pl.ds(pl.multiple_of(r0 * GRID_W, nq), nq)
        ksl = pl.ds(pl.multiple_of(ws * GRID_W, nq), nk)
        s = lax.dot_general(qn_ref[qsl, :], kn_ref[ksl, :], (((1,), (1,)), ((), ())),
                            preferred_element_type=F32)
        s = s + bias_ref[cls]
        m = jnp.max(s, axis=-1, keepdims=True)
        e = jnp.exp(s - m)
        l = jnp.sum(e, axis=-1, keepdims=True)
        o = jnp.dot(e.astype(BF16), v_ref[0, ksl, :], preferred_element_type=F32) / l
        o_ref[qsl, :] = _rms(o, og).astype(o_ref.dtype)
        return carry

    lax.fori_loop(0, ng, group_body, 0, unroll=2)


def _na(z, toeplitz, qg, kg, og, n_heads, batch, seq, head0):
    rows = seq // GRID_W
    assert rows % NA_G == 0 and (WIN_ROWS // 2) % NA_G == 0
    n_dr = 2 * WIN_ROWS - 1
    return pl.pallas_call(
        functools.partial(_na_kernel, rows=rows, norm_rows=512),
        grid=(batch, n_heads),
        in_specs=[pl.BlockSpec((1, seq, HEAD_DIM), lambda b, h: (head0 + h, b, 0)),
                  pl.BlockSpec((1, seq, HEAD_DIM), lambda b, h: (head0 + n_heads + h, b, 0)),
                  pl.BlockSpec((1, seq, HEAD_DIM), lambda b, h: (head0 + 2 * n_heads + h, b, 0)),
                  pl.BlockSpec((1, n_dr, GRID_W, GRID_W), lambda b, h: (h, 0, 0, 0)),
                  pl.BlockSpec((1, HEAD_DIM), lambda b, h: (0, 0)),
                  pl.BlockSpec((1, HEAD_DIM), lambda b, h: (0, 0)),
                  pl.BlockSpec((1, 1, HEAD_DIM), lambda b, h: (h, 0, 0))],
        out_specs=pl.BlockSpec((seq, HEAD_DIM), lambda b, h: (b, h)),
        out_shape=jax.ShapeDtypeStruct((batch * seq, n_heads * HEAD_DIM), BF16),
        scratch_shapes=[pltpu.VMEM((seq, HEAD_DIM), BF16), pltpu.VMEM((seq, HEAD_DIM), BF16),
                        pltpu.VMEM((3, NA_G * GRID_W, NA_WIN * GRID_W), F32)],
        compiler_params=_params(("parallel", "parallel"), 40),
        name="natten",
    )(z, z, z, toeplitz, qg.reshape(1, HEAD_DIM), kg.reshape(1, HEAD_DIM),
      og.reshape(n_heads, 1, HEAD_DIM))


def _outproj_kernel(a_ref, b_ref, wa_ref, wb_ref, x_ref, o_ref):
    acc = jnp.dot(a_ref[...], wa_ref[...], preferred_element_type=F32)
    acc = acc + jnp.dot(b_ref[...], wb_ref[...], preferred_element_type=F32)
    o_ref[...] = x_ref[...] + acc


def _outproj(mix_a, mix_b, w, x, bm=1024, bn=512):
    t, ka = mix_a.shape
    kb = mix_b.shape[1]
    n = w.shape[1]
    assert ka == kb
    return pl.pallas_call(
        _outproj_kernel,
        grid=(t // bm, n // bn),
        in_specs=[pl.BlockSpec((bm, ka), lambda i, j: (i, 0)),
                  pl.BlockSpec((bm, kb), lambda i, j: (i, 0)),
                  pl.BlockSpec((ka, bn), lambda i, j: (0, j)),
                  pl.BlockSpec((kb, bn), lambda i, j: (1, j)),
                  pl.BlockSpec((bm, bn), lambda i, j: (i, j))],
        out_specs=pl.BlockSpec((bm, bn), lambda i, j: (i, j)),
        out_shape=jax.ShapeDtypeStruct((t, n), F32),
        compiler_params=_params(("parallel", "arbitrary"), 48),
        name="outproj",
    )(mix_a, mix_b, w, w, x)


def _ffn_up_kernel(xm_ref, xp_ref, xn_ref, w_ref, cw_ref, cb_ref,
                   o_ref, xe_ref, up_a_ref, up_b_ref, *, bm, bn, nj, n_blocks, blocks_per_seq):
    s = pl.program_id(0)
    i = jnp.minimum(s // nj, n_blocks // nj - 1)
    pos = i % blocks_per_seq

    @pl.when((s % nj == 0) & (s < n_blocks))
    def _():
        xe_ref[HALO:HALO + bm, :] = xm_ref[...]

        @pl.when(pos == 0)
        def _():
            xe_ref[0:HALO, :] = jnp.zeros((HALO, xe_ref.shape[1]), xe_ref.dtype)

        @pl.when(pos != 0)
        def _():
            xe_ref[0:HALO, :] = xp_ref[...]

        @pl.when(pos == blocks_per_seq - 1)
        def _():
            xe_ref[HALO + bm:, :] = jnp.zeros((HALO, xe_ref.shape[1]), xe_ref.dtype)

        @pl.when(pos != blocks_per_seq - 1)
        def _():
            xe_ref[HALO + bm:, :] = xn_ref[...]

    @pl.when(s == 0)
    def _():
        up_b_ref[...] = jnp.zeros(up_b_ref.shape, up_b_ref.dtype)

    def step(new_ref, old_ref):
        cw = cw_ref[0]
        cb = cb_ref[0]
        for r0 in range(0, bm, EPI_ROWS):
            prev = old_ref[r0 + HALO - 1:r0 + HALO - 1 + EPI_ROWS, :]
            cur = old_ref[r0 + HALO:r0 + HALO + EPI_ROWS, :]
            nxt = old_ref[r0 + HALO + 1:r0 + HALO + 1 + EPI_ROWS, :]
            c = cb + (prev * cw[0:1, :] + cur * cw[1:2, :] + nxt * cw[2:3, :])
            o_ref[r0:r0 + EPI_ROWS, :] = (_gelu(c[:, :bn]) * c[:, bn:]).astype(o_ref.dtype)

        new_ref[...] = jnp.dot(xe_ref[...], w_ref[0], preferred_element_type=F32)

    @pl.when(s % 2 == 0)
    def _():
        step(up_a_ref, up_b_ref)

    @pl.when(s % 2 == 1)
    def _():
        step(up_b_ref, up_a_ref)


def _gate_value_blocks(a, d_ff, bn):
    r = a.shape[0]
    return a.reshape(r, 2, d_ff // bn, bn).transpose(2, 0, 1, 3).reshape(d_ff // bn, r, 2 * bn)


def _ffn_up(xn, w, cw, cb, seq, bm=1024):
    t, k = xn.shape
    nj = w.shape[0]
    bn = w.shape[2] // 2
    n = nj * bn
    ni = t // bm
    n_blocks = ni * nj
    hb = bm // HALO
    n_halo_blocks = t // HALO

    def cur_i(s):
        return jnp.minimum(s // nj, ni - 1)

    def prev_ij(s):
        sp = jnp.maximum(s - 1, 0)
        return sp // nj, sp % nj

    return pl.pallas_call(
        functools.partial(_ffn_up_kernel, bm=bm, bn=bn, nj=nj, n_blocks=n_blocks,
                          blocks_per_seq=seq // bm),
        grid=(n_blocks + 1,),
        in_specs=[pl.BlockSpec((bm, k), lambda s: (cur_i(s), 0)),
                  pl.BlockSpec((HALO, k), lambda s: (jnp.maximum(cur_i(s) * hb - 1, 0), 0)),
                  pl.BlockSpec((HALO, k),
                               lambda s: (jnp.minimum((cur_i(s) + 1) * hb, n_halo_blocks - 1), 0)),
                  pl.BlockSpec((1, k, 2 * bn), lambda s: (s % nj, 0, 0)),
                  pl.BlockSpec((1, 3, 2 * bn), lambda s: (prev_ij(s)[1], 0, 0)),
                  pl.BlockSpec((1, 1, 2 * bn), lambda s: (prev_ij(s)[1], 0, 0))],
        out_specs=pl.BlockSpec((bm, bn), lambda s: prev_ij(s)),
        out_shape=jax.ShapeDtypeStruct((t, n), BF16),
        scratch_shapes=[pltpu.VMEM((bm + 2 * HALO, k), BF16),
                        pltpu.VMEM((bm + 2 * HALO, 2 * bn), F32),
                        pltpu.VMEM((bm + 2 * HALO, 2 * bn), F32)],
        compiler_params=_params(("arbitrary",), 48),
        name="ffn_up",
    )(xn, xn, xn, w, cw, cb)


def _ffn_down_kernel(a_ref, w_ref, h_ref, o_ref):
    o_ref[...] = h_ref[...] + jnp.dot(a_ref[...], w_ref[...], preferred_element_type=F32)


def _ffn_down(act, w, h, bm=512, bn=512):
    t, k = act.shape
    n = w.shape[1]
    return pl.pallas_call(
        _ffn_down_kernel,
        grid=(t // bm, n // bn),
        in_specs=[pl.BlockSpec((bm, k), lambda i, j: (i, 0)),
                  pl.BlockSpec((k, bn), lambda i, j: (0, j)),
                  pl.BlockSpec((bm, bn), lambda i, j: (i, j))],
        out_specs=pl.BlockSpec((bm, bn), lambda i, j: (i, j)),
        out_shape=jax.ShapeDtypeStruct((t, n), F32),
        compiler_params=_params(("parallel", "arbitrary"), 56),
        name="ffn_down",
    )(act, w, h)


def _ple_embed_kernel(p_ref, w_ref, g_ref, o_ref):
    e = jnp.dot(p_ref[...].astype(BF16), w_ref[...], preferred_element_type=F32)
    o_ref[...] = _rms(e, g_ref[...]).astype(o_ref.dtype)


def _ple_embed(p, w, g, bm=512):
    t, k = p.shape
    n = w.shape[1]
    return pl.pallas_call(
        _ple_embed_kernel,
        grid=(t // bm,),
        in_specs=[pl.BlockSpec((bm, k), lambda i: (i, 0)),
                  pl.BlockSpec((k, n), lambda i: (0, 0)),
                  pl.BlockSpec((1, n), lambda i: (0, 0))],
        out_specs=pl.BlockSpec((bm, n), lambda i: (i, 0)),
        out_shape=jax.ShapeDtypeStruct((t, n), BF16),
        compiler_params=_params(("parallel",), 40),
        name="ple_embed",
    )(p, w, g.reshape(1, n))


def _ple_gate_kernel(a_ref, w_ref, e_ref, h_ref, o_ref):
    gate = jax.nn.sigmoid(jnp.dot(a_ref[...], w_ref[...], preferred_element_type=F32))
    o_ref[...] = h_ref[...] + gate * e_ref[...].astype(F32)


def _ple_gate(hn, w, e, h, bm=1024, bn=512):
    t, k = hn.shape
    n = w.shape[1]
    return pl.pallas_call(
        _ple_gate_kernel,
        grid=(t // bm, n // bn),
        in_specs=[pl.BlockSpec((bm, k), lambda i, j: (i, 0)),
                  pl.BlockSpec((k, bn), lambda i, j: (0, j)),
                  pl.BlockSpec((bm, bn), lambda i, j: (i, j)),
                  pl.BlockSpec((bm, bn), lambda i, j: (i, j))],
        out_specs=pl.BlockSpec((bm, bn), lambda i, j: (i, j)),
        out_shape=jax.ShapeDtypeStruct((t, n), F32),
        compiler_params=_params(("parallel", "arbitrary"), 48),
        name="ple_gate",
    )(hn, w, e, h)


def kernel(x, p, norm_mix_g, w_in, gmlp_v_g, gmlp_ws, gmlp_bs, q_norm_g, k_norm_g, na_rpb,
           out_norm_a_g, out_norm_b_g, w_out, norm_ffn_g, w_up, conv_w, conv_b, w_down,
           norm_ple_g, w_ple_gate, w_ple_proj, ple_post_g):
    batch, seq, d_model = x.shape
    t = batch * seq
    depth = w_in.shape[0]
    n_a = gmlp_ws.shape[1]
    n_b = na_rpb.shape[1]
    d_ff = w_down.shape[1]
    h = x.reshape(t, d_model)
    for i in range(depth):
        hn = _rmsnorm(h, norm_mix_g[i])
        z = _inproj(hn, w_in[i].astype(BF16), 2 * n_a * HEAD_DIM)
        mix_a = _gmlp(z, gmlp_ws[i], gmlp_bs[i], gmlp_v_g[i], out_norm_a_g[i], n_a)
        mix_b = _na(z, _na_toeplitz(na_rpb[i]), q_norm_g[i], k_norm_g[i], out_norm_b_g[i],
                    n_b, batch, seq, 2 * n_a)
        h = _outproj(mix_a, mix_b, w_out[i].astype(BF16), h)
        xn = _rmsnorm(h, norm_ffn_g[i])
        act = _ffn_up(xn, _gate_value_blocks(w_up[i], d_ff, FFN_BN).astype(BF16),
                      _gate_value_blocks(conv_w[i], d_ff, FFN_BN),
                      _gate_value_blocks(conv_b[i].reshape(1, -1), d_ff, FFN_BN), seq)
        h = _ffn_down(act, w_down[i].astype(BF16), h)
        hn = _rmsnorm(h, norm_ple_g[i])
        e = _ple_embed(p[i].reshape(t, -1), w_ple_proj[i].astype(BF16), ple_post_g[i])
        h = _ple_gate(hn, w_ple_gate[i].astype(BF16), e, h)
    return h.reshape(batch, seq, d_model)
```

```python
import functools

import numpy as np
import jax
import jax.numpy as jnp
from jax import lax
from jax.experimental import pallas as pl
from jax.experimental.pallas import tpu as pltpu

F32 = jnp.float32
BF16 = jnp.bfloat16

EPS = 1e-6
HEAD_DIM = 128
CHUNK = 128
GRID_W = 64
WIN_ROWS = 8
WIN_COLS = 16
NA_G = 4
NA_WIN = NA_G + WIN_ROWS
MASKED = -1e30
HALO = 16
EPI_ROWS = 8
MIB = 1 << 20


def _params(semantics, vmem_mib, flags=None):
    return pltpu.CompilerParams(dimension_semantics=semantics,
                                vmem_limit_bytes=vmem_mib * MIB, flags=flags)


def _rms(x, g):
    ms = jnp.mean(x * x, axis=-1, keepdims=True)
    return x * lax.rsqrt(ms + EPS) * g


def _gelu(x):
    return 0.5 * x * (1.0 + lax.erf(x * np.float32(np.sqrt(0.5))))


def _rmsnorm_kernel(x_ref, g_ref, o_ref):
    o_ref[...] = _rms(x_ref[...], g_ref[...]).astype(o_ref.dtype)


def _rmsnorm(x, g, bm=512):
    t, d = x.shape
    return pl.pallas_call(
        _rmsnorm_kernel,
        grid=(t // bm,),
        in_specs=[pl.BlockSpec((bm, d), lambda i: (i, 0)),
                  pl.BlockSpec((1, d), lambda i: (0, 0))],
        out_specs=pl.BlockSpec((bm, d), lambda i: (i, 0)),
        out_shape=jax.ShapeDtypeStruct((t, d), BF16),
        compiler_params=_params(("parallel",), 40),
        name="rmsnorm",
    )(x, g.reshape(1, d))


def _inproj_kernel(a_ref, w_ref, o_ref, *, n_gelu_blocks, heads_per_block):
    acc = jnp.dot(a_ref[...], w_ref[...], preferred_element_type=F32)
    j = pl.program_id(1)

    def store(val):
        for hh in range(heads_per_block):
            o_ref[hh] = val[:, hh * HEAD_DIM:(hh + 1) * HEAD_DIM].astype(o_ref.dtype)

    @pl.when(j < n_gelu_blocks)
    def _():
        store(_gelu(acc))

    @pl.when(j >= n_gelu_blocks)
    def _():
        store(acc)


def _inproj(hn, w, n_gelu_cols, bm=1024, bn=1024):
    t, k = hn.shape
    n = w.shape[1]
    hpb = bn // HEAD_DIM
    return pl.pallas_call(
        functools.partial(_inproj_kernel, n_gelu_blocks=n_gelu_cols // bn, heads_per_block=hpb),
        grid=(t // bm, n // bn),
        in_specs=[pl.BlockSpec((bm, k), lambda i, j: (i, 0)),
                  pl.BlockSpec((k, bn), lambda i, j: (0, j))],
        out_specs=pl.BlockSpec((hpb, bm, HEAD_DIM), lambda i, j: (j, i, 0)),
        out_shape=jax.ShapeDtypeStruct((n // HEAD_DIM, t, HEAD_DIM), BF16),
        compiler_params=_params(("parallel", "arbitrary"), 56),
        name="inproj",
    )(hn, w)


def _gmlp_kernel(u_ref, v_ref, ws_ref, bs_ref, gv_ref, og_ref, o_ref, *, n_chunks):
    ws = ws_ref[0]
    bs = bs_ref[0]
    gv = gv_ref[0]
    og = og_ref[0]
    for c in range(n_chunks):
        sl = slice(c * CHUNK, (c + 1) * CHUNK)
        vn = _rms(v_ref[0, sl, :].astype(F32), gv).astype(BF16)
        mixed = jnp.dot(ws, vn, preferred_element_type=F32) + bs
        a = u_ref[0, sl, :].astype(F32) * mixed
        o_ref[sl, :] = _rms(a, og).astype(o_ref.dtype)


def _gmlp(z, ws, bs, gv, og, n_heads, tb=1024):
    t = z.shape[1]
    return pl.pallas_call(
        functools.partial(_gmlp_kernel, n_chunks=tb // CHUNK),
        grid=(n_heads, t // tb),
        in_specs=[pl.BlockSpec((1, tb, HEAD_DIM), lambda h, i: (h, i, 0)),
                  pl.BlockSpec((1, tb, HEAD_DIM), lambda h, i: (n_heads + h, i, 0)),
                  pl.BlockSpec((1, CHUNK, CHUNK), lambda h, i: (h, 0, 0)),
                  pl.BlockSpec((1, CHUNK, 1), lambda h, i: (h, 0, 0)),
                  pl.BlockSpec((1, 1, HEAD_DIM), lambda h, i: (h, 0, 0)),
                  pl.BlockSpec((1, 1, HEAD_DIM), lambda h, i: (h, 0, 0))],
        out_specs=pl.BlockSpec((tb, HEAD_DIM), lambda h, i: (i, h)),
        out_shape=jax.ShapeDtypeStruct((t, n_heads * HEAD_DIM), BF16),
        compiler_params=_params(("parallel", "parallel"), 24),
        name="gmlp",
    )(z, z, ws.astype(BF16), bs.reshape(n_heads, CHUNK, 1),
      gv.reshape(n_heads, 1, HEAD_DIM), og.reshape(n_heads, 1, HEAD_DIM))


def _na_toeplitz(rpb):
    col = np.arange(GRID_W)
    cs = np.clip(col - WIN_COLS // 2, 0, GRID_W - WIN_COLS)
    kc = np.arange(GRID_W)
    valid = (kc[None, :] >= cs[:, None]) & (kc[None, :] < cs[:, None] + WIN_COLS)
    dc = kc[None, :] - col[:, None] + (WIN_COLS - 1)
    onehot = (dc[:, :, None] == np.arange(2 * WIN_COLS - 1)).astype(np.float32)
    t = jnp.einsum('hrd,ckd->hrck', rpb, onehot, precision=lax.Precision.HIGHEST)
    return jnp.where(valid[None, None], t, MASKED)


def _na_group_geometry(rows):
    def geom(g):
        r0 = g * NA_G
        ws = int(np.clip(r0 - WIN_ROWS // 2, 0, rows - NA_WIN))
        rs = [int(np.clip(r0 + i - WIN_ROWS // 2, 0, rows - WIN_ROWS)) for i in range(NA_G)]
        return r0 - ws, tuple(r - ws for r in rs)

    ng = rows // NA_G
    assert ng >= 3 and all(geom(g) == geom(1) for g in range(1, ng - 1))
    return geom(0), geom(1), geom(ng - 1)


def _na_kernel(q_ref, k_ref, v_ref, t_ref, qg_ref, kg_ref, og_ref, o_ref,
               qn_ref, kn_ref, bias_ref, *, rows, norm_rows):
    qg = qg_ref[...] * np.float32(HEAD_DIM ** -0.5)
    kg = kg_ref[...]
    og = og_ref[0]

    masked_tile = jnp.full((GRID_W, GRID_W), MASKED, F32)
    for cls, (roff, rsoff) in enumerate(_na_group_geometry(rows)):
        for ri in range(NA_G):
            for kp in range(NA_WIN // 2):
                pair = []
                for kr in (2 * kp, 2 * kp + 1):
                    in_window = 0 <= kr - rsoff[ri] < WIN_ROWS
                    dr = kr - roff - ri + (WIN_ROWS - 1)
                    pair.append(t_ref[0, dr] if in_window else masked_tile)
                bias_ref[cls, ri * GRID_W:(ri + 1) * GRID_W, kp * 128:(kp + 1) * 128] = (
                    jnp.concatenate(pair, axis=1))

    def norm_body(c, carry):
        sl = pl.ds(pl.multiple_of(c * norm_rows, norm_rows), norm_rows)
        qn_ref[sl, :] = _rms(q_ref[0, sl, :].astype(F32), qg).astype(BF16)
        kn_ref[sl, :] = _rms(k_ref[0, sl, :].astype(F32), kg).astype(BF16)
        return carry

    lax.fori_loop(0, rows * GRID_W // norm_rows, norm_body, 0)

    ng = rows // NA_G
    nq = NA_G * GRID_W
    nk = NA_WIN * GRID_W

    def group_body(g, carry):
        r0 = g * NA_G
        ws = jnp.clip(r0 - WIN_ROWS // 2, 0, rows - NA_WIN)
        cls = (g > 0).astype(jnp.int32) + (g == ng - 1).astype(jnp.int32)
        qsl = pl.ds(pl.multiple_of(r0 * GRID_W, nq), nq)
        ksl = pl.ds(pl.multiple_of(ws * GRID_W, nq), nk)
        s = lax.dot_general(qn_ref[qsl, :], kn_ref[ksl, :], (((1,), (1,)), ((), ())),
                            preferred_element_type=F32)
        s = s + bias_ref[cls]
        m = jnp.max(s, axis=-1, keepdims=True)
        e = jnp.exp(s - m)
        l = jnp.sum(e, axis=-1, keepdims=True)
        o = jnp.dot(e.astype(BF16), v_ref[0, ksl, :], preferred_element_type=F32) / l
        o_ref[qsl, :] = _rms(o, og).astype(o_ref.dtype)
        return carry

    lax.fori_loop(0, ng, group_body, 0, unroll=2)


def _na(z, toeplitz, qg, kg, og, n_heads, batch, seq, head0):
    rows = seq // GRID_W
    assert rows % NA_G == 0 and (WIN_ROWS // 2) % NA_G == 0
    n_dr = 2 * WIN_ROWS - 1
    return pl.pallas_call(
        functools.partial(_na_kernel, rows=rows, norm_rows=512),
        grid=(batch, n_heads),
        in_specs=[pl.BlockSpec((1, seq, HEAD_DIM), lambda b, h: (head0 + h, b, 0)),
                  pl.BlockSpec((1, seq, HEAD_DIM), lambda b, h: (head0 + n_heads + h, b, 0)),
                  pl.BlockSpec((1, seq, HEAD_DIM), lambda b, h: (head0 + 2 * n_heads + h, b, 0)),
                  pl.BlockSpec((1, n_dr, GRID_W, GRID_W), lambda b, h: (h, 0, 0, 0)),
                  pl.BlockSpec((1, HEAD_DIM), lambda b, h: (0, 0)),
                  pl.BlockSpec((1, HEAD_DIM), lambda b, h: (0, 0)),
                  pl.BlockSpec((1, 1, HEAD_DIM), lambda b, h: (h, 0, 0))],
        out_specs=pl.BlockSpec((seq, HEAD_DIM), lambda b, h: (b, h)),
        out_shape=jax.ShapeDtypeStruct((batch * seq, n_heads * HEAD_DIM), BF16),
        scratch_shapes=[pltpu.VMEM((seq, HEAD_DIM), BF16), pltpu.VMEM((seq, HEAD_DIM), BF16),
                        pltpu.VMEM((3, NA_G * GRID_W, NA_WIN * GRID_W), F32)],
        compiler_params=_params(("parallel", "parallel"), 40),
        name="natten",
    )(z, z, z, toeplitz, qg.reshape(1, HEAD_DIM), kg.reshape(1, HEAD_DIM),
      og.reshape(n_heads, 1, HEAD_DIM))


def _outproj_kernel(a_ref, b_ref, wa_ref, wb_ref, x_ref, o_ref):
    acc = jnp.dot(a_ref[...], wa_ref[...], preferred_element_type=F32)
    acc = acc + jnp.dot(b_ref[...], wb_ref[...], preferred_element_type=F32)
    o_ref[...] = x_ref[...] + acc


def _outproj(mix_a, mix_b, w, x, bm=1024, bn=512):
    t, ka = mix_a.shape
    kb = mix_b.shape[1]
    n = w.shape[1]
    assert ka == kb
    return pl.pallas_call(
        _outproj_kernel,
        grid=(t // bm, n // bn),
        in_specs=[pl.BlockSpec((bm, ka), lambda i, j: (i, 0)),
                  pl.BlockSpec((bm, kb), lambda i, j: (i, 0)),
                  pl.BlockSpec((ka, bn), lambda i, j: (0, j)),
                  pl.BlockSpec((kb, bn), lambda i, j: (1, j)),
                  pl.BlockSpec((bm, bn), lambda i, j: (i, j))],
        out_specs=pl.BlockSpec((bm, bn), lambda i, j: (i, j)),
        out_shape=jax.ShapeDtypeStruct((t, n), F32),
        compiler_params=_params(("parallel", "arbitrary"), 48),
        name="outproj",
    )(mix_a, mix_b, w, w, x)


def _dependent_zero(x):
    bits = pltpu.bitcast(x, jnp.uint32)
    return pltpu.bitcast((bits >> 16) >> 16, F32)


def _ffn_up_kernel(xm_ref, xp_ref, xn_ref, wg_ref, wv_ref, cwg_ref, cwv_ref, cbg_ref, cbv_ref,
                   o_ref, xe_ref, up_a_ref, up_b_ref, *, bm, bn, nj, n_blocks, blocks_per_seq):
    s = pl.program_id(0)
    i = jnp.minimum(s // nj, n_blocks // nj - 1)
    pos = i % blocks_per_seq

    @pl.when((s % nj == 0) & (s < n_blocks))
    def _():
        xe_ref[HALO:HALO + bm, :] = xm_ref[...]

        @pl.when(pos == 0)
        def _():
            xe_ref[0:HALO, :] = jnp.zeros((HALO, xe_ref.shape[1]), xe_ref.dtype)

        @pl.when(pos != 0)
        def _():
            xe_ref[0:HALO, :] = xp_ref[...]

        @pl.when(pos == blocks_per_seq - 1)
        def _():
            xe_ref[HALO + bm:, :] = jnp.zeros((HALO, xe_ref.shape[1]), xe_ref.dtype)

        @pl.when(pos != blocks_per_seq - 1)
        def _():
            xe_ref[HALO + bm:, :] = xn_ref[...]

    @pl.when(s == 0)
    def _():
        up_b_ref[...] = jnp.zeros(up_b_ref.shape, up_b_ref.dtype)

    def step(new_ref, old_ref):
        cw = jnp.concatenate([cwg_ref[...], cwv_ref[...]], axis=1)
        cb = jnp.concatenate([cbg_ref[...], cbv_ref[...]], axis=1)
        w_cur = cw[1:2, :]
        for r0 in range(0, bm, EPI_ROWS):
            slab = old_ref[r0 + HALO - 8:r0 + HALO + EPI_ROWS + 8, :]
            prev = pltpu.roll(slab, 1, 0)[8:8 + EPI_ROWS]
            cur = slab[8:8 + EPI_ROWS]
            nxt = pltpu.roll(slab, EPI_ROWS + 15, 0)[8:8 + EPI_ROWS]
            c = cb + (prev * cw[0:1, :] + cur * w_cur + nxt * cw[2:3, :])
            act = _gelu(c[:, :bn]) * c[:, bn:]
            o_ref[r0:r0 + EPI_ROWS, :] = act.astype(o_ref.dtype)
            zero = _dependent_zero(act[0:1, 0:128])
            w_cur = cw[1:2, :] + jnp.concatenate([zero] * (2 * bn // 128), axis=1)

        xe = xe_ref[...]
        new_ref[:, :bn] = jnp.dot(xe, wg_ref[...], preferred_element_type=F32)
        new_ref[:, bn:] = jnp.dot(xe, wv_ref[...], preferred_element_type=F32)

    @pl.when(s % 2 == 0)
    def _():
        step(up_a_ref, up_b_ref)

    @pl.when(s % 2 == 1)
    def _():
        step(up_b_ref, up_a_ref)


def _ffn_up(xn, w, cw, cb, seq, bm=1024, bn=256):
    t, k = xn.shape
    n = w.shape[1] // 2
    ni, nj = t // bm, n // bn
    n_blocks = ni * nj
    hb = bm // HALO
    n_halo_blocks = t // HALO

    def cur_i(s):
        return jnp.minimum(s // nj, ni - 1)

    def prev_ij(s):
        sp = jnp.maximum(s - 1, 0)
        return sp // nj, sp % nj

    return pl.pallas_call(
        functools.partial(_ffn_up_kernel, bm=bm, bn=bn, nj=nj, n_blocks=n_blocks,
                          blocks_per_seq=seq // bm),
        grid=(n_blocks + 1,),
        in_specs=[pl.BlockSpec((bm, k), lambda s: (cur_i(s), 0)),
                  pl.BlockSpec((HALO, k), lambda s: (jnp.maximum(cur_i(s) * hb - 1, 0), 0)),
                  pl.BlockSpec((HALO, k),
                               lambda s: (jnp.minimum((cur_i(s) + 1) * hb, n_halo_blocks - 1), 0)),
                  pl.BlockSpec((k, bn), lambda s: (0, s % nj)),
                  pl.BlockSpec((k, bn), lambda s: (0, nj + s % nj)),
                  pl.BlockSpec((3, bn), lambda s: (0, prev_ij(s)[1])),
                  pl.BlockSpec((3, bn), lambda s: (0, nj + prev_ij(s)[1])),
                  pl.BlockSpec((1, bn), lambda s: (0, prev_ij(s)[1])),
                  pl.BlockSpec((1, bn), lambda s: (0, nj + prev_ij(s)[1]))],
        out_specs=pl.BlockSpec((bm, bn), lambda s: prev_ij(s)),
        out_shape=jax.ShapeDtypeStruct((t, n), BF16),
        scratch_shapes=[pltpu.VMEM((bm + 2 * HALO, k), BF16),
                        pltpu.VMEM((bm + 2 * HALO, 2 * bn), F32),
                        pltpu.VMEM((bm + 2 * HALO, 2 * bn), F32)],
        compiler_params=_params(("arbitrary",), 48),
        name="ffn_up",
    )(xn, xn, xn, w, w, cw, cw, cb, cb)


def _ffn_down_kernel(a_ref, w_ref, h_ref, o_ref):
    o_ref[...] = h_ref[...] + jnp.dot(a_ref[...], w_ref[...], preferred_element_type=F32)


def _ffn_down(act, w, h, bm=512, bn=512):
    t, k = act.shape
    n = w.shape[1]
    return pl.pallas_call(
        _ffn_down_kernel,
        grid=(t // bm, n // bn),
        in_specs=[pl.BlockSpec((bm, k), lambda i, j: (i, 0)),
                  pl.BlockSpec((k, bn), lambda i, j: (0, j)),
                  pl.BlockSpec((bm, bn), lambda i, j: (i, j))],
        out_specs=pl.BlockSpec((bm, bn), lambda i, j: (i, j)),
        out_shape=jax.ShapeDtypeStruct((t, n), F32),
        compiler_params=_params(("parallel", "arbitrary"), 56),
        name="ffn_down",
    )(act, w, h)


def _ple_embed_kernel(p_ref, w_ref, g_ref, o_ref):
    e = jnp.dot(p_ref[...].astype(BF16), w_ref[...], preferred_element_type=F32)
    o_ref[...] = _rms(e, g_ref[...]).astype(o_ref.dtype)


def _ple_embed(p, w, g, bm=512):
    t, k = p.shape
    n = w.shape[1]
    return pl.pallas_call(
        _ple_embed_kernel,
        grid=(t // bm,),
        in_specs=[pl.BlockSpec((bm, k), lambda i: (i, 0)),
                  pl.BlockSpec((k, n), lambda i: (0, 0)),
                  pl.BlockSpec((1, n), lambda i: (0, 0))],
        out_specs=pl.BlockSpec((bm, n), lambda i: (i, 0)),
        out_shape=jax.ShapeDtypeStruct((t, n), BF16),
        compiler_params=_params(("parallel",), 40),
        name="ple_embed",
    )(p, w, g.reshape(1, n))


def _ple_gate_kernel(a_ref, w_ref, e_ref, h_ref, o_ref):
    gate = jax.nn.sigmoid(jnp.dot(a_ref[...], w_ref[...], preferred_element_type=F32))
    o_ref[...] = h_ref[...] + gate * e_ref[...].astype(F32)


def _ple_gate(hn, w, e, h, bm=1024, bn=512):
    t, k = hn.shape
    n = w.shape[1]
    return pl.pallas_call(
        _ple_gate_kernel,
        grid=(t // bm, n // bn),
        in_specs=[pl.BlockSpec((bm, k), lambda i, j: (i, 0)),
                  pl.BlockSpec((k, bn), lambda i, j: (0, j)),
                  pl.BlockSpec((bm, bn), lambda i, j: (i, j)),
                  pl.BlockSpec((bm, bn), lambda i, j: (i, j))],
        out_specs=pl.BlockSpec((bm, bn), lambda i, j: (i, j)),
        out_shape=jax.ShapeDtypeStruct((t, n), F32),
        compiler_params=_params(("parallel", "arbitrary"), 48),
        name="ple_gate",
    )(hn, w, e, h)


def kernel(x, p, norm_mix_g, w_in, gmlp_v_g, gmlp_ws, gmlp_bs, q_norm_g, k_norm_g, na_rpb,
           out_norm_a_g, out_norm_b_g, w_out, norm_ffn_g, w_up, conv_w, conv_b, w_down,
           norm_ple_g, w_ple_gate, w_ple_proj, ple_post_g):
    batch, seq, d_model = x.shape
    t = batch * seq
    depth = w_in.shape[0]
    n_a = gmlp_ws.shape[1]
    n_b = na_rpb.shape[1]
    h = x.reshape(t, d_model)
    for i in range(depth):
        hn = _rmsnorm(h, norm_mix_g[i])
        z = _inproj(hn, w_in[i].astype(BF16), 2 * n_a * HEAD_DIM)
        mix_a = _gmlp(z, gmlp_ws[i], gmlp_bs[i], gmlp_v_g[i], out_norm_a_g[i], n_a)
        mix_b = _na(z, _na_toeplitz(na_rpb[i]), q_norm_g[i], k_norm_g[i], out_norm_b_g[i],
                    n_b, batch, seq, 2 * n_a)
        h = _outproj(mix_a, mix_b, w_out[i].astype(BF16), h)
        xn = _rmsnorm(h, norm_ffn_g[i])
        act = _ffn_up(xn, w_up[i].astype(BF16), conv_w[i], conv_b[i].reshape(1, -1), seq)
        h = _ffn_down(act, w_down[i].astype(BF16), h)
        hn = _rmsnorm(h, norm_ple_g[i])
        e = _ple_embed(p[i].reshape(t, -1), w_ple_proj[i].astype(BF16), ple_post_g[i])
        h = _ple_gate(hn, w_ple_gate[i].astype(BF16), e, h)
    return h.reshape(batch, seq, d_model)
```

```python
import functools

import numpy as np
import jax
import jax.numpy as jnp
from jax import lax
from jax.experimental import pallas as pl
from jax.experimental.pallas import tpu as pltpu

F32 = jnp.float32
BF16 = jnp.bfloat16

EPS = 1e-6
HEAD_DIM = 128
CHUNK = 128
GMLP_GROUP = 8
GRID_W = 64
WIN_ROWS = 8
WIN_COLS = 16
NA_G = 4
NA_WIN = NA_G + WIN_ROWS
MASKED = -1e30
HALO = 16
EPI_ROWS = 8
MIB = 1 << 20


def _params(semantics, vmem_mib, flags=None):
    return pltpu.CompilerParams(dimension_semantics=semantics,
                                vmem_limit_bytes=vmem_mib * MIB, flags=flags)


def _rms(x, g):
    ms = jnp.mean(x * x, axis=-1, keepdims=True)
    return x * lax.rsqrt(ms + EPS) * g


def _gelu(x):
    return 0.5 * x * (1.0 + lax.erf(x * np.float32(np.sqrt(0.5))))


def _rmsnorm_kernel(x_ref, g_ref, o_ref):
    o_ref[...] = _rms(x_ref[...], g_ref[...]).astype(o_ref.dtype)


def _rmsnorm(x, g, bm=512):
    t, d = x.shape
    return pl.pallas_call(
        _rmsnorm_kernel,
        grid=(t // bm,),
        in_specs=[pl.BlockSpec((bm, d), lambda i: (i, 0)),
                  pl.BlockSpec((1, d), lambda i: (0, 0))],
        out_specs=pl.BlockSpec((bm, d), lambda i: (i, 0)),
        out_shape=jax.ShapeDtypeStruct((t, d), BF16),
        compiler_params=_params(("parallel",), 40),
        name="rmsnorm",
    )(x, g.reshape(1, d))


def _inproj_kernel(a_ref, w_ref, o_ref, *, n_gelu_blocks, heads_per_block):
    acc = jnp.dot(a_ref[...], w_ref[...], preferred_element_type=F32)
    j = pl.program_id(1)

    def store(val):
        for hh in range(heads_per_block):
            o_ref[hh] = val[:, hh * HEAD_DIM:(hh + 1) * HEAD_DIM].astype(o_ref.dtype)

    @pl.when(j < n_gelu_blocks)
    def _():
        store(_gelu(acc))

    @pl.when(j >= n_gelu_blocks)
    def _():
        store(acc)


def _inproj(hn, w, n_gelu_cols, bm=1024, bn=1024):
    t, k = hn.shape
    n = w.shape[1]
    hpb = bn // HEAD_DIM
    return pl.pallas_call(
        functools.partial(_inproj_kernel, n_gelu_blocks=n_gelu_cols // bn, heads_per_block=hpb),
        grid=(t // bm, n // bn),
        in_specs=[pl.BlockSpec((bm, k), lambda i, j: (i, 0)),
                  pl.BlockSpec((k, bn), lambda i, j: (0, j))],
        out_specs=pl.BlockSpec((hpb, bm, HEAD_DIM), lambda i, j: (j, i, 0)),
        out_shape=jax.ShapeDtypeStruct((n // HEAD_DIM, t, HEAD_DIM), BF16),
        compiler_params=_params(("parallel", "arbitrary"), 56),
        name="inproj",
    )(hn, w)


def _gmlp_kernel(u_ref, v_ref, ws_ref, bs_ref, gv_ref, og_ref, o_ref, *, n_groups):
    ws = ws_ref[0]
    bs = bs_ref[0]
    gv = gv_ref[0]
    og = og_ref[0]

    def group_body(gi, carry):
        base = pl.multiple_of(gi * (GMLP_GROUP * CHUNK), GMLP_GROUP * CHUNK)
        for c in range(GMLP_GROUP):
            sl = pl.ds(base + c * CHUNK, CHUNK)
            vn = _rms(v_ref[0, sl, :].astype(F32), gv).astype(BF16)
            mixed = jnp.dot(ws, vn, preferred_element_type=F32) + bs
            a = u_ref[0, sl, :].astype(F32) * mixed
            o_ref[sl, :] = _rms(a, og).astype(o_ref.dtype)
        return carry

    lax.fori_loop(0, n_groups, group_body, 0)


def _gmlp(z, ws, bs, gv, og, n_heads, tb=8192):
    t = z.shape[1]
    return pl.pallas_call(
        functools.partial(_gmlp_kernel, n_groups=tb // (GMLP_GROUP * CHUNK)),
        grid=(n_heads, t // tb),
        in_specs=[pl.BlockSpec((1, tb, HEAD_DIM), lambda h, i: (h, i, 0)),
                  pl.BlockSpec((1, tb, HEAD_DIM), lambda h, i: (n_heads + h, i, 0)),
                  pl.BlockSpec((1, CHUNK, CHUNK), lambda h, i: (h, 0, 0)),
                  pl.BlockSpec((1, CHUNK, 1), lambda h, i: (h, 0, 0)),
                  pl.BlockSpec((1, 1, HEAD_DIM), lambda h, i: (h, 0, 0)),
                  pl.BlockSpec((1, 1, HEAD_DIM), lambda h, i: (h, 0, 0))],
        out_specs=pl.BlockSpec((tb, HEAD_DIM), lambda h, i: (i, h)),
        out_shape=jax.ShapeDtypeStruct((t, n_heads * HEAD_DIM), BF16),
        compiler_params=_params(("parallel", "parallel"), 24),
        name="gmlp",
    )(z, z, ws.astype(BF16), bs.reshape(n_heads, CHUNK, 1),
      gv.reshape(n_heads, 1, HEAD_DIM), og.reshape(n_heads, 1, HEAD_DIM))


def _na_toeplitz(rpb):
    col = np.arange(GRID_W)
    cs = np.clip(col - WIN_COLS // 2, 0, GRID_W - WIN_COLS)
    kc = np.arange(GRID_W)
    valid = (kc[None, :] >= cs[:, None]) & (kc[None, :] < cs[:, None] + WIN_COLS)
    dc = kc[None, :] - col[:, None] + (WIN_COLS - 1)
    onehot = (dc[:, :, None] == np.arange(2 * WIN_COLS - 1)).astype(np.float32)
    t = jnp.einsum('hrd,ckd->hrck', rpb, onehot, precision=lax.Precision.HIGHEST)
    return jnp.where(valid[None, None], t, MASKED)


def _na_group_geometry(rows):
    def geom(g):
        r0 = g * NA_G
        ws = int(np.clip(r0 - WIN_ROWS // 2, 0, rows - NA_WIN))
        rs = [int(np.clip(r0 + i - WIN_ROWS // 2, 0, rows - WIN_ROWS)) for i in range(NA_G)]
        return r0 - ws, tuple(r - ws for r in rs)

    ng = rows // NA_G
    assert ng >= 3 and all(geom(g) == geom(1) for g in range(1, ng - 1))
    return geom(0), geom(1), geom(ng - 1)


def _na_kernel(q_ref, k_ref, v_ref, t_ref, qg_ref, kg_ref, og_ref, o_ref,
               qn_ref, kn_ref, bias_ref, *, rows, norm_rows):
    log2e = np.float32(np.log2(np.e))
    qg = qg_ref[...] * (np.float32(HEAD_DIM ** -0.5) * log2e)
    kg = kg_ref[...]
    og = og_ref[0]

    masked_tile = jnp.full((GRID_W, GRID_W), MASKED, F32)
    for cls, (roff, rsoff) in enumerate(_na_group_geometry(rows)):
        for ri in range(NA_G):
            for kp in range(NA_WIN // 2):
                pair = []
                for kr in (2 * kp, 2 * kp + 1):
                    in_window = 0 <= kr - rsoff[ri] < WIN_ROWS
                    dr = kr - roff - ri + (WIN_ROWS - 1)
                    pair.append(t_ref[0, dr] * log2e if in_window else masked_tile)
                bias_ref[cls, ri * GRID_W:(ri + 1) * GRID_W, kp * 128:(kp + 1) * 128] = (
                    jnp.concatenate(pair, axis=1))

    def norm_body(c, carry):
        sl = pl.ds(pl.multiple_of(c * norm_rows, norm_rows), norm_rows)
        qn_ref[sl, :] = _rms(q_ref[0, sl, :].astype(F32), qg).astype(BF16)
        kn_ref[sl, :] = _rms(k_ref[0, sl, :].astype(F32), kg).astype(BF16)
        return carry

    lax.fori_loop(0, rows * GRID_W // norm_rows, norm_body, 0)

    ng = rows // NA_G
    nq = NA_G * GRID_W
    nk = NA_WIN * GRID_W

    def group_body(g, carry):
        r0 = g * NA_G
        ws = jnp.clip(r0 - WIN_ROWS // 2, 0, rows - NA_WIN)
        cls = jnp.minimum(g, 1) + jnp.maximum(g - (ng - 2), 0)
        qsl = pl.ds(pl.multiple_of(r0 * GRID_W, nq), nq)
        ksl = pl.ds(pl.multiple_of(ws * GRID_W, nq), nk)
        s = lax.dot_general(qn_ref[qsl, :], kn_ref[ksl, :], (((1,), (1,)), ((), ())),
                            preferred_element_type=F32)
        s = s + bias_ref[cls]
        m = jnp.max(s, axis=-1, keepdims=True)
        e = jnp.exp2(s - m)
        l = jnp.sum(e, axis=-1, keepdims=True)
        o = jnp.dot(e.astype(BF16), v_ref[0, ksl, :], preferred_element_type=F32) / l
        o_ref[qsl, :] = _rms(o, og).astype(o_ref.dtype)
        return carry

    lax.fori_loop(0, ng, group_body, 0, unroll=2)


def _na(z, toeplitz, qg, kg, og, n_heads, batch, seq, head0):
    rows = seq // GRID_W
    assert rows % NA_G == 0 and (WIN_ROWS // 2) % NA_G == 0
    n_dr = 2 * WIN_ROWS - 1
    return pl.pallas_call(
        functools.partial(_na_kernel, rows=rows, norm_rows=512),
        grid=(batch, n_heads),
        in_specs=[pl.BlockSpec((1, seq, HEAD_DIM), lambda b, h: (head0 + h, b, 0)),
                  pl.BlockSpec((1, seq, HEAD_DIM), lambda b, h: (head0 + n_heads + h, b, 0)),
                  pl.BlockSpec((1, seq, HEAD_DIM), lambda b, h: (head0 + 2 * n_heads + h, b, 0)),
                  pl.BlockSpec((1, n_dr, GRID_W, GRID_W), lambda b, h: (h, 0, 0, 0)),
                  pl.BlockSpec((1, HEAD_DIM), lambda b, h: (0, 0)),
                  pl.BlockSpec((1, HEAD_DIM), lambda b, h: (0, 0)),
                  pl.BlockSpec((1, 1, HEAD_DIM), lambda b, h: (h, 0, 0))],
        out_specs=pl.BlockSpec((seq, HEAD_DIM), lambda b, h: (b, h)),
        out_shape=jax.ShapeDtypeStruct((batch * seq, n_heads * HEAD_DIM), BF16),
        scratch_shapes=[pltpu.VMEM((seq, HEAD_DIM), BF16), pltpu.VMEM((seq, HEAD_DIM), BF16),
                        pltpu.VMEM((3, NA_G * GRID_W, NA_WIN * GRID_W), F32)],
        compiler_params=_params(("parallel", "parallel"), 40),
        name="natten",
    )(z, z, z, toeplitz, qg.reshape(1, HEAD_DIM), kg.reshape(1, HEAD_DIM),
      og.reshape(n_heads, 1, HEAD_DIM))


def _outproj_kernel(a_ref, b_ref, wa_ref, wb_ref, x_ref, o_ref):
    acc = jnp.dot(a_ref[...], wa_ref[...], preferred_element_type=F32)
    acc = acc + jnp.dot(b_ref[...], wb_ref[...], preferred_element_type=F32)
    o_ref[...] = x_ref[...] + acc


def _outproj(mix_a, mix_b, w, x, bm=1024, bn=512):
    t, ka = mix_a.shape
    kb = mix_b.shape[1]
    n = w.shape[1]
    assert ka == kb
    return pl.pallas_call(
        _outproj_kernel,
        grid=(t // bm, n // bn),
        in_specs=[pl.BlockSpec((bm, ka), lambda i, j: (i, 0)),
                  pl.BlockSpec((bm, kb), lambda i, j: (i, 0)),
                  pl.BlockSpec((ka, bn), lambda i, j: (0, j)),
                  pl.BlockSpec((kb, bn), lambda i, j: (1, j)),
                  pl.BlockSpec((bm, bn), lambda i, j: (i, j))],
        out_specs=pl.BlockSpec((bm, bn), lambda i, j: (i, j)),
        out_shape=jax.ShapeDtypeStruct((t, n), F32),
        compiler_params=_params(("parallel", "arbitrary"), 48),
        name="outproj",
    )(mix_a, mix_b, w, w, x)


def _dependent_zero(x):
    bits = pltpu.bitcast(x, jnp.uint32)
    return pltpu.bitcast((bits >> 16) >> 16, F32)


def _ffn_up_kernel(xm_ref, xp_ref, xn_ref, wg_ref, wv_ref, cwg_ref, cwv_ref, cbg_ref, cbv_ref,
                   o_ref, xe_ref, up_a_ref, up_b_ref, *, bm, bn, nj, n_blocks, blocks_per_seq):
    s = pl.program_id(0)
    i = jnp.minimum(s // nj, n_blocks // nj - 1)
    pos = i % blocks_per_seq

    @pl.when((s % nj == 0) & (s < n_blocks))
    def _():
        xe_ref[HALO:HALO + bm, :] = xm_ref[...]

        @pl.when(pos == 0)
        def _():
            xe_ref[0:HALO, :] = jnp.zeros((HALO, xe_ref.shape[1]), xe_ref.dtype)

        @pl.when(pos != 0)
        def _():
            xe_ref[0:HALO, :] = xp_ref[...]

        @pl.when(pos == blocks_per_seq - 1)
        def _():
            xe_ref[HALO + bm:, :] = jnp.zeros((HALO, xe_ref.shape[1]), xe_ref.dtype)

        @pl.when(pos != blocks_per_seq - 1)
        def _():
            xe_ref[HALO + bm:, :] = xn_ref[...]

    @pl.when(s == 0)
    def _():
        up_b_ref[...] = jnp.zeros(up_b_ref.shape, up_b_ref.dtype)

    def step(new_ref, old_ref):
        cw = jnp.concatenate([cwg_ref[...], cwv_ref[...]], axis=1)
        cb = jnp.concatenate([cbg_ref[...], cbv_ref[...]], axis=1)
        w_cur = cw[1:2, :]
        for r0 in range(0, bm, EPI_ROWS):
            slab = old_ref[r0 + HALO - 8:r0 + HALO + EPI_ROWS + 8, :]
            prev = pltpu.roll(slab, 1, 0)[8:8 + EPI_ROWS]
            cur = slab[8:8 + EPI_ROWS]
            nxt = pltpu.roll(slab, EPI_ROWS + 15, 0)[8:8 + EPI_ROWS]
            c = cb + (prev * cw[0:1, :] + cur * w_cur + nxt * cw[2:3, :])
            act = _gelu(c[:, :bn]) * c[:, bn:]
            o_ref[r0:r0 + EPI_ROWS, :] = act.astype(o_ref.dtype)
            zero = _dependent_zero(act[0:1, 0:128])
            w_cur = cw[1:2, :] + jnp.concatenate([zero] * (2 * bn // 128), axis=1)

        xe = xe_ref[...]
        new_ref[:, :bn] = jnp.dot(xe, wg_ref[...], preferred_element_type=F32)
        new_ref[:, bn:] = jnp.dot(xe, wv_ref[...], preferred_element_type=F32)

    @pl.when(s % 2 == 0)
    def _():
        step(up_a_ref, up_b_ref)

    @pl.when(s % 2 == 1)
    def _():
        step(up_b_ref, up_a_ref)


def _ffn_up(xn, w, cw, cb, seq, bm=1024, bn=256):
    t, k = xn.shape
    n = w.shape[1] // 2
    ni, nj = t // bm, n // bn
    n_blocks = ni * nj
    hb = bm // HALO
    n_halo_blocks = t // HALO

    def cur_i(s):
        return jnp.minimum(s // nj, ni - 1)

    def prev_ij(s):
        sp = jnp.maximum(s - 1, 0)
        return sp // nj, sp % nj

    return pl.pallas_call(
        functools.partial(_ffn_up_kernel, bm=bm, bn=bn, nj=nj, n_blocks=n_blocks,
                          blocks_per_seq=seq // bm),
        grid=(n_blocks + 1,),
        in_specs=[pl.BlockSpec((bm, k), lambda s: (cur_i(s), 0)),
                  pl.BlockSpec((HALO, k), lambda s: (jnp.maximum(cur_i(s) * hb - 1, 0), 0)),
                  pl.BlockSpec((HALO, k),
                               lambda s: (jnp.minimum((cur_i(s) + 1) * hb, n_halo_blocks - 1), 0)),
                  pl.BlockSpec((k, bn), lambda s: (0, s % nj)),
                  pl.BlockSpec((k, bn), lambda s: (0, nj + s % nj)),
                  pl.BlockSpec((3, bn), lambda s: (0, prev_ij(s)[1])),
                  pl.BlockSpec((3, bn), lambda s: (0, nj + prev_ij(s)[1])),
                  pl.BlockSpec((1, bn), lambda s: (0, prev_ij(s)[1])),
                  pl.BlockSpec((1, bn), lambda s: (0, nj + prev_ij(s)[1]))],
        out_specs=pl.BlockSpec((bm, bn), lambda s: prev_ij(s)),
        out_shape=jax.ShapeDtypeStruct((t, n), BF16),
        scratch_shapes=[pltpu.VMEM((bm + 2 * HALO, k), BF16),
                        pltpu.VMEM((bm + 2 * HALO, 2 * bn), F32),
                        pltpu.VMEM((bm + 2 * HALO, 2 * bn), F32)],
        compiler_params=_params(("arbitrary",), 48),
        name="ffn_up",
    )(xn, xn, xn, w, w, cw, cw, cb, cb)


def _ffn_down_kernel(a_ref, w_ref, h_ref, o_ref):
    o_ref[...] = h_ref[...] + jnp.dot(a_ref[...], w_ref[...], preferred_element_type=F32)


def _ffn_down(act, w, h, bm=512, bn=512):
    t, k = act.shape
    n = w.shape[1]
    return pl.pallas_call(
        _ffn_down_kernel,
        grid=(t // bm, n // bn),
        in_specs=[pl.BlockSpec((bm, k), lambda i, j: (i, 0)),
                  pl.BlockSpec((k, bn), lambda i, j: (0, j)),
                  pl.BlockSpec((bm, bn), lambda i, j: (i, j))],
        out_specs=pl.BlockSpec((bm, bn), lambda i, j: (i, j)),
        out_shape=jax.ShapeDtypeStruct((t, n), F32),
        compiler_params=_params(("parallel", "arbitrary"), 56),
        name="ffn_down",
    )(act, w, h)


def _ple_embed_kernel(p_ref, w_ref, g_ref, o_ref):
    e = jnp.dot(p_ref[...].astype(BF16), w_ref[...], preferred_element_type=F32)
    o_ref[...] = _rms(e, g_ref[...]).astype(o_ref.dtype)


def _ple_embed(p, w, g, bm=512):
    t, k = p.shape
    n = w.shape[1]
    return pl.pallas_call(
        _ple_embed_kernel,
        grid=(t // bm,),
        in_specs=[pl.BlockSpec((bm, k), lambda i: (i, 0)),
                  pl.BlockSpec((k, n), lambda i: (0, 0)),
                  pl.BlockSpec((1, n), lambda i: (0, 0))],
        out_specs=pl.BlockSpec((bm, n), lambda i: (i, 0)),
        out_shape=jax.ShapeDtypeStruct((t, n), BF16),
        compiler_params=_params(("parallel",), 40),
        name="ple_embed",
    )(p, w, g.reshape(1, n))


def _ple_gate_kernel(a_ref, w_ref, e_ref, h_ref, o_ref):
    gate = jax.nn.sigmoid(jnp.dot(a_ref[...], w_ref[...], preferred_element_type=F32))
    o_ref[...] = h_ref[...] + gate * e_ref[...].astype(F32)


def _ple_gate(hn, w, e, h, bm=1024, bn=512):
    t, k = hn.shape
    n = w.shape[1]
    return pl.pallas_call(
        _ple_gate_kernel,
        grid=(t // bm, n // bn),
        in_specs=[pl.BlockSpec((bm, k), lambda i, j: (i, 0)),
                  pl.BlockSpec((k, bn), lambda i, j: (0, j)),
                  pl.BlockSpec((bm, bn), lambda i, j: (i, j)),
                  pl.BlockSpec((bm, bn), lambda i, j: (i, j))],
        out_specs=pl.BlockSpec((bm, bn), lambda i, j: (i, j)),
        out_shape=jax.ShapeDtypeStruct((t, n), F32),
        compiler_params=_params(("parallel", "arbitrary"), 48),
        name="ple_gate",
    )(hn, w, e, h)


def kernel(x, p, norm_mix_g, w_in, gmlp_v_g, gmlp_ws, gmlp_bs, q_norm_g, k_norm_g, na_rpb,
           out_norm_a_g, out_norm_b_g, w_out, norm_ffn_g, w_up, conv_w, conv_b, w_down,
           norm_ple_g, w_ple_gate, w_ple_proj, ple_post_g):
    batch, seq, d_model = x.shape
    t = batch * seq
    depth = w_in.shape[0]
    n_a = gmlp_ws.shape[1]
    n_b = na_rpb.shape[1]
    h = x.reshape(t, d_model)
    for i in range(depth):
        hn = _rmsnorm(h, norm_mix_g[i])
        z = _inproj(hn, w_in[i].astype(BF16), 2 * n_a * HEAD_DIM)
        mix_a = _gmlp(z, gmlp_ws[i], gmlp_bs[i], gmlp_v_g[i], out_norm_a_g[i], n_a)
        mix_b = _na(z, _na_toeplitz(na_rpb[i]), q_norm_g[i], k_norm_g[i], out_norm_b_g[i],
                    n_b, batch, seq, 2 * n_a)
        h = _outproj(mix_a, mix_b, w_out[i].astype(BF16), h)
        xn = _rmsnorm(h, norm_ffn_g[i])
        act = _ffn_up(xn, w_up[i].astype(BF16), conv_w[i], conv_b[i].reshape(1, -1), seq)
        h = _ffn_down(act, w_down[i].astype(BF16), h)
        hn = _rmsnorm(h, norm_ple_g[i])
        e = _ple_embed(p[i].reshape(t, -1), w_ple_proj[i].astype(BF16), ple_post_g[i])
        h = _ple_gate(hn, w_ple_gate[i].astype(BF16), e, h)
    return h.reshape(batch, seq, d_model)
```

```python
import functools

import numpy as np
import jax
import jax.numpy as jnp
from jax import lax
from jax.experimental import pallas as pl
from jax.experimental.pallas import tpu as pltpu

F32 = jnp.float32
BF16 = jnp.bfloat16

EPS = 1e-6
HEAD_DIM = 128
CHUNK = 128
GMLP_GROUP = 8
GRID_W = 64
WIN_ROWS = 8
WIN_COLS = 16
NA_G = 4
NA_WIN = NA_G + WIN_ROWS
MASKED = -1e30
HALO = 16
EPI_ROWS = 8
SSQ_BN = 512
MIB = 1 << 20


def _params(semantics, vmem_mib, flags=None):
    return pltpu.CompilerParams(dimension_semantics=semantics,
                                vmem_limit_bytes=vmem_mib * MIB, flags=flags)


def _rms(x, g):
    ms = jnp.mean(x * x, axis=-1, keepdims=True)
    return x * lax.rsqrt(ms + EPS) * g


def _gelu(x):
    return 0.5 * x * (1.0 + lax.erf(x * np.float32(np.sqrt(0.5))))


def _rmsnorm_kernel(x_ref, g_ref, o_ref):
    o_ref[...] = _rms(x_ref[...], g_ref[...]).astype(o_ref.dtype)


def _rmsnorm(x, g, bm=512):
    t, d = x.shape
    return pl.pallas_call(
        _rmsnorm_kernel,
        grid=(t // bm,),
        in_specs=[pl.BlockSpec((bm, d), lambda i: (i, 0)),
                  pl.BlockSpec((1, d), lambda i: (0, 0))],
        out_specs=pl.BlockSpec((bm, d), lambda i: (i, 0)),
        out_shape=jax.ShapeDtypeStruct((t, d), BF16),
        compiler_params=_params(("parallel",), 40),
        name="rmsnorm",
    )(x, g.reshape(1, d))


def _inproj_kernel(a_ref, w_ref, o_ref, *, n_gelu_blocks, heads_per_block):
    acc = jnp.dot(a_ref[...], w_ref[...], preferred_element_type=F32)
    j = pl.program_id(1)

    def store(val):
        for hh in range(heads_per_block):
            o_ref[hh] = val[:, hh * HEAD_DIM:(hh + 1) * HEAD_DIM].astype(o_ref.dtype)

    @pl.when(j < n_gelu_blocks)
    def _():
        store(_gelu(acc))

    @pl.when(j >= n_gelu_blocks)
    def _():
        store(acc)


def _inproj(hn, w, n_gelu_cols, bm=1024, bn=1024):
    t, k = hn.shape
    n = w.shape[1]
    hpb = bn // HEAD_DIM
    return pl.pallas_call(
        functools.partial(_inproj_kernel, n_gelu_blocks=n_gelu_cols // bn, heads_per_block=hpb),
        grid=(t // bm, n // bn),
        in_specs=[pl.BlockSpec((bm, k), lambda i, j: (i, 0)),
                  pl.BlockSpec((k, bn), lambda i, j: (0, j))],
        out_specs=pl.BlockSpec((hpb, bm, HEAD_DIM), lambda i, j: (j, i, 0)),
        out_shape=jax.ShapeDtypeStruct((n // HEAD_DIM, t, HEAD_DIM), BF16),
        compiler_params=_params(("parallel", "arbitrary"), 56),
        name="inproj",
    )(hn, w)


def _gmlp_kernel(u_ref, v_ref, ws_ref, bs_ref, gv_ref, og_ref, o_ref, *, n_groups):
    ws = ws_ref[0]
    bs = bs_ref[0]
    gv = gv_ref[0]
    og = og_ref[0]

    def group_body(gi, carry):
        base = pl.multiple_of(gi * (GMLP_GROUP * CHUNK), GMLP_GROUP * CHUNK)
        for c in range(GMLP_GROUP):
            sl = pl.ds(base + c * CHUNK, CHUNK)
            vn = _rms(v_ref[0, sl, :].astype(F32), gv).astype(BF16)
            mixed = jnp.dot(ws, vn, preferred_element_type=F32) + bs
            a = u_ref[0, sl, :].astype(F32) * mixed
            o_ref[sl, :] = _rms(a, og).astype(o_ref.dtype)
        return carry

    lax.fori_loop(0, n_groups, group_body, 0)


def _gmlp(z, ws, bs, gv, og, n_heads, tb=8192):
    t = z.shape[1]
    return pl.pallas_call(
        functools.partial(_gmlp_kernel, n_groups=tb // (GMLP_GROUP * CHUNK)),
        grid=(n_heads, t // tb),
        in_specs=[pl.BlockSpec((1, tb, HEAD_DIM), lambda h, i: (h, i, 0)),
                  pl.BlockSpec((1, tb, HEAD_DIM), lambda h, i: (n_heads + h, i, 0)),
                  pl.BlockSpec((1, CHUNK, CHUNK), lambda h, i: (h, 0, 0)),
                  pl.BlockSpec((1, CHUNK, 1), lambda h, i: (h, 0, 0)),
                  pl.BlockSpec((1, 1, HEAD_DIM), lambda h, i: (h, 0, 0)),
                  pl.BlockSpec((1, 1, HEAD_DIM), lambda h, i: (h, 0, 0))],
        out_specs=pl.BlockSpec((tb, HEAD_DIM), lambda h, i: (i, h)),
        out_shape=jax.ShapeDtypeStruct((t, n_heads * HEAD_DIM), BF16),
        compiler_params=_params(("parallel", "parallel"), 24),
        name="gmlp",
    )(z, z, ws.astype(BF16), bs.reshape(n_heads, CHUNK, 1),
      gv.reshape(n_heads, 1, HEAD_DIM), og.reshape(n_heads, 1, HEAD_DIM))


def _na_toeplitz(rpb):
    col = np.arange(GRID_W)
    cs = np.clip(col - WIN_COLS // 2, 0, GRID_W - WIN_COLS)
    kc = np.arange(GRID_W)
    valid = (kc[None, :] >= cs[:, None]) & (kc[None, :] < cs[:, None] + WIN_COLS)
    dc = kc[None, :] - col[:, None] + (WIN_COLS - 1)
    onehot = (dc[:, :, None] == np.arange(2 * WIN_COLS - 1)).astype(np.float32)
    t = jnp.einsum('hrd,ckd->hrck', rpb, onehot, precision=lax.Precision.HIGHEST)
    return jnp.where(valid[None, None], t, MASKED)


def _na_group_geometry(rows):
    def geom(g):
        r0 = g * NA_G
        ws = int(np.clip(r0 - WIN_ROWS // 2, 0, rows - NA_WIN))
        rs = [int(np.clip(r0 + i - WIN_ROWS // 2, 0, rows - WIN_ROWS)) for i in range(NA_G)]
        return r0 - ws, tuple(r - ws for r in rs)

    ng = rows // NA_G
    assert ng >= 3 and all(geom(g) == geom(1) for g in range(1, ng - 1))
    return geom(0), geom(1), geom(ng - 1)


def _na_kernel(q_ref, k_ref, v_ref, t_ref, qg_ref, kg_ref, og_ref, o_ref,
               qn_ref, kn_ref, bias_ref, *, rows, norm_rows):
    log2e = np.float32(np.log2(np.e))
    qg = qg_ref[...] * (np.float32(HEAD_DIM ** -0.5) * log2e)
    kg = kg_ref[...]
    og = og_ref[0]

    masked_tile = jnp.full((GRID_W, GRID_W), MASKED, F32)
    for cls, (roff, rsoff) in enumerate(_na_group_geometry(rows)):
        for ri in range(NA_G):
            for kp in range(NA_WIN // 2):
                pair = []
                for kr in (2 * kp, 2 * kp + 1):
                    in_window = 0 <= kr - rsoff[ri] < WIN_ROWS
                    dr = kr - roff - ri + (WIN_ROWS - 1)
                    pair.append(t_ref[0, dr] * log2e if in_window else masked_tile)
                bias_ref[cls, ri * GRID_W:(ri + 1) * GRID_W, kp * 128:(kp + 1) * 128] = (
                    jnp.concatenate(pair, axis=1))

    def norm_body(c, carry):
        sl = pl.ds(pl.multiple_of(c * norm_rows, norm_rows), norm_rows)
        qn_ref[sl, :] = _rms(q_ref[0, sl, :].astype(F32), qg).astype(BF16)
        kn_ref[sl, :] = _rms(k_ref[0, sl, :].astype(F32), kg).astype(BF16)
        return carry

    lax.fori_loop(0, rows * GRID_W // norm_rows, norm_body, 0)

    ng = rows // NA_G
    nq = NA_G * GRID_W
    nk = NA_WIN * GRID_W

    def group_body(g, carry):
        r0 = g * NA_G
        ws = jnp.clip(r0 - WIN_ROWS // 2, 0, rows - NA_WIN)
        cls = jnp.minimum(g, 1) + jnp.maximum(g - (ng - 2), 0)
        qsl = pl.ds(pl.multiple_of(r0 * GRID_W, nq), nq)
        ksl = pl.ds(pl.multiple_of(ws * GRID_W, nq), nk)
        s = lax.dot_general(qn_ref[qsl, :], kn_ref[ksl, :], (((1,), (1,)), ((), ())),
                            preferred_element_type=F32)
        s = s + bias_ref[cls]
        m = jnp.max(s, axis=-1, keepdims=True)
        e = jnp.exp2(s - m)
        l = jnp.sum(e, axis=-1, keepdims=True)
        o = jnp.dot(e.astype(BF16), v_ref[0, ksl, :], preferred_element_type=F32) / l
        o_ref[qsl, :] = _rms(o, og).astype(o_ref.dtype)
        return carry

    lax.fori_loop(0, ng, group_body, 0, unroll=2)


def _na(z, toeplitz, qg, kg, og, n_heads, batch, seq, head0):
    rows = seq // GRID_W
    assert rows % NA_G == 0 and (WIN_ROWS // 2) % NA_G == 0
    n_dr = 2 * WIN_ROWS - 1
    return pl.pallas_call(
        functools.partial(_na_kernel, rows=rows, norm_rows=512),
        grid=(batch, n_heads),
        in_specs=[pl.BlockSpec((1, seq, HEAD_DIM), lambda b, h: (head0 + h, b, 0)),
                  pl.BlockSpec((1, seq, HEAD_DIM), lambda b, h: (head0 + n_heads + h, b, 0)),
                  pl.BlockSpec((1, seq, HEAD_DIM), lambda b, h: (head0 + 2 * n_heads + h, b, 0)),
                  pl.BlockSpec((1, n_dr, GRID_W, GRID_W), lambda b, h: (h, 0, 0, 0)),
                  pl.BlockSpec((1, HEAD_DIM), lambda b, h: (0, 0)),
                  pl.BlockSpec((1, HEAD_DIM), lambda b, h: (0, 0)),
                  pl.BlockSpec((1, 1, HEAD_DIM), lambda b, h: (h, 0, 0))],
        out_specs=pl.BlockSpec((seq, HEAD_DIM), lambda b, h: (b, h)),
        out_shape=jax.ShapeDtypeStruct((batch * seq, n_heads * HEAD_DIM), BF16),
        scratch_shapes=[pltpu.VMEM((seq, HEAD_DIM), BF16), pltpu.VMEM((seq, HEAD_DIM), BF16),
                        pltpu.VMEM((3, NA_G * GRID_W, NA_WIN * GRID_W), F32)],
        compiler_params=_params(("parallel", "parallel"), 40),
        name="natten",
    )(z, z, z, toeplitz, qg.reshape(1, HEAD_DIM), kg.reshape(1, HEAD_DIM),
      og.reshape(n_heads, 1, HEAD_DIM))


def _store_residual(h, o_ref, ob_ref, ssq_ref):
    o_ref[...] = h
    ob_ref[...] = h.astype(ob_ref.dtype)
    ssq_ref[...] = jnp.broadcast_to(jnp.sum(h * h, axis=-1, keepdims=True), ssq_ref.shape)


def _residual_out(t, n, bm, bn):
    assert bn == SSQ_BN
    specs = [pl.BlockSpec((bm, bn), lambda i, j: (i, j)),
             pl.BlockSpec((bm, bn), lambda i, j: (i, j)),
             pl.BlockSpec((bm, 128), lambda i, j: (i, j))]
    shapes = [jax.ShapeDtypeStruct((t, n), F32), jax.ShapeDtypeStruct((t, n), BF16),
              jax.ShapeDtypeStruct((t, n // bn * 128), F32)]
    return specs, shapes


def _rstd_from_ssq(ssq, d):
    tot = ssq[:, 0:128]
    for c in range(1, ssq.shape[1] // 128):
        tot = tot + ssq[:, c * 128:(c + 1) * 128]
    return lax.rsqrt(tot * np.float32(1.0 / d) + EPS)


def _outproj_kernel(a_ref, b_ref, wa_ref, wb_ref, x_ref, o_ref, ob_ref, ssq_ref):
    acc = jnp.dot(a_ref[...], wa_ref[...], preferred_element_type=F32)
    acc = acc + jnp.dot(b_ref[...], wb_ref[...], preferred_element_type=F32)
    _store_residual(x_ref[...] + acc, o_ref, ob_ref, ssq_ref)


def _outproj(mix_a, mix_b, w, x, bm=1024, bn=SSQ_BN):
    t, ka = mix_a.shape
    kb = mix_b.shape[1]
    n = w.shape[1]
    assert ka == kb
    out_specs, out_shapes = _residual_out(t, n, bm, bn)
    return pl.pallas_call(
        _outproj_kernel,
        grid=(t // bm, n // bn),
        in_specs=[pl.BlockSpec((bm, ka), lambda i, j: (i, 0)),
                  pl.BlockSpec((bm, kb), lambda i, j: (i, 0)),
                  pl.BlockSpec((ka, bn), lambda i, j: (0, j)),
                  pl.BlockSpec((kb, bn), lambda i, j: (1, j)),
                  pl.BlockSpec((bm, bn), lambda i, j: (i, j))],
        out_specs=out_specs,
        out_shape=out_shapes,
        compiler_params=_params(("parallel", "arbitrary"), 48),
        name="outproj",
    )(mix_a, mix_b, w, w, x)


def _dependent_zero(x):
    bits = pltpu.bitcast(x, jnp.uint32)
    return pltpu.bitcast((bits >> 16) >> 16, F32)


def _ffn_up_kernel(xm_ref, xp_ref, xn_ref, sm_ref, sp_ref, sn_ref, wg_ref, wv_ref,
                   cwg_ref, cwv_ref, cbg_ref, cbv_ref,
                   o_ref, xe_ref, rstd_ref, up_a_ref, up_b_ref,
                   *, bm, bn, nj, n_blocks, blocks_per_seq):
    s = pl.program_id(0)
    i = jnp.minimum(s // nj, n_blocks // nj - 1)
    pos = i % blocks_per_seq
    d = xe_ref.shape[1]

    @pl.when((s % nj == 0) & (s < n_blocks))
    def _():
        xe_ref[HALO:HALO + bm, :] = xm_ref[...]
        rstd_ref[HALO:HALO + bm, :] = _rstd_from_ssq(sm_ref[...], d)
        rstd_ref[0:HALO, :] = _rstd_from_ssq(sp_ref[...], d)
        rstd_ref[HALO + bm:, :] = _rstd_from_ssq(sn_ref[...], d)

        @pl.when(pos == 0)
        def _():
            xe_ref[0:HALO, :] = jnp.zeros((HALO, d), xe_ref.dtype)

        @pl.when(pos != 0)
        def _():
            xe_ref[0:HALO, :] = xp_ref[...]

        @pl.when(pos == blocks_per_seq - 1)
        def _():
            xe_ref[HALO + bm:, :] = jnp.zeros((HALO, d), xe_ref.dtype)

        @pl.when(pos != blocks_per_seq - 1)
        def _():
            xe_ref[HALO + bm:, :] = xn_ref[...]

    @pl.when(s == 0)
    def _():
        up_b_ref[...] = jnp.zeros(up_b_ref.shape, up_b_ref.dtype)

    def step(new_ref, old_ref):
        cw = jnp.concatenate([cwg_ref[...], cwv_ref[...]], axis=1)
        cb = jnp.concatenate([cbg_ref[...], cbv_ref[...]], axis=1)
        w_cur = cw[1:2, :]
        for r0 in range(0, bm, EPI_ROWS):
            slab = old_ref[r0 + HALO - 8:r0 + HALO + EPI_ROWS + 8, :]
            prev = pltpu.roll(slab, 1, 0)[8:8 + EPI_ROWS]
            cur = slab[8:8 + EPI_ROWS]
            nxt = pltpu.roll(slab, EPI_ROWS + 15, 0)[8:8 + EPI_ROWS]
            c = cb + (prev * cw[0:1, :] + cur * w_cur + nxt * cw[2:3, :])
            act = _gelu(c[:, :bn]) * c[:, bn:]
            o_ref[r0:r0 + EPI_ROWS, :] = act.astype(o_ref.dtype)
            zero = _dependent_zero(act[0:1, 0:128])
            w_cur = cw[1:2, :] + jnp.concatenate([zero] * (2 * bn // 128), axis=1)

        xe = xe_ref[...]
        rstd = jnp.concatenate([rstd_ref[...]] * (bn // 128), axis=1)
        new_ref[:, :bn] = jnp.dot(xe, wg_ref[...], preferred_element_type=F32) * rstd
        new_ref[:, bn:] = jnp.dot(xe, wv_ref[...], preferred_element_type=F32) * rstd

    @pl.when(s % 2 == 0)
    def _():
        step(up_a_ref, up_b_ref)

    @pl.when(s % 2 == 1)
    def _():
        step(up_b_ref, up_a_ref)


def _ffn_up(xn, ssq, w, cw, cb, seq, bm=1024, bn=256):
    t, k = xn.shape
    sw = ssq.shape[1]
    n = w.shape[1] // 2
    ni, nj = t // bm, n // bn
    n_blocks = ni * nj
    hb = bm // HALO
    n_halo_blocks = t // HALO

    def cur_i(s):
        return jnp.minimum(s // nj, ni - 1)

    def halo_before(s):
        return jnp.maximum(cur_i(s) * hb - 1, 0)

    def halo_after(s):
        return jnp.minimum((cur_i(s) + 1) * hb, n_halo_blocks - 1)

    def prev_ij(s):
        sp = jnp.maximum(s - 1, 0)
        return sp // nj, sp % nj

    return pl.pallas_call(
        functools.partial(_ffn_up_kernel, bm=bm, bn=bn, nj=nj, n_blocks=n_blocks,
                          blocks_per_seq=seq // bm),
        grid=(n_blocks + 1,),
        in_specs=[pl.BlockSpec((bm, k), lambda s: (cur_i(s), 0)),
                  pl.BlockSpec((HALO, k), lambda s: (halo_before(s), 0)),
                  pl.BlockSpec((HALO, k), lambda s: (halo_after(s), 0)),
                  pl.BlockSpec((bm, sw), lambda s: (cur_i(s), 0)),
                  pl.BlockSpec((HALO, sw), lambda s: (halo_before(s), 0)),
                  pl.BlockSpec((HALO, sw), lambda s: (halo_after(s), 0)),
                  pl.BlockSpec((k, bn), lambda s: (0, s % nj)),
                  pl.BlockSpec((k, bn), lambda s: (0, nj + s % nj)),
                  pl.BlockSpec((3, bn), lambda s: (0, prev_ij(s)[1])),
                  pl.BlockSpec((3, bn), lambda s: (0, nj + prev_ij(s)[1])),
                  pl.BlockSpec((1, bn), lambda s: (0, prev_ij(s)[1])),
                  pl.BlockSpec((1, bn), lambda s: (0, nj + prev_ij(s)[1]))],
        out_specs=pl.BlockSpec((bm, bn), lambda s: prev_ij(s)),
        out_shape=jax.ShapeDtypeStruct((t, n), BF16),
        scratch_shapes=[pltpu.VMEM((bm + 2 * HALO, k), BF16),
                        pltpu.VMEM((bm + 2 * HALO, 128), F32),
                        pltpu.VMEM((bm + 2 * HALO, 2 * bn), F32),
                        pltpu.VMEM((bm + 2 * HALO, 2 * bn), F32)],
        compiler_params=_params(("arbitrary",), 56),
        name="ffn_up",
    )(xn, xn, xn, ssq, ssq, ssq, w, w, cw, cw, cb, cb)


def _ffn_down_kernel(a_ref, w_ref, h_ref, o_ref, ob_ref, ssq_ref):
    acc = jnp.dot(a_ref[...], w_ref[...], preferred_element_type=F32)
    _store_residual(h_ref[...] + acc, o_ref, ob_ref, ssq_ref)


def _ffn_down(act, w, h, bm=512, bn=SSQ_BN):
    t, k = act.shape
    n = w.shape[1]
    out_specs, out_shapes = _residual_out(t, n, bm, bn)
    return pl.pallas_call(
        _ffn_down_kernel,
        grid=(t // bm, n // bn),
        in_specs=[pl.BlockSpec((bm, k), lambda i, j: (i, 0)),
                  pl.BlockSpec((k, bn), lambda i, j: (0, j)),
                  pl.BlockSpec((bm, bn), lambda i, j: (i, j))],
        out_specs=out_specs,
        out_shape=out_shapes,
        compiler_params=_params(("parallel", "arbitrary"), 56),
        name="ffn_down",
    )(act, w, h)


def _ple_embed_kernel(p_ref, w_ref, g_ref, o_ref):
    e = jnp.dot(p_ref[...].astype(BF16), w_ref[...], preferred_element_type=F32)
    o_ref[...] = _rms(e, g_ref[...]).astype(o_ref.dtype)


def _ple_embed(p, w, g, bm=512):
    t, k = p.shape
    n = w.shape[1]
    return pl.pallas_call(
        _ple_embed_kernel,
        grid=(t // bm,),
        in_specs=[pl.BlockSpec((bm, k), lambda i: (i, 0)),
                  pl.BlockSpec((k, n), lambda i: (0, 0)),
                  pl.BlockSpec((1, n), lambda i: (0, 0))],
        out_specs=pl.BlockSpec((bm, n), lambda i: (i, 0)),
        out_shape=jax.ShapeDtypeStruct((t, n), BF16),
        compiler_params=_params(("parallel",), 40),
        name="ple_embed",
    )(p, w, g.reshape(1, n))


def _ple_gate_kernel(a_ref, ssq_ref, w_ref, e_ref, h_ref, o_ref, rstd_ref):
    bn = o_ref.shape[1]

    @pl.when(pl.program_id(1) == 0)
    def _():
        rstd_ref[...] = _rstd_from_ssq(ssq_ref[...], a_ref.shape[1])

    rstd = jnp.concatenate([rstd_ref[...]] * (bn // 128), axis=1)
    gate = jax.nn.sigmoid(jnp.dot(a_ref[...], w_ref[...], preferred_element_type=F32) * rstd)
    o_ref[...] = h_ref[...] + gate * e_ref[...].astype(F32)


def _ple_gate(hb, ssq, w, e, h, bm=1024, bn=512):
    t, k = hb.shape
    n = w.shape[1]
    return pl.pallas_call(
        _ple_gate_kernel,
        grid=(t // bm, n // bn),
        in_specs=[pl.BlockSpec((bm, k), lambda i, j: (i, 0)),
                  pl.BlockSpec((bm, ssq.shape[1]), lambda i, j: (i, 0)),
                  pl.BlockSpec((k, bn), lambda i, j: (0, j)),
                  pl.BlockSpec((bm, bn), lambda i, j: (i, j)),
                  pl.BlockSpec((bm, bn), lambda i, j: (i, j))],
        out_specs=pl.BlockSpec((bm, bn), lambda i, j: (i, j)),
        out_shape=jax.ShapeDtypeStruct((t, n), F32),
        scratch_shapes=[pltpu.VMEM((bm, 128), F32)],
        compiler_params=_params(("parallel", "arbitrary"), 48),
        name="ple_gate",
    )(hb, ssq, w, e, h)


def kernel(x, p, norm_mix_g, w_in, gmlp_v_g, gmlp_ws, gmlp_bs, q_norm_g, k_norm_g, na_rpb,
           out_norm_a_g, out_norm_b_g, w_out, norm_ffn_g, w_up, conv_w, conv_b, w_down,
           norm_ple_g, w_ple_gate, w_ple_proj, ple_post_g):
    batch, seq, d_model = x.shape
    t = batch * seq
    depth = w_in.shape[0]
    n_a = gmlp_ws.shape[1]
    n_b = na_rpb.shape[1]
    h = x.reshape(t, d_model)
    for i in range(depth):
        hn = _rmsnorm(h, norm_mix_g[i])
        z = _inproj(hn, w_in[i].astype(BF16), 2 * n_a * HEAD_DIM)
        mix_a = _gmlp(z, gmlp_ws[i], gmlp_bs[i], gmlp_v_g[i], out_norm_a_g[i], n_a)
        mix_b = _na(z, _na_toeplitz(na_rpb[i]), q_norm_g[i], k_norm_g[i], out_norm_b_g[i],
                    n_b, batch, seq, 2 * n_a)
        h, hb, ssq = _outproj(mix_a, mix_b, w_out[i].astype(BF16), h)
        act = _ffn_up(hb, ssq, (norm_ffn_g[i][:, None] * w_up[i]).astype(BF16),
                      conv_w[i], conv_b[i].reshape(1, -1), seq)
        h, hb, ssq = _ffn_down(act, w_down[i].astype(BF16), h)
        e = _ple_embed(p[i].reshape(t, -1), w_ple_proj[i].astype(BF16), ple_post_g[i])
        h = _ple_gate(hb, ssq, (norm_ple_g[i][:, None] * w_ple_gate[i]).astype(BF16), e, h)
    return h.reshape(batch, seq, d_model)
```

```python
import functools

import numpy as np
import jax
import jax.numpy as jnp
from jax import lax
from jax.experimental import pallas as pl
from jax.experimental.pallas import tpu as pltpu

F32 = jnp.float32
BF16 = jnp.bfloat16

EPS = 1e-6
HEAD_DIM = 128
CHUNK = 128
GMLP_GROUP = 8
GRID_W = 64
WIN_ROWS = 8
WIN_COLS = 16
NA_G = 4
NA_WIN = NA_G + WIN_ROWS
MASKED = -1e30
HALO = 16
EPI_ROWS = 8
SSQ_BN = 512
MIB = 1 << 20


def _params(semantics, vmem_mib, flags=None):
    return pltpu.CompilerParams(dimension_semantics=semantics,
                                vmem_limit_bytes=vmem_mib * MIB, flags=flags)


def _rms(x, g):
    ms = jnp.mean(x * x, axis=-1, keepdims=True)
    return x * lax.rsqrt(ms + EPS) * g


def _gelu(x):
    return 0.5 * x * (1.0 + lax.erf(x * np.float32(np.sqrt(0.5))))


def _rmsnorm_kernel(x_ref, g_ref, o_ref):
    o_ref[...] = _rms(x_ref[...], g_ref[...]).astype(o_ref.dtype)


def _rmsnorm(x, g, bm=512):
    t, d = x.shape
    return pl.pallas_call(
        _rmsnorm_kernel,
        grid=(t // bm,),
        in_specs=[pl.BlockSpec((bm, d), lambda i: (i, 0)),
                  pl.BlockSpec((1, d), lambda i: (0, 0))],
        out_specs=pl.BlockSpec((bm, d), lambda i: (i, 0)),
        out_shape=jax.ShapeDtypeStruct((t, d), BF16),
        compiler_params=_params(("parallel",), 40),
        name="rmsnorm",
    )(x, g.reshape(1, d))


def _dependent_zero(x):
    bits = pltpu.bitcast(x, jnp.uint32)
    return pltpu.bitcast((bits >> 16) >> 16, F32)


def _pipelined_matmul_kernel(*refs, n_lhs, n_epi, n_out, epilogue, bm, bn, epi_rows):
    lhs = refs[:n_lhs]
    rhs = refs[n_lhs:2 * n_lhs]
    epi = refs[2 * n_lhs:2 * n_lhs + n_epi]
    outs = refs[2 * n_lhs + n_epi:2 * n_lhs + n_epi + n_out]
    acc_a_ref, acc_b_ref = refs[2 * n_lhs + n_epi + n_out:]
    s = pl.program_id(0)

    @pl.when(s == 0)
    def _():
        acc_b_ref[...] = jnp.zeros(acc_b_ref.shape, acc_b_ref.dtype)

    def step(new_ref, old_ref):
        zero = None
        for r0 in range(0, bm, epi_rows):
            chunk = old_ref[r0:r0 + epi_rows, :]
            if zero is not None:
                chunk = chunk + zero
            last = epilogue(chunk, r0, epi, outs)
            zero = jnp.concatenate([_dependent_zero(last)] * (bn // 128), axis=1)
        acc = jnp.dot(lhs[0][...], rhs[0][...], preferred_element_type=F32)
        for a_ref, w_ref in zip(lhs[1:], rhs[1:]):
            acc = acc + jnp.dot(a_ref[...], w_ref[...], preferred_element_type=F32)
        new_ref[...] = acc

    @pl.when(s % 2 == 0)
    def _():
        step(acc_a_ref, acc_b_ref)

    @pl.when(s % 2 == 1)
    def _():
        step(acc_b_ref, acc_a_ref)


def _pipelined_matmul(name, lhs, rhs, rhs_blocks, epi, epi_specs, out_specs, out_shapes, epilogue,
                      *, nj, bm, bn, epi_rows, vmem_mib):
    ni = lhs[0].shape[0] // bm
    n_blocks = ni * nj

    def cur_ij(s):
        return jnp.minimum(s // nj, ni - 1), s % nj

    def prev_ij(s):
        sp = jnp.maximum(s - 1, 0)
        return sp // nj, sp % nj

    def cur_row(s):
        return cur_ij(s)[0], 0

    in_specs = [pl.BlockSpec((bm, a.shape[1]), cur_row) for a in lhs]
    in_specs += [pl.BlockSpec((a.shape[1], bn),
                              functools.partial(lambda s, blk: blk(cur_ij(s)[1]), blk=blk))
                 for a, blk in zip(lhs, rhs_blocks)]
    in_specs += [pl.BlockSpec(shape, functools.partial(lambda s, fn: fn(*prev_ij(s)), fn=fn))
                 for shape, fn in epi_specs]
    out_block_specs = [pl.BlockSpec(shape, functools.partial(lambda s, fn: fn(*prev_ij(s)), fn=fn))
                       for shape, fn in out_specs]
    return pl.pallas_call(
        functools.partial(_pipelined_matmul_kernel, n_lhs=len(lhs), n_epi=len(epi),
                          n_out=len(out_shapes), epilogue=epilogue, bm=bm, bn=bn,
                          epi_rows=epi_rows),
        grid=(n_blocks + 1,),
        in_specs=in_specs,
        out_specs=out_block_specs,
        out_shape=out_shapes,
        scratch_shapes=[pltpu.VMEM((bm, bn), F32), pltpu.VMEM((bm, bn), F32)],
        compiler_params=_params(("arbitrary",), vmem_mib),
        name=name,
    )(*lhs, *rhs, *epi)


def _inproj_kernel(a_ref, w_ref, o_ref, *, n_gelu_blocks, heads_per_block):
    acc = jnp.dot(a_ref[...], w_ref[...], preferred_element_type=F32)
    j = pl.program_id(1)

    def store(val):
        for hh in range(heads_per_block):
            o_ref[hh] = val[:, hh * HEAD_DIM:(hh + 1) * HEAD_DIM].astype(o_ref.dtype)

    @pl.when(j < n_gelu_blocks)
    def _():
        store(_gelu(acc))

    @pl.when(j >= n_gelu_blocks)
    def _():
        store(acc)


def _inproj(hn, w, n_gelu_cols, bm=1024, bn=1024):
    t, k = hn.shape
    n = w.shape[1]
    hpb = bn // HEAD_DIM
    return pl.pallas_call(
        functools.partial(_inproj_kernel, n_gelu_blocks=n_gelu_cols // bn, heads_per_block=hpb),
        grid=(t // bm, n // bn),
        in_specs=[pl.BlockSpec((bm, k), lambda i, j: (i, 0)),
                  pl.BlockSpec((k, bn), lambda i, j: (0, j))],
        out_specs=pl.BlockSpec((hpb, bm, HEAD_DIM), lambda i, j: (j, i, 0)),
        out_shape=jax.ShapeDtypeStruct((n // HEAD_DIM, t, HEAD_DIM), BF16),
        compiler_params=_params(("parallel", "arbitrary"), 56),
        name="inproj",
    )(hn, w)


def _gmlp_kernel(u_ref, v_ref, ws_ref, bs_ref, gv_ref, og_ref, o_ref, *, n_groups):
    ws = ws_ref[0]
    bs = bs_ref[0]
    gv = gv_ref[0]
    og = og_ref[0]

    def group_body(gi, carry):
        base = pl.multiple_of(gi * (GMLP_GROUP * CHUNK), GMLP_GROUP * CHUNK)
        for c in range(GMLP_GROUP):
            sl = pl.ds(base + c * CHUNK, CHUNK)
            vn = _rms(v_ref[0, sl, :].astype(F32), gv).astype(BF16)
            mixed = jnp.dot(ws, vn, preferred_element_type=F32) + bs
            a = u_ref[0, sl, :].astype(F32) * mixed
            o_ref[sl, :] = _rms(a, og).astype(o_ref.dtype)
        return carry

    lax.fori_loop(0, n_groups, group_body, 0)


def _gmlp(z, ws, bs, gv, og, n_heads, tb=8192):
    t = z.shape[1]
    return pl.pallas_call(
        functools.partial(_gmlp_kernel, n_groups=tb // (GMLP_GROUP * CHUNK)),
        grid=(n_heads, t // tb),
        in_specs=[pl.BlockSpec((1, tb, HEAD_DIM), lambda h, i: (h, i, 0)),
                  pl.BlockSpec((1, tb, HEAD_DIM), lambda h, i: (n_heads + h, i, 0)),
                  pl.BlockSpec((1, CHUNK, CHUNK), lambda h, i: (h, 0, 0)),
                  pl.BlockSpec((1, CHUNK, 1), lambda h, i: (h, 0, 0)),
                  pl.BlockSpec((1, 1, HEAD_DIM), lambda h, i: (h, 0, 0)),
                  pl.BlockSpec((1, 1, HEAD_DIM), lambda h, i: (h, 0, 0))],
        out_specs=pl.BlockSpec((tb, HEAD_DIM), lambda h, i: (i, h)),
        out_shape=jax.ShapeDtypeStruct((t, n_heads * HEAD_DIM), BF16),
        compiler_params=_params(("parallel", "parallel"), 24),
        name="gmlp",
    )(z, z, ws.astype(BF16), bs.reshape(n_heads, CHUNK, 1),
      gv.reshape(n_heads, 1, HEAD_DIM), og.reshape(n_heads, 1, HEAD_DIM))


def _na_toeplitz(rpb):
    col = np.arange(GRID_W)
    cs = np.clip(col - WIN_COLS // 2, 0, GRID_W - WIN_COLS)
    kc = np.arange(GRID_W)
    valid = (kc[None, :] >= cs[:, None]) & (kc[None, :] < cs[:, None] + WIN_COLS)
    dc = kc[None, :] - col[:, None] + (WIN_COLS - 1)
    onehot = (dc[:, :, None] == np.arange(2 * WIN_COLS - 1)).astype(np.float32)
    t = jnp.einsum('hrd,ckd->hrck', rpb, onehot, precision=lax.Precision.HIGHEST)
    return jnp.where(valid[None, None], t, MASKED)


def _na_group_geometry(rows):
    def geom(g):
        r0 = g * NA_G
        ws = int(np.clip(r0 - WIN_ROWS // 2, 0, rows - NA_WIN))
        rs = [int(np.clip(r0 + i - WIN_ROWS // 2, 0, rows - WIN_ROWS)) for i in range(NA_G)]
        return r0 - ws, tuple(r - ws for r in rs)

    ng = rows // NA_G
    assert ng >= 3 and all(geom(g) == geom(1) for g in range(1, ng - 1))
    return geom(0), geom(1), geom(ng - 1)


def _na_kernel(q_ref, k_ref, v_ref, t_ref, qg_ref, kg_ref, og_ref, o_ref,
               qn_ref, kn_ref, bias_ref, *, rows, norm_rows):
    log2e = np.float32(np.log2(np.e))
    qg = qg_ref[...] * (np.float32(HEAD_DIM ** -0.5) * log2e)
    kg = kg_ref[...]
    og = og_ref[0]

    masked_tile = jnp.full((GRID_W, GRID_W), MASKED, F32)
    for cls, (roff, rsoff) in enumerate(_na_group_geometry(rows)):
        for ri in range(NA_G):
            for kp in range(NA_WIN // 2):
                pair = []
                for kr in (2 * kp, 2 * kp + 1):
                    in_window = 0 <= kr - rsoff[ri] < WIN_ROWS
                    dr = kr - roff - ri + (WIN_ROWS - 1)
                    pair.append(t_ref[0, dr] * log2e if in_window else masked_tile)
                bias_ref[cls, ri * GRID_W:(ri + 1) * GRID_W, kp * 128:(kp + 1) * 128] = (
                    jnp.concatenate(pair, axis=1))

    def norm_body(c, carry):
        sl = pl.ds(pl.multiple_of(c * norm_rows, norm_rows), norm_rows)
        qn_ref[sl, :] = _rms(q_ref[0, sl, :].astype(F32), qg).astype(BF16)
        kn_ref[sl, :] = _rms(k_ref[0, sl, :].astype(F32), kg).astype(BF16)
        return carry

    lax.fori_loop(0, rows * GRID_W // norm_rows, norm_body, 0)

    ng = rows // NA_G
    nq = NA_G * GRID_W
    nk = NA_WIN * GRID_W

    def group_body(g, carry):
        r0 = g * NA_G
        ws = jnp.clip(r0 - WIN_ROWS // 2, 0, rows - NA_WIN)
        cls = jnp.minimum(g, 1) + jnp.maximum(g - (ng - 2), 0)
        qsl = pl.ds(pl.multiple_of(r0 * GRID_W, nq), nq)
        ksl = pl.ds(pl.multiple_of(ws * GRID_W, nq), nk)
        s = lax.dot_general(qn_ref[qsl, :], kn_ref[ksl, :], (((1,), (1,)), ((), ())),
                            preferred_element_type=F32)
        s = s + bias_ref[cls]
        m = jnp.max(s, axis=-1, keepdims=True)
        e = jnp.exp2(s - m)
        l = jnp.sum(e, axis=-1, keepdims=True)
        o = jnp.dot(e.astype(BF16), v_ref[0, ksl, :], preferred_element_type=F32) / l
        o_ref[qsl, :] = _rms(o, og).astype(o_ref.dtype)
        return carry

    lax.fori_loop(0, ng, group_body, 0, unroll=2)


def _na(z, toeplitz, qg, kg, og, n_heads, batch, seq, head0):
    rows = seq // GRID_W
    assert rows % NA_G == 0 and (WIN_ROWS // 2) % NA_G == 0
    n_dr = 2 * WIN_ROWS - 1
    return pl.pallas_call(
        functools.partial(_na_kernel, rows=rows, norm_rows=512),
        grid=(batch, n_heads),
        in_specs=[pl.BlockSpec((1, seq, HEAD_DIM), lambda b, h: (head0 + h, b, 0)),
                  pl.BlockSpec((1, seq, HEAD_DIM), lambda b, h: (head0 + n_heads + h, b, 0)),
                  pl.BlockSpec((1, seq, HEAD_DIM), lambda b, h: (head0 + 2 * n_heads + h, b, 0)),
                  pl.BlockSpec((1, n_dr, GRID_W, GRID_W), lambda b, h: (h, 0, 0, 0)),
                  pl.BlockSpec((1, HEAD_DIM), lambda b, h: (0, 0)),
                  pl.BlockSpec((1, HEAD_DIM), lambda b, h: (0, 0)),
                  pl.BlockSpec((1, 1, HEAD_DIM), lambda b, h: (h, 0, 0))],
        out_specs=pl.BlockSpec((seq, HEAD_DIM), lambda b, h: (b, h)),
        out_shape=jax.ShapeDtypeStruct((batch * seq, n_heads * HEAD_DIM), BF16),
        scratch_shapes=[pltpu.VMEM((seq, HEAD_DIM), BF16), pltpu.VMEM((seq, HEAD_DIM), BF16),
                        pltpu.VMEM((3, NA_G * GRID_W, NA_WIN * GRID_W), F32)],
        compiler_params=_params(("parallel", "parallel"), 40),
        name="natten",
    )(z, z, z, toeplitz, qg.reshape(1, HEAD_DIM), kg.reshape(1, HEAD_DIM),
      og.reshape(n_heads, 1, HEAD_DIM))


def _residual_epilogue(chunk, r0, epi, outs, *, rows):
    (h_ref,) = epi
    o_ref, ob_ref, ssq_ref = outs
    sl = slice(r0, r0 + rows)
    h = h_ref[sl, :] + chunk
    o_ref[sl, :] = h
    ob_ref[sl, :] = h.astype(ob_ref.dtype)
    ssq = jnp.broadcast_to(jnp.sum(h * h, axis=-1, keepdims=True), (rows, 128))
    ssq_ref[sl, :] = ssq
    return ssq[0:1, :]


def _residual_matmul(name, lhs, rhs, rhs_blocks, h, *, bm, epi_rows, vmem_mib):
    t, n = h.shape
    bn = SSQ_BN
    block = ((bm, bn), lambda i, j: (i, j))
    return _pipelined_matmul(
        name, lhs, rhs, rhs_blocks, [h], [block],
        [block, block, ((bm, 128), lambda i, j: (i, j))],
        [jax.ShapeDtypeStruct((t, n), F32), jax.ShapeDtypeStruct((t, n), BF16),
         jax.ShapeDtypeStruct((t, n // bn * 128), F32)],
        functools.partial(_residual_epilogue, rows=epi_rows),
        nj=n // bn, bm=bm, bn=bn, epi_rows=epi_rows, vmem_mib=vmem_mib)


def _rstd_from_ssq(ssq, d):
    tot = ssq[:, 0:128]
    for c in range(1, ssq.shape[1] // 128):
        tot = tot + ssq[:, c * 128:(c + 1) * 128]
    return lax.rsqrt(tot * np.float32(1.0 / d) + EPS)


def _outproj(mix_a, mix_b, w, x, bm=1024):
    assert mix_a.shape[1] == mix_b.shape[1]
    return _residual_matmul("outproj", [mix_a, mix_b], [w, w],
                            [lambda j: (0, j), lambda j: (1, j)], x,
                            bm=bm, epi_rows=32, vmem_mib=48)


def _ffn_up_kernel(xm_ref, xp_ref, xn_ref, sm_ref, sp_ref, sn_ref, wg_ref, wv_ref,
                   cwg_ref, cwv_ref, cbg_ref, cbv_ref,
                   o_ref, xe_ref, rstd_ref, up_a_ref, up_b_ref,
                   *, bm, bn, nj, n_blocks, blocks_per_seq):
    s = pl.program_id(0)
    i = jnp.minimum(s // nj, n_blocks // nj - 1)
    pos = i % blocks_per_seq
    d = xe_ref.shape[1]

    @pl.when((s % nj == 0) & (s < n_blocks))
    def _():
        xe_ref[HALO:HALO + bm, :] = xm_ref[...]
        rstd_ref[HALO:HALO + bm, :] = _rstd_from_ssq(sm_ref[...], d)
        rstd_ref[0:HALO, :] = _rstd_from_ssq(sp_ref[...], d)
        rstd_ref[HALO + bm:, :] = _rstd_from_ssq(sn_ref[...], d)

        @pl.when(pos == 0)
        def _():
            xe_ref[0:HALO, :] = jnp.zeros((HALO, d), xe_ref.dtype)

        @pl.when(pos != 0)
        def _():
            xe_ref[0:HALO, :] = xp_ref[...]

        @pl.when(pos == blocks_per_seq - 1)
        def _():
            xe_ref[HALO + bm:, :] = jnp.zeros((HALO, d), xe_ref.dtype)

        @pl.when(pos != blocks_per_seq - 1)
        def _():
            xe_ref[HALO + bm:, :] = xn_ref[...]

    @pl.when(s == 0)
    def _():
        up_b_ref[...] = jnp.zeros(up_b_ref.shape, up_b_ref.dtype)

    def step(new_ref, old_ref):
        cw = jnp.concatenate([cwg_ref[...], cwv_ref[...]], axis=1)
        cb = jnp.concatenate([cbg_ref[...], cbv_ref[...]], axis=1)
        w_cur = cw[1:2, :]
        for r0 in range(0, bm, EPI_ROWS):
            slab = old_ref[r0 + HALO - 8:r0 + HALO + EPI_ROWS + 8, :]
            prev = pltpu.roll(slab, 1, 0)[8:8 + EPI_ROWS]
            cur = slab[8:8 + EPI_ROWS]
            nxt = pltpu.roll(slab, EPI_ROWS + 15, 0)[8:8 + EPI_ROWS]
            c = cb + (prev * cw[0:1, :] + cur * w_cur + nxt * cw[2:3, :])
            act = _gelu(c[:, :bn]) * c[:, bn:]
            o_ref[r0:r0 + EPI_ROWS, :] = act.astype(o_ref.dtype)
            zero = _dependent_zero(act[0:1, 0:128])
            w_cur = cw[1:2, :] + jnp.concatenate([zero] * (2 * bn // 128), axis=1)

        xe = xe_ref[...]
        rstd = jnp.concatenate([rstd_ref[...]] * (bn // 128), axis=1)
        new_ref[:, :bn] = jnp.dot(xe, wg_ref[...], preferred_element_type=F32) * rstd
        new_ref[:, bn:] = jnp.dot(xe, wv_ref[...], preferred_element_type=F32) * rstd

    @pl.when(s % 2 == 0)
    def _():
        step(up_a_ref, up_b_ref)

    @pl.when(s % 2 == 1)
    def _():
        step(up_b_ref, up_a_ref)


def _ffn_up(xn, ssq, w, cw, cb, seq, bm=1024, bn=256):
    t, k = xn.shape
    sw = ssq.shape[1]
    n = w.shape[1] // 2
    ni, nj = t // bm, n // bn
    n_blocks = ni * nj
    hb = bm // HALO
    n_halo_blocks = t // HALO

    def cur_i(s):
        return jnp.minimum(s // nj, ni - 1)

    def halo_before(s):
        return jnp.maximum(cur_i(s) * hb - 1, 0)

    def halo_after(s):
        return jnp.minimum((cur_i(s) + 1) * hb, n_halo_blocks - 1)

    def prev_ij(s):
        sp = jnp.maximum(s - 1, 0)
        return sp // nj, sp % nj

    return pl.pallas_call(
        functools.partial(_ffn_up_kernel, bm=bm, bn=bn, nj=nj, n_blocks=n_blocks,
                          blocks_per_seq=seq // bm),
        grid=(n_blocks + 1,),
        in_specs=[pl.BlockSpec((bm, k), lambda s: (cur_i(s), 0)),
                  pl.BlockSpec((HALO, k), lambda s: (halo_before(s), 0)),
                  pl.BlockSpec((HALO, k), lambda s: (halo_after(s), 0)),
                  pl.BlockSpec((bm, sw), lambda s: (cur_i(s), 0)),
                  pl.BlockSpec((HALO, sw), lambda s: (halo_before(s), 0)),
                  pl.BlockSpec((HALO, sw), lambda s: (halo_after(s), 0)),
                  pl.BlockSpec((k, bn), lambda s: (0, s % nj)),
                  pl.BlockSpec((k, bn), lambda s: (0, nj + s % nj)),
                  pl.BlockSpec((3, bn), lambda s: (0, prev_ij(s)[1])),
                  pl.BlockSpec((3, bn), lambda s: (0, nj + prev_ij(s)[1])),
                  pl.BlockSpec((1, bn), lambda s: (0, prev_ij(s)[1])),
                  pl.BlockSpec((1, bn), lambda s: (0, nj + prev_ij(s)[1]))],
        out_specs=pl.BlockSpec((bm, bn), lambda s: prev_ij(s)),
        out_shape=jax.ShapeDtypeStruct((t, n), BF16),
        scratch_shapes=[pltpu.VMEM((bm + 2 * HALO, k), BF16),
                        pltpu.VMEM((bm + 2 * HALO, 128), F32),
                        pltpu.VMEM((bm + 2 * HALO, 2 * bn), F32),
                        pltpu.VMEM((bm + 2 * HALO, 2 * bn), F32)],
        compiler_params=_params(("arbitrary",), 56),
        name="ffn_up",
    )(xn, xn, xn, ssq, ssq, ssq, w, w, cw, cw, cb, cb)


def _ffn_down(act, w, h, bm=512):
    return _residual_matmul("ffn_down", [act], [w], [lambda j: (0, j)], h,
                            bm=bm, epi_rows=16, vmem_mib=56)


def _ple_embed_kernel(p_ref, w_ref, g_ref, o_ref):
    e = jnp.dot(p_ref[...].astype(BF16), w_ref[...], preferred_element_type=F32)
    o_ref[...] = _rms(e, g_ref[...]).astype(o_ref.dtype)


def _ple_embed(p, w, g, bm=512):
    t, k = p.shape
    n = w.shape[1]
    return pl.pallas_call(
        _ple_embed_kernel,
        grid=(t // bm,),
        in_specs=[pl.BlockSpec((bm, k), lambda i: (i, 0)),
                  pl.BlockSpec((k, n), lambda i: (0, 0)),
                  pl.BlockSpec((1, n), lambda i: (0, 0))],
        out_specs=pl.BlockSpec((bm, n), lambda i: (i, 0)),
        out_shape=jax.ShapeDtypeStruct((t, n), BF16),
        compiler_params=_params(("parallel",), 40),
        name="ple_embed",
    )(p, w, g.reshape(1, n))


def _ple_gate_epilogue(chunk, r0, epi, outs, *, rows, d):
    ssq_ref, e_ref, h_ref = epi
    (o_ref,) = outs
    sl = slice(r0, r0 + rows)
    rstd = _rstd_from_ssq(ssq_ref[sl, :], d)
    rstd = jnp.concatenate([rstd] * (chunk.shape[1] // 128), axis=1)
    out = h_ref[sl, :] + jax.nn.sigmoid(chunk * rstd) * e_ref[sl, :].astype(F32)
    o_ref[sl, :] = out
    return out[0:1, 0:128]


def _ple_gate(hb, ssq, w, e, h, bm=1024, bn=512, epi_rows=8):
    t, k = hb.shape
    n = w.shape[1]
    block = ((bm, bn), lambda i, j: (i, j))
    return _pipelined_matmul(
        "ple_gate", [hb], [w], [lambda j: (0, j)],
        [ssq, e, h], [((bm, ssq.shape[1]), lambda i, j: (i, 0)), block, block],
        [block], [jax.ShapeDtypeStruct((t, n), F32)],
        functools.partial(_ple_gate_epilogue, rows=epi_rows, d=k),
        nj=n // bn, bm=bm, bn=bn, epi_rows=epi_rows, vmem_mib=48)[0]


def kernel(x, p, norm_mix_g, w_in, gmlp_v_g, gmlp_ws, gmlp_bs, q_norm_g, k_norm_g, na_rpb,
           out_norm_a_g, out_norm_b_g, w_out, norm_ffn_g, w_up, conv_w, conv_b, w_down,
           norm_ple_g, w_ple_gate, w_ple_proj, ple_post_g):
    batch, seq, d_model = x.shape
    t = batch * seq
    depth = w_in.shape[0]
    n_a = gmlp_ws.shape[1]
    n_b = na_rpb.shape[1]
    h = x.reshape(t, d_model)
    for i in range(depth):
        hn = _rmsnorm(h, norm_mix_g[i])
        z = _inproj(hn, w_in[i].astype(BF16), 2 * n_a * HEAD_DIM)
        mix_a = _gmlp(z, gmlp_ws[i], gmlp_bs[i], gmlp_v_g[i], out_norm_a_g[i], n_a)
        mix_b = _na(z, _na_toeplitz(na_rpb[i]), q_norm_g[i], k_norm_g[i], out_norm_b_g[i],
                    n_b, batch, seq, 2 * n_a)
        h, hb, ssq = _outproj(mix_a, mix_b, w_out[i].astype(BF16), h)
        act = _ffn_up(hb, ssq, (norm_ffn_g[i][:, None] * w_up[i]).astype(BF16),
                      conv_w[i], conv_b[i].reshape(1, -1), seq)
        h, hb, ssq = _ffn_down(act, w_down[i].astype(BF16), h)
        e = _ple_embed(p[i].reshape(t, -1), w_ple_proj[i].astype(BF16), ple_post_g[i])
        h = _ple_gate(hb, ssq, (norm_ple_g[i][:, None] * w_ple_gate[i]).astype(BF16), e, h)
    return h.reshape(batch, seq, d_model)
```

```python
import functools

import numpy as np
import jax
import jax.numpy as jnp
from jax import lax
from jax.experimental import pallas as pl
from jax.experimental.pallas import tpu as pltpu

F32 = jnp.float32
BF16 = jnp.bfloat16

EPS = 1e-6
HEAD_DIM = 128
CHUNK = 128
GMLP_GROUP = 8
GRID_W = 64
WIN_ROWS = 8
WIN_COLS = 16
NA_G = 4
NA_WIN = NA_G + WIN_ROWS
MASKED = -1e30
NA_LOG2E = float(np.log2(np.e))
HALO = 16
EPI_ROWS = 8
SSQ_BN = 512
MIB = 1 << 20


def _params(semantics, vmem_mib, flags=None):
    return pltpu.CompilerParams(dimension_semantics=semantics,
                                vmem_limit_bytes=vmem_mib * MIB, flags=flags)


def _rms(x, g):
    ms = jnp.mean(x * x, axis=-1, keepdims=True)
    return x * lax.rsqrt(ms + EPS) * g


def _gelu(x):
    return 0.5 * x * (1.0 + lax.erf(x * np.float32(np.sqrt(0.5))))


def _rmsnorm_kernel(x_ref, g_ref, o_ref):
    o_ref[...] = _rms(x_ref[...], g_ref[...]).astype(o_ref.dtype)


def _rmsnorm(x, g, bm=512):
    t, d = x.shape
    return pl.pallas_call(
        _rmsnorm_kernel,
        grid=(t // bm,),
        in_specs=[pl.BlockSpec((bm, d), lambda i: (i, 0)),
                  pl.BlockSpec((1, d), lambda i: (0, 0))],
        out_specs=pl.BlockSpec((bm, d), lambda i: (i, 0)),
        out_shape=jax.ShapeDtypeStruct((t, d), BF16),
        compiler_params=_params(("parallel",), 40),
        name="rmsnorm",
    )(x, g.reshape(1, d))


def _dependent_zero(x):
    bits = pltpu.bitcast(x, jnp.uint32)
    return pltpu.bitcast((bits >> 16) >> 16, F32)


def _pipelined_matmul_kernel(*refs, n_lhs, n_epi, n_out, epilogue, bm, bn, epi_rows):
    lhs = refs[:n_lhs]
    rhs = refs[n_lhs:2 * n_lhs]
    epi = refs[2 * n_lhs:2 * n_lhs + n_epi]
    outs = refs[2 * n_lhs + n_epi:2 * n_lhs + n_epi + n_out]
    acc_a_ref, acc_b_ref = refs[2 * n_lhs + n_epi + n_out:]
    s = pl.program_id(0)

    @pl.when(s == 0)
    def _():
        acc_b_ref[...] = jnp.zeros(acc_b_ref.shape, acc_b_ref.dtype)

    def step(new_ref, old_ref):
        zero = None
        for r0 in range(0, bm, epi_rows):
            chunk = old_ref[r0:r0 + epi_rows, :]
            if zero is not None:
                chunk = chunk + zero
            last = epilogue(chunk, r0, epi, outs)
            zero = jnp.concatenate([_dependent_zero(last)] * (bn // 128), axis=1)
        acc = jnp.dot(lhs[0][...], rhs[0][...], preferred_element_type=F32)
        for a_ref, w_ref in zip(lhs[1:], rhs[1:]):
            acc = acc + jnp.dot(a_ref[...], w_ref[...], preferred_element_type=F32)
        new_ref[...] = acc

    @pl.when(s % 2 == 0)
    def _():
        step(acc_a_ref, acc_b_ref)

    @pl.when(s % 2 == 1)
    def _():
        step(acc_b_ref, acc_a_ref)


def _pipelined_matmul(name, lhs, rhs, rhs_blocks, epi, epi_specs, out_specs, out_shapes, epilogue,
                      *, nj, bm, bn, epi_rows, vmem_mib):
    ni = lhs[0].shape[0] // bm
    n_blocks = ni * nj

    def cur_ij(s):
        return jnp.minimum(s // nj, ni - 1), s % nj

    def prev_ij(s):
        sp = jnp.maximum(s - 1, 0)
        return sp // nj, sp % nj

    def cur_row(s):
        return cur_ij(s)[0], 0

    in_specs = [pl.BlockSpec((bm, a.shape[1]), cur_row) for a in lhs]
    in_specs += [pl.BlockSpec((a.shape[1], bn),
                              functools.partial(lambda s, blk: blk(cur_ij(s)[1]), blk=blk))
                 for a, blk in zip(lhs, rhs_blocks)]
    in_specs += [pl.BlockSpec(shape, functools.partial(lambda s, fn: fn(*prev_ij(s)), fn=fn))
                 for shape, fn in epi_specs]
    out_block_specs = [pl.BlockSpec(shape, functools.partial(lambda s, fn: fn(*prev_ij(s)), fn=fn))
                       for shape, fn in out_specs]
    return pl.pallas_call(
        functools.partial(_pipelined_matmul_kernel, n_lhs=len(lhs), n_epi=len(epi),
                          n_out=len(out_shapes), epilogue=epilogue, bm=bm, bn=bn,
                          epi_rows=epi_rows),
        grid=(n_blocks + 1,),
        in_specs=in_specs,
        out_specs=out_block_specs,
        out_shape=out_shapes,
        scratch_shapes=[pltpu.VMEM((bm, bn), F32), pltpu.VMEM((bm, bn), F32)],
        compiler_params=_params(("arbitrary",), vmem_mib),
        name=name,
    )(*lhs, *rhs, *epi)


def _heads_epilogue(chunk, r0, epi, outs, *, rows, kind):
    (o_ref,) = outs
    y = None
    for hh in range(chunk.shape[1] // HEAD_DIM):
        cols = slice(hh * HEAD_DIM, (hh + 1) * HEAD_DIM)
        y = chunk[:, cols]
        if kind == "gelu":
            y = _gelu(y)
        elif kind == "norm":
            y = _rms(y, epi[0][:, cols])
        o_ref[hh, r0:r0 + rows, :] = y.astype(o_ref.dtype)
    return y[0:1, :]


def _inproj(hn, w, col0, n_cols, kind, gain=None, bm=1024, bn=1024):
    t = hn.shape[0]
    hpb = bn // HEAD_DIM
    epi_rows = {"norm": 32, "gelu": 8, "plain": 16}[kind]
    epi, epi_specs = [], []
    if kind == "norm":
        epi, epi_specs = [gain], [((1, bn), lambda i, j: (0, j))]
    return _pipelined_matmul(
        "inproj_" + kind, [hn], [w], [lambda j: (0, col0 // bn + j)], epi, epi_specs,
        [((hpb, bm, HEAD_DIM), lambda i, j: (j, i, 0))],
        [jax.ShapeDtypeStruct((n_cols // HEAD_DIM, t, HEAD_DIM), BF16)],
        functools.partial(_heads_epilogue, rows=epi_rows, kind=kind),
        nj=n_cols // bn, bm=bm, bn=bn, epi_rows=epi_rows, vmem_mib=56)[0]


def _gmlp_kernel(u_ref, v_ref, ws_ref, bs_ref, gv_ref, og_ref, o_ref, *, n_groups):
    ws = ws_ref[0]
    bs = bs_ref[0]
    gv = gv_ref[0]
    og = og_ref[0]

    def group_body(gi, carry):
        base = pl.multiple_of(gi * (GMLP_GROUP * CHUNK), GMLP_GROUP * CHUNK)
        for c in range(GMLP_GROUP):
            sl = pl.ds(base + c * CHUNK, CHUNK)
            vn = _rms(v_ref[0, sl, :].astype(F32), gv).astype(BF16)
            mixed = jnp.dot(ws, vn, preferred_element_type=F32) + bs
            a = u_ref[0, sl, :].astype(F32) * mixed
            o_ref[sl, :] = _rms(a, og).astype(o_ref.dtype)
        return carry

    lax.fori_loop(0, n_groups, group_body, 0)


def _gmlp(z, ws, bs, gv, og, n_heads, tb=8192):
    t = z.shape[1]
    return pl.pallas_call(
        functools.partial(_gmlp_kernel, n_groups=tb // (GMLP_GROUP * CHUNK)),
        grid=(n_heads, t // tb),
        in_specs=[pl.BlockSpec((1, tb, HEAD_DIM), lambda h, i: (h, i, 0)),
                  pl.BlockSpec((1, tb, HEAD_DIM), lambda h, i: (n_heads + h, i, 0)),
                  pl.BlockSpec((1, CHUNK, CHUNK), lambda h, i: (h, 0, 0)),
                  pl.BlockSpec((1, CHUNK, 1), lambda h, i: (h, 0, 0)),
                  pl.BlockSpec((1, 1, HEAD_DIM), lambda h, i: (h, 0, 0)),
                  pl.BlockSpec((1, 1, HEAD_DIM), lambda h, i: (h, 0, 0))],
        out_specs=pl.BlockSpec((tb, HEAD_DIM), lambda h, i: (i, h)),
        out_shape=jax.ShapeDtypeStruct((t, n_heads * HEAD_DIM), BF16),
        compiler_params=_params(("parallel", "parallel"), 24),
        name="gmlp",
    )(z, z, ws.astype(BF16), bs.reshape(n_heads, CHUNK, 1),
      gv.reshape(n_heads, 1, HEAD_DIM), og.reshape(n_heads, 1, HEAD_DIM))


def _na_toeplitz(rpb):
    col = np.arange(GRID_W)
    cs = np.clip(col - WIN_COLS // 2, 0, GRID_W - WIN_COLS)
    kc = np.arange(GRID_W)
    valid = (kc[None, :] >= cs[:, None]) & (kc[None, :] < cs[:, None] + WIN_COLS)
    dc = kc[None, :] - col[:, None] + (WIN_COLS - 1)
    onehot = (dc[:, :, None] == np.arange(2 * WIN_COLS - 1)).astype(np.float32)
    t = jnp.einsum('hrd,ckd->hrck', rpb, onehot, precision=lax.Precision.HIGHEST)
    return jnp.where(valid[None, None], t, MASKED)


def _na_group_geometry(rows):
    def geom(g):
        r0 = g * NA_G
        ws = int(np.clip(r0 - WIN_ROWS // 2, 0, rows - NA_WIN))
        rs = [int(np.clip(r0 + i - WIN_ROWS // 2, 0, rows - WIN_ROWS)) for i in range(NA_G)]
        return r0 - ws, tuple(r - ws for r in rs)

    ng = rows // NA_G
    assert ng >= 3 and all(geom(g) == geom(1) for g in range(1, ng - 1))
    return geom(0), geom(1), geom(ng - 1)


def _na_kernel(q_ref, k_ref, v_ref, t_ref, og_ref, o_ref, bias_ref, *, rows):
    log2e = np.float32(NA_LOG2E)
    og = og_ref[0]

    masked_tile = jnp.full((GRID_W, GRID_W), MASKED, F32)
    for cls, (roff, rsoff) in enumerate(_na_group_geometry(rows)):
        for ri in range(NA_G):
            for kp in range(NA_WIN // 2):
                pair = []
                for kr in (2 * kp, 2 * kp + 1):
                    in_window = 0 <= kr - rsoff[ri] < WIN_ROWS
                    dr = kr - roff - ri + (WIN_ROWS - 1)
                    pair.append(t_ref[0, dr] * log2e if in_window else masked_tile)
                bias_ref[cls, ri * GRID_W:(ri + 1) * GRID_W, kp * 128:(kp + 1) * 128] = (
                    jnp.concatenate(pair, axis=1))

    ng = rows // NA_G
    nq = NA_G * GRID_W
    nk = NA_WIN * GRID_W

    def group_body(g, carry):
        r0 = g * NA_G
        ws = jnp.clip(r0 - WIN_ROWS // 2, 0, rows - NA_WIN)
        cls = jnp.minimum(g, 1) + jnp.maximum(g - (ng - 2), 0)
        qsl = pl.ds(pl.multiple_of(r0 * GRID_W, nq), nq)
        ksl = pl.ds(pl.multiple_of(ws * GRID_W, nq), nk)
        s = lax.dot_general(q_ref[0, qsl, :], k_ref[0, ksl, :], (((1,), (1,)), ((), ())),
                            preferred_element_type=F32)
        s = s + bias_ref[cls]
        m = jnp.max(s, axis=-1, keepdims=True)
        e = jnp.exp2(s - m)
        l = jnp.sum(e, axis=-1, keepdims=True)
        o = jnp.dot(e.astype(BF16), v_ref[0, ksl, :], preferred_element_type=F32) / l
        o_ref[qsl, :] = _rms(o, og).astype(o_ref.dtype)
        return carry

    lax.fori_loop(0, ng, group_body, 0, unroll=2)


def _na_qk_gain(qg, kg, n_heads):
    qg = qg * np.float32(HEAD_DIM ** -0.5 * NA_LOG2E)
    return jnp.concatenate([jnp.tile(qg, n_heads), jnp.tile(kg, n_heads)]).reshape(1, -1)


def _na(zqk, zv, toeplitz, og, n_heads, batch, seq):
    rows = seq // GRID_W
    assert rows % NA_G == 0 and (WIN_ROWS // 2) % NA_G == 0
    n_dr = 2 * WIN_ROWS - 1
    return pl.pallas_call(
        functools.partial(_na_kernel, rows=rows),
        grid=(batch, n_heads),
        in_specs=[pl.BlockSpec((1, seq, HEAD_DIM), lambda b, h: (h, b, 0)),
                  pl.BlockSpec((1, seq, HEAD_DIM), lambda b, h: (n_heads + h, b, 0)),
                  pl.BlockSpec((1, seq, HEAD_DIM), lambda b, h: (h, b, 0)),
                  pl.BlockSpec((1, n_dr, GRID_W, GRID_W), lambda b, h: (h, 0, 0, 0)),
                  pl.BlockSpec((1, 1, HEAD_DIM), lambda b, h: (h, 0, 0))],
        out_specs=pl.BlockSpec((seq, HEAD_DIM), lambda b, h: (b, h)),
        out_shape=jax.ShapeDtypeStruct((batch * seq, n_heads * HEAD_DIM), BF16),
        scratch_shapes=[pltpu.VMEM((3, NA_G * GRID_W, NA_WIN * GRID_W), F32)],
        compiler_params=_params(("parallel", "parallel"), 40),
        name="natten",
    )(zqk, zqk, zv, toeplitz, og.reshape(n_heads, 1, HEAD_DIM))


def _residual_epilogue(chunk, r0, epi, outs, *, rows):
    (h_ref,) = epi
    o_ref, ob_ref, ssq_ref = outs
    sl = slice(r0, r0 + rows)
    h = h_ref[sl, :] + chunk
    o_ref[sl, :] = h
    ob_ref[sl, :] = h.astype(ob_ref.dtype)
    ssq = jnp.broadcast_to(jnp.sum(h * h, axis=-1, keepdims=True), (rows, 128))
    ssq_ref[sl, :] = ssq
    return ssq[0:1, :]


def _residual_matmul(name, lhs, rhs, rhs_blocks, h, *, bm, epi_rows, vmem_mib):
    t, n = h.shape
    bn = SSQ_BN
    block = ((bm, bn), lambda i, j: (i, j))
    return _pipelined_matmul(
        name, lhs, rhs, rhs_blocks, [h], [block],
        [block, block, ((bm, 128), lambda i, j: (i, j))],
        [jax.ShapeDtypeStruct((t, n), F32), jax.ShapeDtypeStruct((t, n), BF16),
         jax.ShapeDtypeStruct((t, n // bn * 128), F32)],
        functools.partial(_residual_epilogue, rows=epi_rows),
        nj=n // bn, bm=bm, bn=bn, epi_rows=epi_rows, vmem_mib=vmem_mib)


def _rstd_from_ssq(ssq, d):
    tot = ssq[:, 0:128]
    for c in range(1, ssq.shape[1] // 128):
        tot = tot + ssq[:, c * 128:(c + 1) * 128]
    return lax.rsqrt(tot * np.float32(1.0 / d) + EPS)


def _outproj(mix_a, mix_b, w, x, bm=1024):
    assert mix_a.shape[1] == mix_b.shape[1]
    return _residual_matmul("outproj", [mix_a, mix_b], [w, w],
                            [lambda j: (0, j), lambda j: (1, j)], x,
                            bm=bm, epi_rows=32, vmem_mib=48)


def _ffn_up_kernel(xm_ref, xp_ref, xn_ref, sm_ref, sp_ref, sn_ref, wg_ref, wv_ref,
                   cwg_ref, cwv_ref, cbg_ref, cbv_ref,
                   o_ref, xe_ref, rstd_ref, up_a_ref, up_b_ref,
                   *, bm, bn, nj, n_blocks, blocks_per_seq):
    s = pl.program_id(0)
    i = jnp.minimum(s // nj, n_blocks // nj - 1)
    pos = i % blocks_per_seq
    d = xe_ref.shape[1]

    @pl.when((s % nj == 0) & (s < n_blocks))
    def _():
        xe_ref[HALO:HALO + bm, :] = xm_ref[...]
        rstd_ref[HALO:HALO + bm, :] = _rstd_from_ssq(sm_ref[...], d)
        rstd_ref[0:HALO, :] = _rstd_from_ssq(sp_ref[...], d)
        rstd_ref[HALO + bm:, :] = _rstd_from_ssq(sn_ref[...], d)

        @pl.when(pos == 0)
        def _():
            xe_ref[0:HALO, :] = jnp.zeros((HALO, d), xe_ref.dtype)

        @pl.when(pos != 0)
        def _():
            xe_ref[0:HALO, :] = xp_ref[...]

        @pl.when(pos == blocks_per_seq - 1)
        def _():
            xe_ref[HALO + bm:, :] = jnp.zeros((HALO, d), xe_ref.dtype)

        @pl.when(pos != blocks_per_seq - 1)
        def _():
            xe_ref[HALO + bm:, :] = xn_ref[...]

    @pl.when(s == 0)
    def _():
        up_b_ref[...] = jnp.zeros(up_b_ref.shape, up_b_ref.dtype)

    def step(new_ref, old_ref):
        cw = jnp.concatenate([cwg_ref[...], cwv_ref[...]], axis=1)
        cb = jnp.concatenate([cbg_ref[...], cbv_ref[...]], axis=1)
        w_cur = cw[1:2, :]
        for r0 in range(0, bm, EPI_ROWS):
            slab = old_ref[r0 + HALO - 8:r0 + HALO + EPI_ROWS + 8, :]
            prev = pltpu.roll(slab, 1, 0)[8:8 + EPI_ROWS]
            cur = slab[8:8 + EPI_ROWS]
            nxt = pltpu.roll(slab, EPI_ROWS + 15, 0)[8:8 + EPI_ROWS]
            c = cb + (prev * cw[0:1, :] + cur * w_cur + nxt * cw[2:3, :])
            act = _gelu(c[:, :bn]) * c[:, bn:]
            o_ref[r0:r0 + EPI_ROWS, :] = act.astype(o_ref.dtype)
            zero = _dependent_zero(act[0:1, 0:128])
            w_cur = cw[1:2, :] + jnp.concatenate([zero] * (2 * bn // 128), axis=1)

        xe = xe_ref[...]
        rstd = jnp.concatenate([rstd_ref[...]] * (bn // 128), axis=1)
        new_ref[:, :bn] = jnp.dot(xe, wg_ref[...], preferred_element_type=F32) * rstd
        new_ref[:, bn:] = jnp.dot(xe, wv_ref[...], preferred_element_type=F32) * rstd

    @pl.when(s % 2 == 0)
    def _():
        step(up_a_ref, up_b_ref)

    @pl.when(s % 2 == 1)
    def _():
        step(up_b_ref, up_a_ref)


def _ffn_up(xn, ssq, w, cw, cb, seq, bm=1024, bn=256):
    t, k = xn.shape
    sw = ssq.shape[1]
    n = w.shape[1] // 2
    ni, nj = t // bm, n // bn
    n_blocks = ni * nj
    hb = bm // HALO
    n_halo_blocks = t // HALO

    def cur_i(s):
        return jnp.minimum(s // nj, ni - 1)

    def halo_before(s):
        return jnp.maximum(cur_i(s) * hb - 1, 0)

    def halo_after(s):
        return jnp.minimum((cur_i(s) + 1) * hb, n_halo_blocks - 1)

    def prev_ij(s):
        sp = jnp.maximum(s - 1, 0)
        return sp // nj, sp % nj

    return pl.pallas_call(
        functools.partial(_ffn_up_kernel, bm=bm, bn=bn, nj=nj, n_blocks=n_blocks,
                          blocks_per_seq=seq // bm),
        grid=(n_blocks + 1,),
        in_specs=[pl.BlockSpec((bm, k), lambda s: (cur_i(s), 0)),
                  pl.BlockSpec((HALO, k), lambda s: (halo_before(s), 0)),
                  pl.BlockSpec((HALO, k), lambda s: (halo_after(s), 0)),
                  pl.BlockSpec((bm, sw), lambda s: (cur_i(s), 0)),
                  pl.BlockSpec((HALO, sw), lambda s: (halo_before(s), 0)),
                  pl.BlockSpec((HALO, sw), lambda s: (halo_after(s), 0)),
                  pl.BlockSpec((k, bn), lambda s: (0, s % nj)),
                  pl.BlockSpec((k, bn), lambda s: (0, nj + s % nj)),
                  pl.BlockSpec((3, bn), lambda s: (0, prev_ij(s)[1])),
                  pl.BlockSpec((3, bn), lambda s: (0, nj + prev_ij(s)[1])),
                  pl.BlockSpec((1, bn), lambda s: (0, prev_ij(s)[1])),
                  pl.BlockSpec((1, bn), lambda s: (0, nj + prev_ij(s)[1]))],
        out_specs=pl.BlockSpec((bm, bn), lambda s: prev_ij(s)),
        out_shape=jax.ShapeDtypeStruct((t, n), BF16),
        scratch_shapes=[pltpu.VMEM((bm + 2 * HALO, k), BF16),
                        pltpu.VMEM((bm + 2 * HALO, 128), F32),
                        pltpu.VMEM((bm + 2 * HALO, 2 * bn), F32),
                        pltpu.VMEM((bm + 2 * HALO, 2 * bn), F32)],
        compiler_params=_params(("arbitrary",), 56),
        name="ffn_up",
    )(xn, xn, xn, ssq, ssq, ssq, w, w, cw, cw, cb, cb)


def _ffn_down(act, w, h, bm=512):
    return _residual_matmul("ffn_down", [act], [w], [lambda j: (0, j)], h,
                            bm=bm, epi_rows=16, vmem_mib=56)


def _ple_embed_kernel(p_ref, w_ref, g_ref, o_ref):
    e = jnp.dot(p_ref[...].astype(BF16), w_ref[...], preferred_element_type=F32)
    o_ref[...] = _rms(e, g_ref[...]).astype(o_ref.dtype)


def _ple_embed(p, w, g, bm=512):
    t, k = p.shape
    n = w.shape[1]
    return pl.pallas_call(
        _ple_embed_kernel,
        grid=(t // bm,),
        in_specs=[pl.BlockSpec((bm, k), lambda i: (i, 0)),
                  pl.BlockSpec((k, n), lambda i: (0, 0)),
                  pl.BlockSpec((1, n), lambda i: (0, 0))],
        out_specs=pl.BlockSpec((bm, n), lambda i: (i, 0)),
        out_shape=jax.ShapeDtypeStruct((t, n), BF16),
        compiler_params=_params(("parallel",), 40),
        name="ple_embed",
    )(p, w, g.reshape(1, n))


def _ple_gate_epilogue(chunk, r0, epi, outs, *, rows, d):
    ssq_ref, e_ref, h_ref = epi
    (o_ref,) = outs
    sl = slice(r0, r0 + rows)
    rstd = _rstd_from_ssq(ssq_ref[sl, :], d)
    rstd = jnp.concatenate([rstd] * (chunk.shape[1] // 128), axis=1)
    out = h_ref[sl, :] + jax.nn.sigmoid(chunk * rstd) * e_ref[sl, :].astype(F32)
    o_ref[sl, :] = out
    return out[0:1, 0:128]


def _ple_gate(hb, ssq, w, e, h, bm=1024, bn=512, epi_rows=8):
    t, k = hb.shape
    n = w.shape[1]
    block = ((bm, bn), lambda i, j: (i, j))
    return _pipelined_matmul(
        "ple_gate", [hb], [w], [lambda j: (0, j)],
        [ssq, e, h], [((bm, ssq.shape[1]), lambda i, j: (i, 0)), block, block],
        [block], [jax.ShapeDtypeStruct((t, n), F32)],
        functools.partial(_ple_gate_epilogue, rows=epi_rows, d=k),
        nj=n // bn, bm=bm, bn=bn, epi_rows=epi_rows, vmem_mib=48)[0]


def kernel(x, p, norm_mix_g, w_in, gmlp_v_g, gmlp_ws, gmlp_bs, q_norm_g, k_norm_g, na_rpb,
           out_norm_a_g, out_norm_b_g, w_out, norm_ffn_g, w_up, conv_w, conv_b, w_down,
           norm_ple_g, w_ple_gate, w_ple_proj, ple_post_g):
    batch, seq, d_model = x.shape
    t = batch * seq
    depth = w_in.shape[0]
    n_a = gmlp_ws.shape[1]
    n_b = na_rpb.shape[1]
    h = x.reshape(t, d_model)
    for i in range(depth):
        hn = _rmsnorm(h, norm_mix_g[i])
        w = w_in[i].astype(BF16)
        d_a, d_b = n_a * HEAD_DIM, n_b * HEAD_DIM
        zuv = _inproj(hn, w, 0, 2 * d_a, "gelu")
        zqk = _inproj(hn, w, 2 * d_a, 2 * d_b, "norm",
                      _na_qk_gain(q_norm_g[i], k_norm_g[i], n_b))
        zv = _inproj(hn, w, 2 * d_a + 2 * d_b, d_b, "plain")
        mix_a = _gmlp(zuv, gmlp_ws[i], gmlp_bs[i], gmlp_v_g[i], out_norm_a_g[i], n_a)
        mix_b = _na(zqk, zv, _na_toeplitz(na_rpb[i]), out_norm_b_g[i], n_b, batch, seq)
        h, hb, ssq = _outproj(mix_a, mix_b, w_out[i].astype(BF16), h)
        act = _ffn_up(hb, ssq, (norm_ffn_g[i][:, None] * w_up[i]).astype(BF16),
                      conv_w[i], conv_b[i].reshape(1, -1), seq)
        h, hb, ssq = _ffn_down(act, w_down[i].astype(BF16), h)
        e = _ple_embed(p[i].reshape(t, -1), w_ple_proj[i].astype(BF16), ple_post_g[i])
        h = _ple_gate(hb, ssq, (norm_ple_g[i][:, None] * w_ple_gate[i]).astype(BF16), e, h)
    return h.reshape(batch, seq, d_model)
```

```python
import functools

import numpy as np
import jax
import jax.numpy as jnp
from jax import lax
from jax.experimental import pallas as pl
from jax.experimental.pallas import tpu as pltpu

F32 = jnp.float32
BF16 = jnp.bfloat16

EPS = 1e-6
HEAD_DIM = 128
CHUNK = 128
GMLP_GROUP = 8
GRID_W = 64
WIN_ROWS = 8
WIN_COLS = 16
NA_G = 4
NA_WIN = NA_G + WIN_ROWS
MASKED = -1e30
NA_LOG2E = float(np.log2(np.e))
HALO = 16
EPI_ROWS = 8
SSQ_BN = 512
MIB = 1 << 20


def _params(semantics, vmem_mib, flags=None):
    return pltpu.CompilerParams(dimension_semantics=semantics,
                                vmem_limit_bytes=vmem_mib * MIB, flags=flags)


def _rms(x, g):
    ms = jnp.mean(x * x, axis=-1, keepdims=True)
    return x * lax.rsqrt(ms + EPS) * g


def _gelu(x):
    return 0.5 * x * (1.0 + lax.erf(x * np.float32(np.sqrt(0.5))))


def _rmsnorm_kernel(x_ref, g_ref, o_ref):
    o_ref[...] = _rms(x_ref[...], g_ref[...]).astype(o_ref.dtype)


def _rmsnorm(x, g, bm=512):
    t, d = x.shape
    return pl.pallas_call(
        _rmsnorm_kernel,
        grid=(t // bm,),
        in_specs=[pl.BlockSpec((bm, d), lambda i: (i, 0)),
                  pl.BlockSpec((1, d), lambda i: (0, 0))],
        out_specs=pl.BlockSpec((bm, d), lambda i: (i, 0)),
        out_shape=jax.ShapeDtypeStruct((t, d), BF16),
        compiler_params=_params(("parallel",), 40),
        name="rmsnorm",
    )(x, g.reshape(1, d))


def _dependent_zero(x):
    bits = pltpu.bitcast(x, jnp.uint32)
    return pltpu.bitcast((bits >> 16) >> 16, F32)


def _pipelined_matmul_kernel(*refs, n_lhs, n_epi, n_out, epilogue, bm, bn, epi_rows):
    lhs = refs[:n_lhs]
    rhs = refs[n_lhs:2 * n_lhs]
    epi = refs[2 * n_lhs:2 * n_lhs + n_epi]
    outs = refs[2 * n_lhs + n_epi:2 * n_lhs + n_epi + n_out]
    acc_a_ref, acc_b_ref = refs[2 * n_lhs + n_epi + n_out:]
    s = pl.program_id(0)

    @pl.when(s == 0)
    def _():
        acc_b_ref[...] = jnp.zeros(acc_b_ref.shape, acc_b_ref.dtype)

    def step(new_ref, old_ref):
        zero = None
        for r0 in range(0, bm, epi_rows):
            chunk = old_ref[r0:r0 + epi_rows, :]
            if zero is not None:
                chunk = chunk + zero
            last = epilogue(chunk, r0, epi, outs)
            zero = jnp.concatenate([_dependent_zero(last)] * (bn // 128), axis=1)
        acc = jnp.dot(lhs[0][...], rhs[0][...], preferred_element_type=F32)
        for a_ref, w_ref in zip(lhs[1:], rhs[1:]):
            acc = acc + jnp.dot(a_ref[...], w_ref[...], preferred_element_type=F32)
        new_ref[...] = acc

    @pl.when(s % 2 == 0)
    def _():
        step(acc_a_ref, acc_b_ref)

    @pl.when(s % 2 == 1)
    def _():
        step(acc_b_ref, acc_a_ref)


def _pipelined_matmul(name, lhs, rhs, rhs_blocks, epi, epi_specs, out_specs, out_shapes, epilogue,
                      *, nj, bm, bn, epi_rows, vmem_mib):
    ni = lhs[0].shape[0] // bm
    n_blocks = ni * nj

    def cur_ij(s):
        return jnp.minimum(s // nj, ni - 1), s % nj

    def prev_ij(s):
        sp = jnp.maximum(s - 1, 0)
        return sp // nj, sp % nj

    def cur_row(s):
        return cur_ij(s)[0], 0

    in_specs = [pl.BlockSpec((bm, a.shape[1]), cur_row) for a in lhs]
    in_specs += [pl.BlockSpec((a.shape[1], bn),
                              functools.partial(lambda s, blk: blk(cur_ij(s)[1]), blk=blk))
                 for a, blk in zip(lhs, rhs_blocks)]
    in_specs += [pl.BlockSpec(shape, functools.partial(lambda s, fn: fn(*prev_ij(s)), fn=fn))
                 for shape, fn in epi_specs]
    out_block_specs = [pl.BlockSpec(shape, functools.partial(lambda s, fn: fn(*prev_ij(s)), fn=fn))
                       for shape, fn in out_specs]
    return pl.pallas_call(
        functools.partial(_pipelined_matmul_kernel, n_lhs=len(lhs), n_epi=len(epi),
                          n_out=len(out_shapes), epilogue=epilogue, bm=bm, bn=bn,
                          epi_rows=epi_rows),
        grid=(n_blocks + 1,),
        in_specs=in_specs,
        out_specs=out_block_specs,
        out_shape=out_shapes,
        scratch_shapes=[pltpu.VMEM((bm, bn), F32), pltpu.VMEM((bm, bn), F32)],
        compiler_params=_params(("arbitrary",), vmem_mib),
        name=name,
    )(*lhs, *rhs, *epi)


def _heads_epilogue(chunk, r0, epi, outs, *, rows, kind):
    (o_ref,) = outs
    y = None
    for hh in range(chunk.shape[1] // HEAD_DIM):
        cols = slice(hh * HEAD_DIM, (hh + 1) * HEAD_DIM)
        y = chunk[:, cols]
        if kind == "gelu":
            y = _gelu(y)
        elif kind == "norm":
            y = _rms(y, epi[0][:, cols])
        o_ref[hh, r0:r0 + rows, :] = y.astype(o_ref.dtype)
    return y[0:1, :]


def _inproj(hn, w, col0, n_cols, kind, gain=None, bm=1024, bn=1024):
    t = hn.shape[0]
    hpb = bn // HEAD_DIM
    epi_rows = {"norm": 32, "gelu": 8, "plain": 16}[kind]
    epi, epi_specs = [], []
    if kind == "norm":
        epi, epi_specs = [gain], [((1, bn), lambda i, j: (0, j))]
    return _pipelined_matmul(
        "inproj_" + kind, [hn], [w], [lambda j: (0, col0 // bn + j)], epi, epi_specs,
        [((hpb, bm, HEAD_DIM), lambda i, j: (j, i, 0))],
        [jax.ShapeDtypeStruct((n_cols // HEAD_DIM, t, HEAD_DIM), BF16)],
        functools.partial(_heads_epilogue, rows=epi_rows, kind=kind),
        nj=n_cols // bn, bm=bm, bn=bn, epi_rows=epi_rows, vmem_mib=56)[0]


def _gmlp_kernel(u_ref, v_ref, ws_ref, bs_ref, gv_ref, og_ref, o_ref, *, n_groups):
    ws = ws_ref[0]
    bs = bs_ref[0]
    gv = gv_ref[0]
    og = og_ref[0]

    def group_body(gi, carry):
        base = pl.multiple_of(gi * (GMLP_GROUP * CHUNK), GMLP_GROUP * CHUNK)
        for c in range(GMLP_GROUP):
            sl = pl.ds(base + c * CHUNK, CHUNK)
            vn = _rms(v_ref[0, sl, :].astype(F32), gv).astype(BF16)
            mixed = jnp.dot(ws, vn, preferred_element_type=F32) + bs
            a = u_ref[0, sl, :].astype(F32) * mixed
            o_ref[sl, :] = _rms(a, og).astype(o_ref.dtype)
        return carry

    lax.fori_loop(0, n_groups, group_body, 0)


def _gmlp(z, ws, bs, gv, og, n_heads, tb=8192):
    t = z.shape[1]
    return pl.pallas_call(
        functools.partial(_gmlp_kernel, n_groups=tb // (GMLP_GROUP * CHUNK)),
        grid=(n_heads, t // tb),
        in_specs=[pl.BlockSpec((1, tb, HEAD_DIM), lambda h, i: (h, i, 0)),
                  pl.BlockSpec((1, tb, HEAD_DIM), lambda h, i: (n_heads + h, i, 0)),
                  pl.BlockSpec((1, CHUNK, CHUNK), lambda h, i: (h, 0, 0)),
                  pl.BlockSpec((1, CHUNK, 1), lambda h, i: (h, 0, 0)),
                  pl.BlockSpec((1, 1, HEAD_DIM), lambda h, i: (h, 0, 0)),
                  pl.BlockSpec((1, 1, HEAD_DIM), lambda h, i: (h, 0, 0))],
        out_specs=pl.BlockSpec((tb, HEAD_DIM), lambda h, i: (i, h)),
        out_shape=jax.ShapeDtypeStruct((t, n_heads * HEAD_DIM), BF16),
        compiler_params=_params(("parallel", "parallel"), 24),
        name="gmlp",
    )(z, z, ws.astype(BF16), bs.reshape(n_heads, CHUNK, 1),
      gv.reshape(n_heads, 1, HEAD_DIM), og.reshape(n_heads, 1, HEAD_DIM))


def _na_toeplitz(rpb):
    col = np.arange(GRID_W)
    cs = np.clip(col - WIN_COLS // 2, 0, GRID_W - WIN_COLS)
    kc = np.arange(GRID_W)
    valid = (kc[None, :] >= cs[:, None]) & (kc[None, :] < cs[:, None] + WIN_COLS)
    dc = kc[None, :] - col[:, None] + (WIN_COLS - 1)
    onehot = (dc[:, :, None] == np.arange(2 * WIN_COLS - 1)).astype(np.float32)
    t = jnp.einsum('hrd,ckd->hrck', rpb, onehot, precision=lax.Precision.HIGHEST)
    return jnp.where(valid[None, None], t, MASKED)


def _na_group_geometry(rows):
    def geom(g):
        r0 = g * NA_G
        ws = int(np.clip(r0 - WIN_ROWS // 2, 0, rows - NA_WIN))
        rs = [int(np.clip(r0 + i - WIN_ROWS // 2, 0, rows - WIN_ROWS)) for i in range(NA_G)]
        return r0 - ws, tuple(r - ws for r in rs)

    ng = rows // NA_G
    assert ng >= 3 and all(geom(g) == geom(1) for g in range(1, ng - 1))
    return geom(0), geom(1), geom(ng - 1)


def _na_kernel(q_ref, k_ref, v_ref, t_ref, og_ref, o_ref, bias_ref, *, rows):
    log2e = np.float32(NA_LOG2E)
    og = og_ref[0]

    masked_tile = jnp.full((GRID_W, GRID_W), MASKED, F32)
    for cls, (roff, rsoff) in enumerate(_na_group_geometry(rows)):
        for ri in range(NA_G):
            for kp in range(NA_WIN // 2):
                pair = []
                for kr in (2 * kp, 2 * kp + 1):
                    in_window = 0 <= kr - rsoff[ri] < WIN_ROWS
                    dr = kr - roff - ri + (WIN_ROWS - 1)
                    pair.append(t_ref[0, dr] * log2e if in_window else masked_tile)
                bias_ref[cls, ri * GRID_W:(ri + 1) * GRID_W, kp * 128:(kp + 1) * 128] = (
                    jnp.concatenate(pair, axis=1))

    ng = rows // NA_G
    nq = NA_G * GRID_W
    nk = NA_WIN * GRID_W

    def group_body(g, carry):
        r0 = g * NA_G
        ws = jnp.clip(r0 - WIN_ROWS // 2, 0, rows - NA_WIN)
        cls = jnp.minimum(g, 1) + jnp.maximum(g - (ng - 2), 0)
        qsl = pl.ds(pl.multiple_of(r0 * GRID_W, nq), nq)
        ksl = pl.ds(pl.multiple_of(ws * GRID_W, nq), nk)
        s = lax.dot_general(q_ref[0, qsl, :], k_ref[0, ksl, :], (((1,), (1,)), ((), ())),
                            preferred_element_type=F32)
        s = s + bias_ref[cls]
        m = jnp.max(s, axis=-1, keepdims=True)
        e = jnp.exp2(s - m)
        l = jnp.sum(e, axis=-1, keepdims=True)
        o = jnp.dot(e.astype(BF16), v_ref[0, ksl, :], preferred_element_type=F32) / l
        o_ref[qsl, :] = _rms(o, og).astype(o_ref.dtype)
        return carry

    lax.fori_loop(0, ng, group_body, 0, unroll=2)


def _na_qk_gain(qg, kg, n_heads):
    qg = qg * np.float32(HEAD_DIM ** -0.5 * NA_LOG2E)
    return jnp.concatenate([jnp.tile(qg, n_heads), jnp.tile(kg, n_heads)]).reshape(1, -1)


def _na(zqk, zv, toeplitz, og, n_heads, batch, seq):
    rows = seq // GRID_W
    assert rows % NA_G == 0 and (WIN_ROWS // 2) % NA_G == 0
    n_dr = 2 * WIN_ROWS - 1
    return pl.pallas_call(
        functools.partial(_na_kernel, rows=rows),
        grid=(batch, n_heads),
        in_specs=[pl.BlockSpec((1, seq, HEAD_DIM), lambda b, h: (h, b, 0)),
                  pl.BlockSpec((1, seq, HEAD_DIM), lambda b, h: (n_heads + h, b, 0)),
                  pl.BlockSpec((1, seq, HEAD_DIM), lambda b, h: (h, b, 0)),
                  pl.BlockSpec((1, n_dr, GRID_W, GRID_W), lambda b, h: (h, 0, 0, 0)),
                  pl.BlockSpec((1, 1, HEAD_DIM), lambda b, h: (h, 0, 0))],
        out_specs=pl.BlockSpec((seq, HEAD_DIM), lambda b, h: (b, h)),
        out_shape=jax.ShapeDtypeStruct((batch * seq, n_heads * HEAD_DIM), BF16),
        scratch_shapes=[pltpu.VMEM((3, NA_G * GRID_W, NA_WIN * GRID_W), F32)],
        compiler_params=_params(("parallel", "parallel"), 40),
        name="natten",
    )(zqk, zqk, zv, toeplitz, og.reshape(n_heads, 1, HEAD_DIM))


def _residual_epilogue(chunk, r0, epi, outs, *, rows):
    (h_ref,) = epi
    o_ref, ob_ref, ssq_ref = outs
    sl = slice(r0, r0 + rows)
    h = h_ref[sl, :] + chunk
    o_ref[sl, :] = h
    ob_ref[sl, :] = h.astype(ob_ref.dtype)
    ssq = jnp.broadcast_to(jnp.sum(h * h, axis=-1, keepdims=True), (rows, 128))
    ssq_ref[sl, :] = ssq
    return ssq[0:1, :]


def _residual_matmul(name, lhs, rhs, rhs_blocks, h, *, bm, epi_rows, vmem_mib):
    t, n = h.shape
    bn = SSQ_BN
    block = ((bm, bn), lambda i, j: (i, j))
    return _pipelined_matmul(
        name, lhs, rhs, rhs_blocks, [h], [block],
        [block, block, ((bm, 128), lambda i, j: (i, j))],
        [jax.ShapeDtypeStruct((t, n), F32), jax.ShapeDtypeStruct((t, n), BF16),
         jax.ShapeDtypeStruct((t, n // bn * 128), F32)],
        functools.partial(_residual_epilogue, rows=epi_rows),
        nj=n // bn, bm=bm, bn=bn, epi_rows=epi_rows, vmem_mib=vmem_mib)


def _row_rstd_kernel(s_ref, o_ref, *, d):
    tot = s_ref[:, 0:128]
    for c in range(1, s_ref.shape[1] // 128):
        tot = tot + s_ref[:, c * 128:(c + 1) * 128]
    o_ref[...] = lax.rsqrt(tot * np.float32(1.0 / d) + EPS)


def _row_rstd(ssq, d, bm=2048):
    t, sw = ssq.shape
    return pl.pallas_call(
        functools.partial(_row_rstd_kernel, d=d),
        grid=(t // bm,),
        in_specs=[pl.BlockSpec((bm, sw), lambda i: (i, 0))],
        out_specs=pl.BlockSpec((bm, 128), lambda i: (i, 0)),
        out_shape=jax.ShapeDtypeStruct((t, 128), F32),
        compiler_params=_params(("parallel",), 32),
        name="row_rstd",
    )(ssq)


def _outproj(mix_a, mix_b, w, x, bm=1024):
    assert mix_a.shape[1] == mix_b.shape[1]
    return _residual_matmul("outproj", [mix_a, mix_b], [w, w],
                            [lambda j: (0, j), lambda j: (1, j)], x,
                            bm=bm, epi_rows=32, vmem_mib=48)


def _ffn_up_window_start(i, bm, t):
    return pl.multiple_of(jnp.clip(i * bm - HALO, 0, t - (bm + 2 * HALO)), HALO)


def _ffn_up_kernel(x_ref, r_ref, wg_ref, wv_ref, cwg_ref, cwv_ref, cbg_ref, cbv_ref,
                   o_ref, rstd_ref, up_a_ref, up_b_ref, *, bm, bn, nj, ni, seq):
    s = pl.program_id(0)
    win = bm + 2 * HALO
    t = ni * bm
    i = jnp.minimum(s // nj, ni - 1)
    start = _ffn_up_window_start(i, bm, t)
    ip = jnp.maximum(s - 1, 0) // nj
    off_prev = pl.multiple_of(ip * bm - _ffn_up_window_start(ip, bm, t), HALO)

    @pl.when((s % nj == 0) & (s < ni * nj))
    def _():
        tok = start + lax.broadcasted_iota(jnp.int32, (win, 128), 0)
        seq_start = (i * bm) // seq * seq
        same_seq = (tok >= seq_start) & (tok < seq_start + seq)
        rstd_ref[...] = jnp.where(same_seq, r_ref[...], 0.0)

    @pl.when(s == 0)
    def _():
        up_a_ref[...] = jnp.zeros(up_a_ref.shape, up_a_ref.dtype)
        up_b_ref[...] = jnp.zeros(up_b_ref.shape, up_b_ref.dtype)

    def step(new_ref, old_ref):
        cw = jnp.concatenate([cwg_ref[...], cwv_ref[...]], axis=1)
        cb = jnp.concatenate([cbg_ref[...], cbv_ref[...]], axis=1)
        w_cur = cw[1:2, :]
        for r0 in range(0, bm, EPI_ROWS):
            slab = old_ref[pl.ds(pl.multiple_of(off_prev + r0, 8), EPI_ROWS + 16), :]
            prev = pltpu.roll(slab, 1, 0)[8:8 + EPI_ROWS]
            cur = slab[8:8 + EPI_ROWS]
            nxt = pltpu.roll(slab, EPI_ROWS + 15, 0)[8:8 + EPI_ROWS]
            c = cb + (prev * cw[0:1, :] + cur * w_cur + nxt * cw[2:3, :])
            act = _gelu(c[:, :bn]) * c[:, bn:]
            o_ref[r0:r0 + EPI_ROWS, :] = act.astype(o_ref.dtype)
            zero = _dependent_zero(act[0:1, 0:128])
            w_cur = cw[1:2, :] + jnp.concatenate([zero] * (2 * bn // 128), axis=1)

        half = win // 2
        for h0 in (0, half):
            xh = x_ref[h0:h0 + half, :]
            rstd = jnp.concatenate([rstd_ref[h0:h0 + half, :]] * (bn // 128), axis=1)
            new_ref[8 + h0:8 + h0 + half, :bn] = (
                jnp.dot(xh, wg_ref[...], preferred_element_type=F32) * rstd)
            new_ref[8 + h0:8 + h0 + half, bn:] = (
                jnp.dot(xh, wv_ref[...], preferred_element_type=F32) * rstd)

    @pl.when(s % 2 == 0)
    def _():
        step(up_a_ref, up_b_ref)

    @pl.when(s % 2 == 1)
    def _():
        step(up_b_ref, up_a_ref)


def _ffn_up(xn, rstd, w, cw, cb, seq, bm=2048, bn=256):
    t, k = xn.shape
    n = w.shape[1] // 2
    ni, nj = t // bm, n // bn
    win = bm + 2 * HALO
    assert seq % bm == 0 and t >= win and win % (2 * HALO) == 0

    def window(s):
        return _ffn_up_window_start(jnp.minimum(s // nj, ni - 1), bm, t), 0

    def prev_ij(s):
        sp = jnp.maximum(s - 1, 0)
        return sp // nj, sp % nj

    return pl.pallas_call(
        functools.partial(_ffn_up_kernel, bm=bm, bn=bn, nj=nj, ni=ni, seq=seq),
        grid=(ni * nj + 1,),
        in_specs=[pl.BlockSpec((pl.Element(win), pl.Element(k)), window,
                               pipeline_mode=pl.Buffered(1)),
                  pl.BlockSpec((pl.Element(win), pl.Element(128)), window),
                  pl.BlockSpec((k, bn), lambda s: (0, s % nj)),
                  pl.BlockSpec((k, bn), lambda s: (0, nj + s % nj)),
                  pl.BlockSpec((3, bn), lambda s: (0, prev_ij(s)[1])),
                  pl.BlockSpec((3, bn), lambda s: (0, nj + prev_ij(s)[1])),
                  pl.BlockSpec((1, bn), lambda s: (0, prev_ij(s)[1])),
                  pl.BlockSpec((1, bn), lambda s: (0, nj + prev_ij(s)[1]))],
        out_specs=pl.BlockSpec((bm, bn), lambda s: prev_ij(s)),
        out_shape=jax.ShapeDtypeStruct((t, n), BF16),
        scratch_shapes=[pltpu.VMEM((win, 128), F32),
                        pltpu.VMEM((win + 16, 2 * bn), F32),
                        pltpu.VMEM((win + 16, 2 * bn), F32)],
        compiler_params=_params(("arbitrary",), 48),
        name="ffn_up",
    )(xn, rstd, w, w, cw, cw, cb, cb)


def _ffn_down(act, w, h, bm=512):
    return _residual_matmul("ffn_down", [act], [w], [lambda j: (0, j)], h,
                            bm=bm, epi_rows=16, vmem_mib=56)


def _ple_embed_kernel(p_ref, w_ref, g_ref, o_ref):
    e = jnp.dot(p_ref[...].astype(BF16), w_ref[...], preferred_element_type=F32)
    o_ref[...] = _rms(e, g_ref[...]).astype(o_ref.dtype)


def _ple_embed(p, w, g, bm=512):
    t, k = p.shape
    n = w.shape[1]
    return pl.pallas_call(
        _ple_embed_kernel,
        grid=(t // bm,),
        in_specs=[pl.BlockSpec((bm, k), lambda i: (i, 0)),
                  pl.BlockSpec((k, n), lambda i: (0, 0)),
                  pl.BlockSpec((1, n), lambda i: (0, 0))],
        out_specs=pl.BlockSpec((bm, n), lambda i: (i, 0)),
        out_shape=jax.ShapeDtypeStruct((t, n), BF16),
        compiler_params=_params(("parallel",), 40),
        name="ple_embed",
    )(p, w, g.reshape(1, n))


def _ple_gate_epilogue(chunk, r0, epi, outs, *, rows):
    rstd_ref, e_ref, h_ref = epi
    (o_ref,) = outs
    sl = slice(r0, r0 + rows)
    rstd = jnp.concatenate([rstd_ref[sl, :]] * (chunk.shape[1] // 128), axis=1)
    out = h_ref[sl, :] + jax.nn.sigmoid(chunk * rstd) * e_ref[sl, :].astype(F32)
    o_ref[sl, :] = out
    return out[0:1, 0:128]


def _ple_gate(hb, rstd, w, e, h, bm=1024, bn=512, epi_rows=8):
    t = hb.shape[0]
    n = w.shape[1]
    block = ((bm, bn), lambda i, j: (i, j))
    return _pipelined_matmul(
        "ple_gate", [hb], [w], [lambda j: (0, j)],
        [rstd, e, h], [((bm, 128), lambda i, j: (i, 0)), block, block],
        [block], [jax.ShapeDtypeStruct((t, n), F32)],
        functools.partial(_ple_gate_epilogue, rows=epi_rows),
        nj=n // bn, bm=bm, bn=bn, epi_rows=epi_rows, vmem_mib=48)[0]


def kernel(x, p, norm_mix_g, w_in, gmlp_v_g, gmlp_ws, gmlp_bs, q_norm_g, k_norm_g, na_rpb,
           out_norm_a_g, out_norm_b_g, w_out, norm_ffn_g, w_up, conv_w, conv_b, w_down,
           norm_ple_g, w_ple_gate, w_ple_proj, ple_post_g):
    batch, seq, d_model = x.shape
    t = batch * seq
    depth = w_in.shape[0]
    n_a = gmlp_ws.shape[1]
    n_b = na_rpb.shape[1]
    h = x.reshape(t, d_model)
    for i in range(depth):
        hn = _rmsnorm(h, norm_mix_g[i])
        w = w_in[i].astype(BF16)
        d_a, d_b = n_a * HEAD_DIM, n_b * HEAD_DIM
        zuv = _inproj(hn, w, 0, 2 * d_a, "gelu")
        zqk = _inproj(hn, w, 2 * d_a, 2 * d_b, "norm",
                      _na_qk_gain(q_norm_g[i], k_norm_g[i], n_b))
        zv = _inproj(hn, w, 2 * d_a + 2 * d_b, d_b, "plain")
        mix_a = _gmlp(zuv, gmlp_ws[i], gmlp_bs[i], gmlp_v_g[i], out_norm_a_g[i], n_a)
        mix_b = _na(zqk, zv, _na_toeplitz(na_rpb[i]), out_norm_b_g[i], n_b, batch, seq)
        h, hb, ssq = _outproj(mix_a, mix_b, w_out[i].astype(BF16), h)
        act = _ffn_up(hb, _row_rstd(ssq, d_model),
                      (norm_ffn_g[i][:, None] * w_up[i]).astype(BF16),
                      conv_w[i], conv_b[i].reshape(1, -1), seq)
        h, hb, ssq = _ffn_down(act, w_down[i].astype(BF16), h)
        e = _ple_embed(p[i].reshape(t, -1), w_ple_proj[i].astype(BF16), ple_post_g[i])
        h = _ple_gate(hb, _row_rstd(ssq, d_model),
                      (norm_ple_g[i][:, None] * w_ple_gate[i]).astype(BF16), e, h)
    return h.reshape(batch, seq, d_model)
```

```python
import functools

import numpy as np
import jax
import jax.numpy as jnp
from jax import lax
from jax.experimental import pallas as pl
from jax.experimental.pallas import tpu as pltpu

F32 = jnp.float32
BF16 = jnp.bfloat16

EPS = 1e-6
HEAD_DIM = 128
CHUNK = 128
GMLP_GROUP = 8
GRID_W = 64
WIN_ROWS = 8
WIN_COLS = 16
NA_G = 4
NA_WIN = NA_G + WIN_ROWS
MASKED = -1e30
NA_LOG2E = float(np.log2(np.e))
HALO = 16
EPI_ROWS = 8
SSQ_BN = 512
MIB = 1 << 20


def _params(semantics, vmem_mib, flags=None):
    return pltpu.CompilerParams(dimension_semantics=semantics,
                                vmem_limit_bytes=vmem_mib * MIB, flags=flags)


def _rms(x, g):
    ms = jnp.mean(x * x, axis=-1, keepdims=True)
    return x * lax.rsqrt(ms + EPS) * g


def _gelu(x):
    return 0.5 * x * (1.0 + lax.erf(x * np.float32(np.sqrt(0.5))))


def _rmsnorm_kernel(x_ref, g_ref, o_ref):
    o_ref[...] = _rms(x_ref[...], g_ref[...]).astype(o_ref.dtype)


def _rmsnorm(x, g, bm=512):
    t, d = x.shape
    return pl.pallas_call(
        _rmsnorm_kernel,
        grid=(t // bm,),
        in_specs=[pl.BlockSpec((bm, d), lambda i: (i, 0)),
                  pl.BlockSpec((1, d), lambda i: (0, 0))],
        out_specs=pl.BlockSpec((bm, d), lambda i: (i, 0)),
        out_shape=jax.ShapeDtypeStruct((t, d), BF16),
        compiler_params=_params(("parallel",), 40),
        name="rmsnorm",
    )(x, g.reshape(1, d))


def _dependent_zero(x):
    bits = pltpu.bitcast(x, jnp.uint32)
    return pltpu.bitcast((bits >> 16) >> 16, F32)


SIDE_COLS = 256


def _convert_weight_block(x_ref, g_ref, o_ref):
    g = None if g_ref is None else jnp.concatenate([g_ref[...]] * (SIDE_COLS // 128), axis=1)
    for c in range(0, x_ref.shape[1], SIDE_COLS):
        blk = x_ref[:, c:c + SIDE_COLS]
        if g is not None:
            blk = blk * g
        o_ref[:, c:c + SIDE_COLS] = blk.astype(o_ref.dtype)


def _side_specs(side, n_steps):
    x, gain, rows, cols, cj = side
    n_side = x.shape[0] // rows
    assert n_side <= n_steps and x.shape[0] % rows == 0 and cols % SIDE_COLS == 0

    def blk(s):
        return jnp.minimum(s, n_side - 1), cj

    in_specs = [pl.BlockSpec((rows, cols), blk)]
    operands = [x]
    if gain is not None:
        in_specs.append(pl.BlockSpec((rows, 128), lambda s: (blk(s)[0], 0)))
        operands.append(gain)
    out_spec = pl.BlockSpec((rows, cols), lambda s: (blk(s)[0], 0))
    out_shape = jax.ShapeDtypeStruct((x.shape[0], cols), BF16)
    return in_specs, operands, out_spec, out_shape


def _pipelined_matmul_kernel(*refs, n_lhs, n_epi, n_out, n_side, epilogue, bm, bn, epi_rows):
    lhs = refs[:n_lhs]
    rhs = refs[n_lhs:2 * n_lhs]
    epi = refs[2 * n_lhs:2 * n_lhs + n_epi]
    side_in = refs[2 * n_lhs + n_epi:2 * n_lhs + n_epi + n_side]
    n_in = 2 * n_lhs + n_epi + n_side
    outs = refs[n_in:n_in + n_out]
    side_out = refs[n_in + n_out:-2]
    acc_a_ref, acc_b_ref = refs[-2:]
    s = pl.program_id(0)

    @pl.when(s == 0)
    def _():
        acc_b_ref[...] = jnp.zeros(acc_b_ref.shape, acc_b_ref.dtype)

    def step(new_ref, old_ref):
        zero = None
        for r0 in range(0, bm, epi_rows):
            chunk = old_ref[r0:r0 + epi_rows, :]
            if zero is not None:
                chunk = chunk + zero
            last = epilogue(chunk, r0, epi, outs)
            zero = jnp.concatenate([_dependent_zero(last)] * (bn // 128), axis=1)
        if side_in:
            _convert_weight_block(side_in[0], side_in[1] if n_side == 2 else None, side_out[0])
        acc = jnp.dot(lhs[0][...], rhs[0][...], preferred_element_type=F32)
        for a_ref, w_ref in zip(lhs[1:], rhs[1:]):
            acc = acc + jnp.dot(a_ref[...], w_ref[...], preferred_element_type=F32)
        new_ref[...] = acc

    @pl.when(s % 2 == 0)
    def _():
        step(acc_a_ref, acc_b_ref)

    @pl.when(s % 2 == 1)
    def _():
        step(acc_b_ref, acc_a_ref)


def _pipelined_matmul(name, lhs, rhs, rhs_blocks, epi, epi_specs, out_specs, out_shapes, epilogue,
                      *, nj, bm, bn, epi_rows, vmem_mib, side=None):
    ni = lhs[0].shape[0] // bm
    n_blocks = ni * nj

    def cur_ij(s):
        return jnp.minimum(s // nj, ni - 1), s % nj

    def prev_ij(s):
        sp = jnp.maximum(s - 1, 0)
        return sp // nj, sp % nj

    def cur_row(s):
        return cur_ij(s)[0], 0

    in_specs = [pl.BlockSpec((bm, a.shape[1]), cur_row) for a in lhs]
    in_specs += [pl.BlockSpec((a.shape[1], bn),
                              functools.partial(lambda s, blk: blk(cur_ij(s)[1]), blk=blk))
                 for a, blk in zip(lhs, rhs_blocks)]
    in_specs += [pl.BlockSpec(shape, functools.partial(lambda s, fn: fn(*prev_ij(s)), fn=fn))
                 for shape, fn in epi_specs]
    out_block_specs = [pl.BlockSpec(shape, functools.partial(lambda s, fn: fn(*prev_ij(s)), fn=fn))
                       for shape, fn in out_specs]
    out_shapes = list(out_shapes)
    side_operands = []
    if side is not None:
        side_in_specs, side_operands, side_out_spec, side_out_shape = _side_specs(side, n_blocks + 1)
        in_specs += side_in_specs
        out_block_specs.append(side_out_spec)
        out_shapes.append(side_out_shape)
    return pl.pallas_call(
        functools.partial(_pipelined_matmul_kernel, n_lhs=len(lhs), n_epi=len(epi),
                          n_out=len(out_specs), n_side=len(side_operands), epilogue=epilogue,
                          bm=bm, bn=bn, epi_rows=epi_rows),
        grid=(n_blocks + 1,),
        in_specs=in_specs,
        out_specs=out_block_specs,
        out_shape=out_shapes,
        scratch_shapes=[pltpu.VMEM((bm, bn), F32), pltpu.VMEM((bm, bn), F32)],
        compiler_params=_params(("arbitrary",), vmem_mib),
        name=name,
    )(*lhs, *rhs, *epi, *side_operands)


def _heads_epilogue(chunk, r0, epi, outs, *, rows, kind):
    (o_ref,) = outs
    y = None
    for hh in range(chunk.shape[1] // HEAD_DIM):
        cols = slice(hh * HEAD_DIM, (hh + 1) * HEAD_DIM)
        y = chunk[:, cols]
        if kind == "gelu":
            y = _gelu(y)
        elif kind == "norm":
            y = _rms(y, epi[0][:, cols])
        o_ref[hh, r0:r0 + rows, :] = y.astype(o_ref.dtype)
    return y[0:1, :]


def _inproj(hn, w, col0, n_cols, kind, gain=None, bm=1024, bn=1024, side=None):
    t = hn.shape[0]
    hpb = bn // HEAD_DIM
    epi_rows = {"norm": 32, "gelu": 8, "plain": 16}[kind]
    epi, epi_specs = [], []
    if kind == "norm":
        epi, epi_specs = [gain], [((1, bn), lambda i, j: (0, j))]
    return _pipelined_matmul(
        "inproj_" + kind, [hn], [w], [lambda j: (0, col0 // bn + j)], epi, epi_specs,
        [((hpb, bm, HEAD_DIM), lambda i, j: (j, i, 0))],
        [jax.ShapeDtypeStruct((n_cols // HEAD_DIM, t, HEAD_DIM), BF16)],
        functools.partial(_heads_epilogue, rows=epi_rows, kind=kind),
        nj=n_cols // bn, bm=bm, bn=bn, epi_rows=epi_rows, vmem_mib=62, side=side)


def _gmlp_kernel(u_ref, v_ref, ws_ref, bs_ref, gv_ref, og_ref, o_ref, *, n_groups):
    ws = ws_ref[0]
    bs = bs_ref[0]
    gv = gv_ref[0]
    og = og_ref[0]

    def group_body(gi, carry):
        base = pl.multiple_of(gi * (GMLP_GROUP * CHUNK), GMLP_GROUP * CHUNK)
        for c in range(GMLP_GROUP):
            sl = pl.ds(base + c * CHUNK, CHUNK)
            vn = _rms(v_ref[0, sl, :].astype(F32), gv).astype(BF16)
            mixed = jnp.dot(ws, vn, preferred_element_type=F32) + bs
            a = u_ref[0, sl, :].astype(F32) * mixed
            o_ref[sl, :] = _rms(a, og).astype(o_ref.dtype)
        return carry

    lax.fori_loop(0, n_groups, group_body, 0)


def _gmlp(z, ws, bs, gv, og, n_heads, tb=8192):
    t = z.shape[1]
    return pl.pallas_call(
        functools.partial(_gmlp_kernel, n_groups=tb // (GMLP_GROUP * CHUNK)),
        grid=(n_heads, t // tb),
        in_specs=[pl.BlockSpec((1, tb, HEAD_DIM), lambda h, i: (h, i, 0)),
                  pl.BlockSpec((1, tb, HEAD_DIM), lambda h, i: (n_heads + h, i, 0)),
                  pl.BlockSpec((1, CHUNK, CHUNK), lambda h, i: (h, 0, 0)),
                  pl.BlockSpec((1, CHUNK, 1), lambda h, i: (h, 0, 0)),
                  pl.BlockSpec((1, 1, HEAD_DIM), lambda h, i: (h, 0, 0)),
                  pl.BlockSpec((1, 1, HEAD_DIM), lambda h, i: (h, 0, 0))],
        out_specs=pl.BlockSpec((tb, HEAD_DIM), lambda h, i: (i, h)),
        out_shape=jax.ShapeDtypeStruct((t, n_heads * HEAD_DIM), BF16),
        compiler_params=_params(("parallel", "parallel"), 24),
        name="gmlp",
    )(z, z, ws.astype(BF16), bs.reshape(n_heads, CHUNK, 1),
      gv.reshape(n_heads, 1, HEAD_DIM), og.reshape(n_heads, 1, HEAD_DIM))


def _na_toeplitz(rpb):
    col = np.arange(GRID_W)
    cs = np.clip(col - WIN_COLS // 2, 0, GRID_W - WIN_COLS)
    kc = np.arange(GRID_W)
    valid = (kc[None, :] >= cs[:, None]) & (kc[None, :] < cs[:, None] + WIN_COLS)
    dc = kc[None, :] - col[:, None] + (WIN_COLS - 1)
    onehot = (dc[:, :, None] == np.arange(2 * WIN_COLS - 1)).astype(np.float32)
    t = jnp.einsum('hrd,ckd->hrck', rpb, onehot, precision=lax.Precision.HIGHEST)
    return jnp.where(valid[None, None], t, MASKED)


def _na_group_geometry(rows):
    def geom(g):
        r0 = g * NA_G
        ws = int(np.clip(r0 - WIN_ROWS // 2, 0, rows - NA_WIN))
        rs = [int(np.clip(r0 + i - WIN_ROWS // 2, 0, rows - WIN_ROWS)) for i in range(NA_G)]
        return r0 - ws, tuple(r - ws for r in rs)

    ng = rows // NA_G
    assert ng >= 3 and all(geom(g) == geom(1) for g in range(1, ng - 1))
    return geom(0), geom(1), geom(ng - 1)


def _na_kernel(q_ref, k_ref, v_ref, t_ref, og_ref, o_ref, bias_ref, *, rows):
    log2e = np.float32(NA_LOG2E)
    og = og_ref[0]

    masked_tile = jnp.full((GRID_W, GRID_W), MASKED, F32)
    for cls, (roff, rsoff) in enumerate(_na_group_geometry(rows)):
        for ri in range(NA_G):
            for kp in range(NA_WIN // 2):
                pair = []
                for kr in (2 * kp, 2 * kp + 1):
                    in_window = 0 <= kr - rsoff[ri] < WIN_ROWS
                    dr = kr - roff - ri + (WIN_ROWS - 1)
                    pair.append(t_ref[0, dr] * log2e if in_window else masked_tile)
                bias_ref[cls, ri * GRID_W:(ri + 1) * GRID_W, kp * 128:(kp + 1) * 128] = (
                    jnp.concatenate(pair, axis=1))

    ng = rows // NA_G
    nq = NA_G * GRID_W
    nk = NA_WIN * GRID_W

    def group_body(g, carry):
        r0 = g * NA_G
        ws = jnp.clip(r0 - WIN_ROWS // 2, 0, rows - NA_WIN)
        cls = jnp.minimum(g, 1) + jnp.maximum(g - (ng - 2), 0)
        qsl = pl.ds(pl.multiple_of(r0 * GRID_W, nq), nq)
        ksl = pl.ds(pl.multiple_of(ws * GRID_W, nq), nk)
        s = lax.dot_general(q_ref[0, qsl, :], k_ref[0, ksl, :], (((1,), (1,)), ((), ())),
                            preferred_element_type=F32)
        s = s + bias_ref[cls]
        m = jnp.max(s, axis=-1, keepdims=True)
        e = jnp.exp2(s - m)
        l = jnp.sum(e, axis=-1, keepdims=True)
        o = jnp.dot(e.astype(BF16), v_ref[0, ksl, :], preferred_element_type=F32) / l
        o_ref[qsl, :] = _rms(o, og).astype(o_ref.dtype)
        return carry

    lax.fori_loop(0, ng, group_body, 0, unroll=4)


def _na_qk_gain(qg, kg, n_heads):
    qg = qg * np.float32(HEAD_DIM ** -0.5 * NA_LOG2E)
    return jnp.concatenate([jnp.tile(qg, n_heads), jnp.tile(kg, n_heads)]).reshape(1, -1)


def _na(zqk, zv, toeplitz, og, n_heads, batch, seq):
    rows = seq // GRID_W
    assert rows % NA_G == 0 and (WIN_ROWS // 2) % NA_G == 0
    n_dr = 2 * WIN_ROWS - 1
    return pl.pallas_call(
        functools.partial(_na_kernel, rows=rows),
        grid=(batch, n_heads),
        in_specs=[pl.BlockSpec((1, seq, HEAD_DIM), lambda b, h: (h, b, 0)),
                  pl.BlockSpec((1, seq, HEAD_DIM), lambda b, h: (n_heads + h, b, 0)),
                  pl.BlockSpec((1, seq, HEAD_DIM), lambda b, h: (h, b, 0)),
                  pl.BlockSpec((1, n_dr, GRID_W, GRID_W), lambda b, h: (h, 0, 0, 0)),
                  pl.BlockSpec((1, 1, HEAD_DIM), lambda b, h: (h, 0, 0))],
        out_specs=pl.BlockSpec((seq, HEAD_DIM), lambda b, h: (b, h)),
        out_shape=jax.ShapeDtypeStruct((batch * seq, n_heads * HEAD_DIM), BF16),
        scratch_shapes=[pltpu.VMEM((3, NA_G * GRID_W, NA_WIN * GRID_W), F32)],
        compiler_params=_params(("parallel", "parallel"), 40),
        name="natten",
    )(zqk, zqk, zv, toeplitz, og.reshape(n_heads, 1, HEAD_DIM))


def _residual_epilogue(chunk, r0, epi, outs, *, rows):
    (h_ref,) = epi
    o_ref, ob_ref, ssq_ref = outs
    sl = slice(r0, r0 + rows)
    h = h_ref[sl, :] + chunk
    o_ref[sl, :] = h
    ob_ref[sl, :] = h.astype(ob_ref.dtype)
    ssq = jnp.broadcast_to(jnp.sum(h * h, axis=-1, keepdims=True), (rows, 128))
    ssq_ref[sl, :] = ssq
    return ssq[0:1, :]


def _residual_matmul(name, lhs, rhs, rhs_blocks, h, *, bm, epi_rows, vmem_mib, side=None):
    t, n = h.shape
    bn = SSQ_BN
    block = ((bm, bn), lambda i, j: (i, j))
    return _pipelined_matmul(
        name, lhs, rhs, rhs_blocks, [h], [block],
        [block, block, ((bm, 128), lambda i, j: (i, j))],
        [jax.ShapeDtypeStruct((t, n), F32), jax.ShapeDtypeStruct((t, n), BF16),
         jax.ShapeDtypeStruct((t, n // bn * 128), F32)],
        functools.partial(_residual_epilogue, rows=epi_rows),
        nj=n // bn, bm=bm, bn=bn, epi_rows=epi_rows, vmem_mib=vmem_mib, side=side)


def _row_rstd_kernel(s_ref, o_ref, *, d):
    tot = s_ref[:, 0:128]
    for c in range(1, s_ref.shape[1] // 128):
        tot = tot + s_ref[:, c * 128:(c + 1) * 128]
    o_ref[...] = lax.rsqrt(tot * np.float32(1.0 / d) + EPS)


def _row_rstd(ssq, d, bm=2048):
    t, sw = ssq.shape
    return pl.pallas_call(
        functools.partial(_row_rstd_kernel, d=d),
        grid=(t // bm,),
        in_specs=[pl.BlockSpec((bm, sw), lambda i: (i, 0))],
        out_specs=pl.BlockSpec((bm, 128), lambda i: (i, 0)),
        out_shape=jax.ShapeDtypeStruct((t, 128), F32),
        compiler_params=_params(("parallel",), 32),
        name="row_rstd",
    )(ssq)


def _outproj(mix_a, mix_b, w, x, bm=1024):
    assert mix_a.shape[1] == mix_b.shape[1]
    return _residual_matmul("outproj", [mix_a, mix_b], [w, w],
                            [lambda j: (0, j), lambda j: (1, j)], x,
                            bm=bm, epi_rows=32, vmem_mib=48)


def _ffn_up_window_start(i, bm, t):
    return pl.multiple_of(jnp.clip(i * bm - HALO, 0, t - (bm + 2 * HALO)), HALO)


def _ffn_up_kernel(x_ref, r_ref, wg_ref, wv_ref, cwg_ref, cwv_ref, cbg_ref, cbv_ref, side_ref,
                   o_ref, side_o_ref, rstd_ref, up_a_ref, up_b_ref, *, bm, bn, nj, ni, seq):
    s = pl.program_id(0)
    win = bm + 2 * HALO
    t = ni * bm
    i = jnp.minimum(s // nj, ni - 1)
    start = _ffn_up_window_start(i, bm, t)
    ip = jnp.maximum(s - 1, 0) // nj
    off_prev = pl.multiple_of(ip * bm - _ffn_up_window_start(ip, bm, t), HALO)

    @pl.when((s % nj == 0) & (s < ni * nj))
    def _():
        tok = start + lax.broadcasted_iota(jnp.int32, (win, 128), 0)
        seq_start = (i * bm) // seq * seq
        same_seq = (tok >= seq_start) & (tok < seq_start + seq)
        rstd_ref[...] = jnp.where(same_seq, r_ref[...], 0.0)

    @pl.when(s == 0)
    def _():
        up_a_ref[...] = jnp.zeros(up_a_ref.shape, up_a_ref.dtype)
        up_b_ref[...] = jnp.zeros(up_b_ref.shape, up_b_ref.dtype)

    def step(new_ref, old_ref):
        cw = jnp.concatenate([cwg_ref[...], cwv_ref[...]], axis=1)
        cb = jnp.concatenate([cbg_ref[...], cbv_ref[...]], axis=1)
        w_cur = cw[1:2, :]
        for r0 in range(0, bm, EPI_ROWS):
            slab = old_ref[pl.ds(pl.multiple_of(off_prev + r0, 8), EPI_ROWS + 16), :]
            prev = pltpu.roll(slab, 1, 0)[8:8 + EPI_ROWS]
            cur = slab[8:8 + EPI_ROWS]
            nxt = pltpu.roll(slab, EPI_ROWS + 15, 0)[8:8 + EPI_ROWS]
            c = cb + (prev * cw[0:1, :] + cur * w_cur + nxt * cw[2:3, :])
            act = _gelu(c[:, :bn]) * c[:, bn:]
            o_ref[r0:r0 + EPI_ROWS, :] = act.astype(o_ref.dtype)
            zero = _dependent_zero(act[0:1, 0:128])
            w_cur = cw[1:2, :] + jnp.concatenate([zero] * (2 * bn // 128), axis=1)

        _convert_weight_block(side_ref, None, side_o_ref)

        half = win // 2
        for h0 in (0, half):
            xh = x_ref[h0:h0 + half, :]
            rstd = jnp.concatenate([rstd_ref[h0:h0 + half, :]] * (bn // 128), axis=1)
            new_ref[8 + h0:8 + h0 + half, :bn] = (
                jnp.dot(xh, wg_ref[...], preferred_element_type=F32) * rstd)
            new_ref[8 + h0:8 + h0 + half, bn:] = (
                jnp.dot(xh, wv_ref[...], preferred_element_type=F32) * rstd)

    @pl.when(s % 2 == 0)
    def _():
        step(up_a_ref, up_b_ref)

    @pl.when(s % 2 == 1)
    def _():
        step(up_b_ref, up_a_ref)


def _ffn_up(xn, rstd, wg, wv, cw, cb, seq, side, bm=2048, bn=256):
    t, k = xn.shape
    n = wg.shape[1]
    ni, nj = t // bm, n // bn
    win = bm + 2 * HALO
    assert seq % bm == 0 and t >= win and win % (2 * HALO) == 0
    side_in_specs, side_operands, side_out_spec, side_out_shape = _side_specs(side, ni * nj + 1)
    assert len(side_operands) == 1

    def window(s):
        return _ffn_up_window_start(jnp.minimum(s // nj, ni - 1), bm, t), 0

    def prev_ij(s):
        sp = jnp.maximum(s - 1, 0)
        return sp // nj, sp % nj

    return pl.pallas_call(
        functools.partial(_ffn_up_kernel, bm=bm, bn=bn, nj=nj, ni=ni, seq=seq),
        grid=(ni * nj + 1,),
        in_specs=[pl.BlockSpec((pl.Element(win), pl.Element(k)), window,
                               pipeline_mode=pl.Buffered(1)),
                  pl.BlockSpec((pl.Element(win), pl.Element(128)), window),
                  pl.BlockSpec((k, bn), lambda s: (0, s % nj)),
                  pl.BlockSpec((k, bn), lambda s: (0, s % nj)),
                  pl.BlockSpec((3, bn), lambda s: (0, prev_ij(s)[1])),
                  pl.BlockSpec((3, bn), lambda s: (0, nj + prev_ij(s)[1])),
                  pl.BlockSpec((1, bn), lambda s: (0, prev_ij(s)[1])),
                  pl.BlockSpec((1, bn), lambda s: (0, nj + prev_ij(s)[1]))] + side_in_specs,
        out_specs=[pl.BlockSpec((bm, bn), lambda s: prev_ij(s)), side_out_spec],
        out_shape=[jax.ShapeDtypeStruct((t, n), BF16), side_out_shape],
        scratch_shapes=[pltpu.VMEM((win, 128), F32),
                        pltpu.VMEM((win + 16, 2 * bn), F32),
                        pltpu.VMEM((win + 16, 2 * bn), F32)],
        compiler_params=_params(("arbitrary",), 48),
        name="ffn_up",
    )(xn, rstd, wg, wv, cw, cw, cb, cb, *side_operands)


def _ffn_down(act, w, h, bm=512, side=None):
    return _residual_matmul("ffn_down", [act], [w], [lambda j: (0, j)], h,
                            bm=bm, epi_rows=16, vmem_mib=56, side=side)


def _ple_embed_kernel(p_ref, w_ref, g_ref, o_ref):
    e = jnp.dot(p_ref[...].astype(BF16), w_ref[...], preferred_element_type=F32)
    o_ref[...] = _rms(e, g_ref[...]).astype(o_ref.dtype)


def _ple_embed(p, w, g, bm=512):
    t, k = p.shape
    n = w.shape[1]
    return pl.pallas_call(
        _ple_embed_kernel,
        grid=(t // bm,),
        in_specs=[pl.BlockSpec((bm, k), lambda i: (i, 0)),
                  pl.BlockSpec((k, n), lambda i: (0, 0)),
                  pl.BlockSpec((1, n), lambda i: (0, 0))],
        out_specs=pl.BlockSpec((bm, n), lambda i: (i, 0)),
        out_shape=jax.ShapeDtypeStruct((t, n), BF16),
        compiler_params=_params(("parallel",), 40),
        name="ple_embed",
    )(p, w, g.reshape(1, n))


def _ple_gate_epilogue(chunk, r0, epi, outs, *, rows):
    rstd_ref, e_ref, h_ref = epi
    (o_ref,) = outs
    sl = slice(r0, r0 + rows)
    rstd = jnp.concatenate([rstd_ref[sl, :]] * (chunk.shape[1] // 128), axis=1)
    out = h_ref[sl, :] + jax.nn.sigmoid(chunk * rstd) * e_ref[sl, :].astype(F32)
    o_ref[sl, :] = out
    return out[0:1, 0:128]


def _ple_gate(hb, rstd, w, e, h, bm=1024, bn=512, epi_rows=8):
    t = hb.shape[0]
    n = w.shape[1]
    block = ((bm, bn), lambda i, j: (i, j))
    return _pipelined_matmul(
        "ple_gate", [hb], [w], [lambda j: (0, j)],
        [rstd, e, h], [((bm, 128), lambda i, j: (i, 0)), block, block],
        [block], [jax.ShapeDtypeStruct((t, n), F32)],
        functools.partial(_ple_gate_epilogue, rows=epi_rows),
        nj=n // bn, bm=bm, bn=bn, epi_rows=epi_rows, vmem_mib=48)[0]


def kernel(x, p, norm_mix_g, w_in, gmlp_v_g, gmlp_ws, gmlp_bs, q_norm_g, k_norm_g, na_rpb,
           out_norm_a_g, out_norm_b_g, w_out, norm_ffn_g, w_up, conv_w, conv_b, w_down,
           norm_ple_g, w_ple_gate, w_ple_proj, ple_post_g):
    batch, seq, d_model = x.shape
    t = batch * seq
    depth = w_in.shape[0]
    n_a = gmlp_ws.shape[1]
    n_b = na_rpb.shape[1]
    h = x.reshape(t, d_model)
    for i in range(depth):
        hn = _rmsnorm(h, norm_mix_g[i])
        w = w_in[i].astype(BF16)
        d_a, d_b = n_a * HEAD_DIM, n_b * HEAD_DIM
        d_ff = w_down.shape[1]
        g_ffn = jnp.broadcast_to(norm_ffn_g[i][:, None], (d_model, 128))
        g_ple = jnp.broadcast_to(norm_ple_g[i][:, None], (d_model, 128))
        zuv, w_gate = _inproj(hn, w, 0, 2 * d_a, "gelu",
                              side=(w_up[i], g_ffn, 64, d_ff, 0))
        zqk, w_val = _inproj(hn, w, 2 * d_a, 2 * d_b, "norm",
                             _na_qk_gain(q_norm_g[i], k_norm_g[i], n_b),
                             side=(w_up[i], g_ffn, 64, d_ff, 1))
        zv, w_o = _inproj(hn, w, 2 * d_a + 2 * d_b, d_b, "plain",
                          side=(w_out[i], None, 128, d_model, 0))
        mix_a = _gmlp(zuv, gmlp_ws[i], gmlp_bs[i], gmlp_v_g[i], out_norm_a_g[i], n_a)
        mix_b = _na(zqk, zv, _na_toeplitz(na_rpb[i]), out_norm_b_g[i], n_b, batch, seq)
        h, hb, ssq = _outproj(mix_a, mix_b, w_o, h)
        act, w_dn = _ffn_up(hb, _row_rstd(ssq, d_model), w_gate, w_val,
                            conv_w[i], conv_b[i].reshape(1, -1), seq,
                            side=(w_down[i], None, 32, d_model, 0))
        h, hb, ssq, w_pg = _ffn_down(act, w_dn, h, side=(w_ple_gate[i], g_ple, 16, d_model, 0))
        e = _ple_embed(p[i].reshape(t, -1), w_ple_proj[i].astype(BF16), ple_post_g[i])
        h = _ple_gate(hb, _row_rstd(ssq, d_model), w_pg, e, h)
    return h.reshape(batch, seq, d_model)
```

```python
import functools

import numpy as np
import jax
import jax.numpy as jnp
from jax import lax
from jax.experimental import pallas as pl
from jax.experimental.pallas import tpu as pltpu

F32 = jnp.float32
BF16 = jnp.bfloat16

EPS = 1e-6
HEAD_DIM = 128
CHUNK = 128
GMLP_GROUP = 8
GRID_W = 64
WIN_ROWS = 8
WIN_COLS = 16
NA_G = 4
NA_WIN = NA_G + WIN_ROWS
NA_ALIGN = WIN_ROWS // 2
MASKED = -1e30
NA_LOG2E = float(np.log2(np.e))
HALO = 16
EPI_ROWS = 8
SSQ_BN = 512
MIB = 1 << 20


def _params(semantics, vmem_mib, flags=None):
    return pltpu.CompilerParams(dimension_semantics=semantics,
                                vmem_limit_bytes=vmem_mib * MIB, flags=flags)


def _rms(x, g):
    ms = jnp.mean(x * x, axis=-1, keepdims=True)
    return x * lax.rsqrt(ms + EPS) * g


def _gelu(x):
    return 0.5 * x * (1.0 + lax.erf(x * np.float32(np.sqrt(0.5))))


def _rmsnorm_kernel(x_ref, g_ref, o_ref):
    o_ref[...] = _rms(x_ref[...], g_ref[...]).astype(o_ref.dtype)


def _rmsnorm(x, g, bm=512):
    t, d = x.shape
    return pl.pallas_call(
        _rmsnorm_kernel,
        grid=(t // bm,),
        in_specs=[pl.BlockSpec((bm, d), lambda i: (i, 0)),
                  pl.BlockSpec((1, d), lambda i: (0, 0))],
        out_specs=pl.BlockSpec((bm, d), lambda i: (i, 0)),
        out_shape=jax.ShapeDtypeStruct((t, d), BF16),
        compiler_params=_params(("parallel",), 40),
        name="rmsnorm",
    )(x, g.reshape(1, d))


def _dependent_zero(x):
    bits = pltpu.bitcast(x, jnp.uint32)
    return pltpu.bitcast((bits >> 16) >> 16, F32)


SIDE_COLS = 256


def _convert_weight_block(x_ref, g_ref, o_ref):
    g = None if g_ref is None else jnp.concatenate([g_ref[...]] * (SIDE_COLS // 128), axis=1)
    for c in range(0, x_ref.shape[1], SIDE_COLS):
        blk = x_ref[:, c:c + SIDE_COLS]
        if g is not None:
            blk = blk * g
        o_ref[:, c:c + SIDE_COLS] = blk.astype(o_ref.dtype)


def _side_specs(side, n_steps):
    x, gain, rows, cols, cj = side
    n_side = x.shape[0] // rows
    assert n_side <= n_steps and x.shape[0] % rows == 0 and cols % SIDE_COLS == 0

    def blk(s):
        return jnp.minimum(s, n_side - 1), cj

    in_specs = [pl.BlockSpec((rows, cols), blk)]
    operands = [x]
    if gain is not None:
        in_specs.append(pl.BlockSpec((rows, 128), lambda s: (blk(s)[0], 0)))
        operands.append(gain)
    out_spec = pl.BlockSpec((rows, cols), lambda s: (blk(s)[0], 0))
    out_shape = jax.ShapeDtypeStruct((x.shape[0], cols), BF16)
    return in_specs, operands, out_spec, out_shape


def _pipelined_matmul_kernel(*refs, n_lhs, n_epi, n_out, n_side, epilogue, bm, bn, epi_rows):
    lhs = refs[:n_lhs]
    rhs = refs[n_lhs:2 * n_lhs]
    epi = refs[2 * n_lhs:2 * n_lhs + n_epi]
    side_in = refs[2 * n_lhs + n_epi:2 * n_lhs + n_epi + n_side]
    n_in = 2 * n_lhs + n_epi + n_side
    outs = refs[n_in:n_in + n_out]
    side_out = refs[n_in + n_out:-2]
    acc_a_ref, acc_b_ref = refs[-2:]
    s = pl.program_id(0)

    @pl.when(s == 0)
    def _():
        acc_b_ref[...] = jnp.zeros(acc_b_ref.shape, acc_b_ref.dtype)

    def step(new_ref, old_ref):
        zero = None
        for r0 in range(0, bm, epi_rows):
            chunk = old_ref[r0:r0 + epi_rows, :]
            if zero is not None:
                chunk = chunk + zero
            last = epilogue(chunk, r0, epi, outs)
            zero = jnp.concatenate([_dependent_zero(last)] * (bn // 128), axis=1)
        if side_in:
            _convert_weight_block(side_in[0], side_in[1] if n_side == 2 else None, side_out[0])
        acc = jnp.dot(lhs[0][...], rhs[0][...], preferred_element_type=F32)
        for a_ref, w_ref in zip(lhs[1:], rhs[1:]):
            acc = acc + jnp.dot(a_ref[...], w_ref[...], preferred_element_type=F32)
        new_ref[...] = acc

    @pl.when(s % 2 == 0)
    def _():
        step(acc_a_ref, acc_b_ref)

    @pl.when(s % 2 == 1)
    def _():
        step(acc_b_ref, acc_a_ref)


def _pipelined_matmul(name, lhs, rhs, rhs_blocks, epi, epi_specs, out_specs, out_shapes, epilogue,
                      *, nj, bm, bn, epi_rows, vmem_mib, side=None, lhs_cols=None):
    ni = lhs[0].shape[0] // bm
    n_blocks = ni * nj

    def cur_ij(s):
        return jnp.minimum(s // nj, ni - 1), s % nj

    def prev_ij(s):
        sp = jnp.maximum(s - 1, 0)
        return sp // nj, sp % nj

    if lhs_cols is None:
        lhs_cols = [(a.shape[1], 0) for a in lhs]
    in_specs = [pl.BlockSpec((bm, width), functools.partial(lambda s, c: (cur_ij(s)[0], c), c=c))
                for width, c in lhs_cols]
    in_specs += [pl.BlockSpec((width, bn),
                              functools.partial(lambda s, blk: blk(cur_ij(s)[1]), blk=blk))
                 for (width, _), blk in zip(lhs_cols, rhs_blocks)]
    in_specs += [pl.BlockSpec(shape, functools.partial(lambda s, fn: fn(*prev_ij(s)), fn=fn))
                 for shape, fn in epi_specs]
    out_block_specs = [pl.BlockSpec(shape, functools.partial(lambda s, fn: fn(*prev_ij(s)), fn=fn))
                       for shape, fn in out_specs]
    out_shapes = list(out_shapes)
    side_operands = []
    if side is not None:
        side_in_specs, side_operands, side_out_spec, side_out_shape = _side_specs(side, n_blocks + 1)
        in_specs += side_in_specs
        out_block_specs.append(side_out_spec)
        out_shapes.append(side_out_shape)
    return pl.pallas_call(
        functools.partial(_pipelined_matmul_kernel, n_lhs=len(lhs), n_epi=len(epi),
                          n_out=len(out_specs), n_side=len(side_operands), epilogue=epilogue,
                          bm=bm, bn=bn, epi_rows=epi_rows),
        grid=(n_blocks + 1,),
        in_specs=in_specs,
        out_specs=out_block_specs,
        out_shape=out_shapes,
        scratch_shapes=[pltpu.VMEM((bm, bn), F32), pltpu.VMEM((bm, bn), F32)],
        compiler_params=_params(("arbitrary",), vmem_mib),
        name=name,
    )(*lhs, *rhs, *epi, *side_operands)


def _heads_epilogue(chunk, r0, epi, outs, *, rows, kind):
    (o_ref,) = outs
    y = None
    for hh in range(chunk.shape[1] // HEAD_DIM):
        cols = slice(hh * HEAD_DIM, (hh + 1) * HEAD_DIM)
        y = chunk[:, cols]
        if kind == "gelu":
            y = _gelu(y)
        elif kind == "norm":
            y = _rms(y, epi[0][:, cols])
        o_ref[hh, r0:r0 + rows, :] = y.astype(o_ref.dtype)
    return y[0:1, :]


def _inproj(hn, w, col0, n_cols, kind, gain=None, bm=1024, bn=1024, side=None):
    t = hn.shape[0]
    hpb = bn // HEAD_DIM
    epi_rows = {"norm": 32, "gelu": 8, "plain": 16}[kind]
    epi, epi_specs = [], []
    if kind == "norm":
        epi, epi_specs = [gain], [((1, bn), lambda i, j: (0, j))]
    return _pipelined_matmul(
        "inproj_" + kind, [hn], [w], [lambda j: (0, col0 // bn + j)], epi, epi_specs,
        [((hpb, bm, HEAD_DIM), lambda i, j: (j, i, 0))],
        [jax.ShapeDtypeStruct((n_cols // HEAD_DIM, t, HEAD_DIM), BF16)],
        functools.partial(_heads_epilogue, rows=epi_rows, kind=kind),
        nj=n_cols // bn, bm=bm, bn=bn, epi_rows=epi_rows, vmem_mib=62, side=side)


def _gmlp_kernel(u_ref, v_ref, ws_ref, bs_ref, gv_ref, og_ref, o_ref, *, n_groups):
    ws = ws_ref[0]
    bs = bs_ref[0]
    gv = gv_ref[0]
    og = og_ref[0]

    def group_body(gi, carry):
        base = pl.multiple_of(gi * (GMLP_GROUP * CHUNK), GMLP_GROUP * CHUNK)
        for c in range(GMLP_GROUP):
            sl = pl.ds(base + c * CHUNK, CHUNK)
            vn = _rms(v_ref[0, sl, :].astype(F32), gv).astype(BF16)
            mixed = jnp.dot(ws, vn, preferred_element_type=F32) + bs
            a = u_ref[0, sl, :].astype(F32) * mixed
            o_ref[sl, :] = _rms(a, og).astype(o_ref.dtype)
        return carry

    lax.fori_loop(0, n_groups, group_body, 0)


def _gmlp(z, ws, bs, gv, og, n_heads, tb=8192):
    t = z.shape[1]
    return pl.pallas_call(
        functools.partial(_gmlp_kernel, n_groups=tb // (GMLP_GROUP * CHUNK)),
        grid=(n_heads, t // tb),
        in_specs=[pl.BlockSpec((1, tb, HEAD_DIM), lambda h, i: (h, i, 0)),
                  pl.BlockSpec((1, tb, HEAD_DIM), lambda h, i: (n_heads + h, i, 0)),
                  pl.BlockSpec((1, CHUNK, CHUNK), lambda h, i: (h, 0, 0)),
                  pl.BlockSpec((1, CHUNK, 1), lambda h, i: (h, 0, 0)),
                  pl.BlockSpec((1, 1, HEAD_DIM), lambda h, i: (h, 0, 0)),
                  pl.BlockSpec((1, 1, HEAD_DIM), lambda h, i: (h, 0, 0))],
        out_specs=pl.BlockSpec((tb, HEAD_DIM), lambda h, i: (i, h)),
        out_shape=jax.ShapeDtypeStruct((t, n_heads * HEAD_DIM), BF16),
        compiler_params=_params(("parallel", "parallel"), 24),
        name="gmlp",
    )(z, z, ws.astype(BF16), bs.reshape(n_heads, CHUNK, 1),
      gv.reshape(n_heads, 1, HEAD_DIM), og.reshape(n_heads, 1, HEAD_DIM))


def _na_toeplitz(rpb):
    col = np.arange(GRID_W)
    cs = np.clip(col - WIN_COLS // 2, 0, GRID_W - WIN_COLS)
    kc = np.arange(GRID_W)
    valid = (kc[None, :] >= cs[:, None]) & (kc[None, :] < cs[:, None] + WIN_COLS)
    dc = kc[None, :] - col[:, None] + (WIN_COLS - 1)
    onehot = (dc[:, :, None] == np.arange(2 * WIN_COLS - 1)).astype(np.float32)
    t = jnp.einsum('hrd,ckd->hrck', rpb, onehot, precision=lax.Precision.HIGHEST)
    return jnp.where(valid[None, None], t, MASKED)


def _na_group_geometry(rows):
    def geom(g):
        r0 = g * NA_G
        ws = int(np.clip(r0 - WIN_ROWS // 2, 0, rows - NA_WIN))
        rs = [int(np.clip(r0 + i - WIN_ROWS // 2, 0, rows - WIN_ROWS)) for i in range(NA_G)]
        return r0 - ws, tuple(r - ws for r in rs)

    ng = rows // NA_G
    assert ng >= 3 and all(geom(g) == geom(1) for g in range(1, ng - 1))
    return geom(0), geom(1), geom(ng - 1)


def _na_kernel(q_ref, k_ref, v_ref, t_ref, og_ref, o_ref, bias_ref, *, rows):
    log2e = np.float32(NA_LOG2E)
    og = og_ref[0]

    masked_tile = jnp.full((GRID_W, GRID_W), MASKED, F32)
    for cls, (roff, rsoff) in enumerate(_na_group_geometry(rows)):
        for ri in range(NA_G):
            for kp in range(NA_WIN // 2):
                pair = []
                for kr in (2 * kp, 2 * kp + 1):
                    in_window = 0 <= kr - rsoff[ri] < WIN_ROWS
                    dr = kr - roff - ri + (WIN_ROWS - 1)
                    pair.append(t_ref[0, dr] * log2e if in_window else masked_tile)
                bias_ref[cls, ri * GRID_W:(ri + 1) * GRID_W, kp * 128:(kp + 1) * 128] = (
                    jnp.concatenate(pair, axis=1))

    ng = rows // NA_G
    nq = NA_G * GRID_W
    nk = NA_WIN * GRID_W

    def group_body(g, carry):
        r0 = g * NA_G
        ws = jnp.clip(r0 - WIN_ROWS // 2, 0, rows - NA_WIN)
        cls = jnp.minimum(g, 1) + jnp.maximum(g - (ng - 2), 0)
        qsl = pl.ds(pl.multiple_of(r0 * GRID_W, nq), nq)
        ksl = pl.ds(pl.multiple_of(ws * GRID_W, NA_ALIGN * GRID_W), nk)
        s = lax.dot_general(q_ref[0, qsl, :], k_ref[0, ksl, :], (((1,), (1,)), ((), ())),
                            preferred_element_type=F32)
        s = s + bias_ref[cls]
        m = jnp.max(s, axis=-1, keepdims=True)
        e = jnp.exp2(s - m)
        l = jnp.sum(e, axis=-1, keepdims=True)
        o = jnp.dot(e.astype(BF16), v_ref[0, ksl, :], preferred_element_type=F32) / l
        o_ref[qsl, :] = _rms(o, og).astype(o_ref.dtype)
        return carry

    lax.fori_loop(0, ng, group_body, 0, unroll=4)


def _na_qk_gain(qg, kg, n_heads):
    qg = qg * np.float32(HEAD_DIM ** -0.5 * NA_LOG2E)
    return jnp.concatenate([jnp.tile(qg, n_heads), jnp.tile(kg, n_heads)]).reshape(1, -1)


def _na(zqk, zv, toeplitz, og, n_heads, batch, seq):
    rows = seq // GRID_W
    assert rows % NA_G == 0 and NA_G % NA_ALIGN == 0 and (rows - NA_WIN) % NA_ALIGN == 0
    n_dr = 2 * WIN_ROWS - 1
    return pl.pallas_call(
        functools.partial(_na_kernel, rows=rows),
        grid=(batch, n_heads),
        in_specs=[pl.BlockSpec((1, seq, HEAD_DIM), lambda b, h: (h, b, 0)),
                  pl.BlockSpec((1, seq, HEAD_DIM), lambda b, h: (n_heads + h, b, 0)),
                  pl.BlockSpec((1, seq, HEAD_DIM), lambda b, h: (h, b, 0)),
                  pl.BlockSpec((1, n_dr, GRID_W, GRID_W), lambda b, h: (h, 0, 0, 0)),
                  pl.BlockSpec((1, 1, HEAD_DIM), lambda b, h: (h, 0, 0))],
        out_specs=pl.BlockSpec((seq, HEAD_DIM), lambda b, h: (b, h)),
        out_shape=jax.ShapeDtypeStruct((batch * seq, n_heads * HEAD_DIM), BF16),
        scratch_shapes=[pltpu.VMEM((3, NA_G * GRID_W, NA_WIN * GRID_W), F32)],
        compiler_params=_params(("parallel", "parallel"), 40),
        name="natten",
    )(zqk, zqk, zv, toeplitz, og.reshape(n_heads, 1, HEAD_DIM))


def _residual_epilogue(chunk, r0, epi, outs, *, rows):
    o_ref, ob_ref, ssq_ref = outs
    sl = slice(r0, r0 + rows)
    h = epi[0][sl, :]
    for part_ref in epi[1:]:
        h = h + part_ref[sl, :]
    h = h + chunk
    o_ref[sl, :] = h
    ob_ref[sl, :] = h.astype(ob_ref.dtype)
    ssq = jnp.broadcast_to(jnp.sum(h * h, axis=-1, keepdims=True), (rows, 128))
    ssq_ref[sl, :] = ssq
    return ssq[0:1, :]


def _residual_matmul(name, lhs, rhs, rhs_blocks, addends, *, bm, epi_rows, vmem_mib, side=None,
                     lhs_cols=None):
    t, n = addends[0].shape
    bn = SSQ_BN
    block = ((bm, bn), lambda i, j: (i, j))
    return _pipelined_matmul(
        name, lhs, rhs, rhs_blocks, addends, [block] * len(addends),
        [block, block, ((bm, 128), lambda i, j: (i, j))],
        [jax.ShapeDtypeStruct((t, n), F32), jax.ShapeDtypeStruct((t, n), BF16),
         jax.ShapeDtypeStruct((t, n // bn * 128), F32)],
        functools.partial(_residual_epilogue, rows=epi_rows),
        nj=n // bn, bm=bm, bn=bn, epi_rows=epi_rows, vmem_mib=vmem_mib, side=side,
        lhs_cols=lhs_cols)


def _row_rstd_kernel(s_ref, o_ref, *, d):
    tot = s_ref[:, 0:128]
    for c in range(1, s_ref.shape[1] // 128):
        tot = tot + s_ref[:, c * 128:(c + 1) * 128]
    o_ref[...] = lax.rsqrt(tot * np.float32(1.0 / d) + EPS)


def _row_rstd(ssq, d, bm=2048):
    t, sw = ssq.shape
    return pl.pallas_call(
        functools.partial(_row_rstd_kernel, d=d),
        grid=(t // bm,),
        in_specs=[pl.BlockSpec((bm, sw), lambda i: (i, 0))],
        out_specs=pl.BlockSpec((bm, 128), lambda i: (i, 0)),
        out_shape=jax.ShapeDtypeStruct((t, 128), F32),
        compiler_params=_params(("parallel",), 32),
        name="row_rstd",
    )(ssq)


def _outproj(mix_a, mix_b, w, x, bm=1024):
    assert mix_a.shape[1] == mix_b.shape[1]
    return _residual_matmul("outproj", [mix_a, mix_b], [w, w],
                            [lambda j: (0, j), lambda j: (1, j)], [x],
                            bm=bm, epi_rows=32, vmem_mib=48)


def _ffn_up_window_start(i, bm, t):
    return pl.multiple_of(jnp.clip(i * bm - HALO, 0, t - (bm + 2 * HALO)), HALO)


def _ffn_up_kernel(x_ref, r_ref, wg_ref, wv_ref, cwg_ref, cwv_ref, cbg_ref, cbv_ref, side_ref,
                   o_ref, side_o_ref, rstd_ref, up_a_ref, up_b_ref, *, bm, bn, nj, ni, seq):
    s = pl.program_id(0)
    win = bm + 2 * HALO
    t = ni * bm
    i = jnp.minimum(s // nj, ni - 1)
    start = _ffn_up_window_start(i, bm, t)
    ip = jnp.maximum(s - 1, 0) // nj
    off_prev = pl.multiple_of(ip * bm - _ffn_up_window_start(ip, bm, t), HALO)

    @pl.when((s % nj == 0) & (s < ni * nj))
    def _():
        tok = start + lax.broadcasted_iota(jnp.int32, (win, 128), 0)
        seq_start = (i * bm) // seq * seq
        same_seq = (tok >= seq_start) & (tok < seq_start + seq)
        rstd_ref[...] = jnp.where(same_seq, r_ref[...], 0.0)

    @pl.when(s == 0)
    def _():
        up_a_ref[...] = jnp.zeros(up_a_ref.shape, up_a_ref.dtype)
        up_b_ref[...] = jnp.zeros(up_b_ref.shape, up_b_ref.dtype)

    def step(new_ref, old_ref):
        cw = jnp.concatenate([cwg_ref[...], cwv_ref[...]], axis=1)
        cb = jnp.concatenate([cbg_ref[...], cbv_ref[...]], axis=1)
        w_cur = cw[1:2, :]
        for r0 in range(0, bm, EPI_ROWS):
            slab = old_ref[pl.ds(pl.multiple_of(off_prev + r0, 8), EPI_ROWS + 16), :]
            prev = pltpu.roll(slab, 1, 0)[8:8 + EPI_ROWS]
            cur = slab[8:8 + EPI_ROWS]
            nxt = pltpu.roll(slab, EPI_ROWS + 15, 0)[8:8 + EPI_ROWS]
            c = cb + (prev * cw[0:1, :] + cur * w_cur + nxt * cw[2:3, :])
            act = _gelu(c[:, :bn]) * c[:, bn:]
            o_ref[r0:r0 + EPI_ROWS, :] = act.astype(o_ref.dtype)
            zero = _dependent_zero(act[0:1, 0:128])
            w_cur = cw[1:2, :] + jnp.concatenate([zero] * (2 * bn // 128), axis=1)

        _convert_weight_block(side_ref, None, side_o_ref)

        half = win // 2
        for h0 in (0, half):
            xh = x_ref[h0:h0 + half, :]
            rstd = jnp.concatenate([rstd_ref[h0:h0 + half, :]] * (bn // 128), axis=1)
            new_ref[8 + h0:8 + h0 + half, :bn] = (
                jnp.dot(xh, wg_ref[...], preferred_element_type=F32) * rstd)
            new_ref[8 + h0:8 + h0 + half, bn:] = (
                jnp.dot(xh, wv_ref[...], preferred_element_type=F32) * rstd)

    @pl.when(s % 2 == 0)
    def _():
        step(up_a_ref, up_b_ref)

    @pl.when(s % 2 == 1)
    def _():
        step(up_b_ref, up_a_ref)


def _ffn_up(xn, rstd, wg, wv, cw, cb, seq, side, bm=2048, bn=256):
    t, k = xn.shape
    n = wg.shape[1]
    ni, nj = t // bm, n // bn
    win = bm + 2 * HALO
    assert seq % bm == 0 and t >= win and win % (2 * HALO) == 0
    side_in_specs, side_operands, side_out_spec, side_out_shape = _side_specs(side, ni * nj + 1)
    assert len(side_operands) == 1

    def window(s):
        return _ffn_up_window_start(jnp.minimum(s // nj, ni - 1), bm, t), 0

    def prev_ij(s):
        sp = jnp.maximum(s - 1, 0)
        return sp // nj, sp % nj

    return pl.pallas_call(
        functools.partial(_ffn_up_kernel, bm=bm, bn=bn, nj=nj, ni=ni, seq=seq),
        grid=(ni * nj + 1,),
        in_specs=[pl.BlockSpec((pl.Element(win), pl.Element(k)), window,
                               pipeline_mode=pl.Buffered(1)),
                  pl.BlockSpec((pl.Element(win), pl.Element(128)), window),
                  pl.BlockSpec((k, bn), lambda s: (0, s % nj)),
                  pl.BlockSpec((k, bn), lambda s: (0, s % nj)),
                  pl.BlockSpec((3, bn), lambda s: (0, prev_ij(s)[1])),
                  pl.BlockSpec((3, bn), lambda s: (0, nj + prev_ij(s)[1])),
                  pl.BlockSpec((1, bn), lambda s: (0, prev_ij(s)[1])),
                  pl.BlockSpec((1, bn), lambda s: (0, nj + prev_ij(s)[1]))] + side_in_specs,
        out_specs=[pl.BlockSpec((bm, bn), lambda s: prev_ij(s)), side_out_spec],
        out_shape=[jax.ShapeDtypeStruct((t, n), BF16), side_out_shape],
        scratch_shapes=[pltpu.VMEM((win, 128), F32),
                        pltpu.VMEM((win + 16, 2 * bn), F32),
                        pltpu.VMEM((win + 16, 2 * bn), F32)],
        compiler_params=_params(("arbitrary",), 48),
        name="ffn_up",
    )(xn, rstd, wg, wv, cw, cw, cb, cb, *side_operands)


def _store_epilogue(chunk, r0, epi, outs, *, rows):
    (o_ref,) = outs
    o_ref[r0:r0 + rows, :] = chunk
    return chunk[0:1, 0:128]


def _ffn_down(act, w, h, bm=1024, side=None):
    t, k = act.shape
    n = w.shape[1]
    kh = k // 2
    bn = SSQ_BN
    block = ((bm, bn), lambda i, j: (i, j))
    (part,) = _pipelined_matmul(
        "ffn_down_a", [act], [w], [lambda j: (0, j)], [], [],
        [block], [jax.ShapeDtypeStruct((t, n), F32)],
        functools.partial(_store_epilogue, rows=32),
        nj=n // bn, bm=bm, bn=bn, epi_rows=32, vmem_mib=56, lhs_cols=[(kh, 0)])
    return _residual_matmul("ffn_down_b", [act], [w], [lambda j: (1, j)], [h, part],
                            bm=bm, epi_rows=32, vmem_mib=56, side=side, lhs_cols=[(kh, 1)])


def _ple_embed_kernel(p_ref, w_ref, g_ref, o_ref):
    e = jnp.dot(p_ref[...].astype(BF16), w_ref[...], preferred_element_type=F32)
    o_ref[...] = _rms(e, g_ref[...]).astype(o_ref.dtype)


def _ple_embed(p, w, g, bm=512):
    t, k = p.shape
    n = w.shape[1]
    return pl.pallas_call(
        _ple_embed_kernel,
        grid=(t // bm,),
        in_specs=[pl.BlockSpec((bm, k), lambda i: (i, 0)),
                  pl.BlockSpec((k, n), lambda i: (0, 0)),
                  pl.BlockSpec((1, n), lambda i: (0, 0))],
        out_specs=pl.BlockSpec((bm, n), lambda i: (i, 0)),
        out_shape=jax.ShapeDtypeStruct((t, n), BF16),
        compiler_params=_params(("parallel",), 40),
        name="ple_embed",
    )(p, w, g.reshape(1, n))


def _ple_gate_epilogue(chunk, r0, epi, outs, *, rows):
    rstd_ref, e_ref, h_ref = epi
    (o_ref,) = outs
    sl = slice(r0, r0 + rows)
    rstd = jnp.concatenate([rstd_ref[sl, :]] * (chunk.shape[1] // 128), axis=1)
    out = h_ref[sl, :] + jax.nn.sigmoid(chunk * rstd) * e_ref[sl, :].astype(F32)
    o_ref[sl, :] = out
    return out[0:1, 0:128]


def _ple_gate(hb, rstd, w, e, h, bm=1024, bn=512, epi_rows=8):
    t = hb.shape[0]
    n = w.shape[1]
    block = ((bm, bn), lambda i, j: (i, j))
    return _pipelined_matmul(
        "ple_gate", [hb], [w], [lambda j: (0, j)],
        [rstd, e, h], [((bm, 128), lambda i, j: (i, 0)), block, block],
        [block], [jax.ShapeDtypeStruct((t, n), F32)],
        functools.partial(_ple_gate_epilogue, rows=epi_rows),
        nj=n // bn, bm=bm, bn=bn, epi_rows=epi_rows, vmem_mib=48)[0]


def kernel(x, p, norm_mix_g, w_in, gmlp_v_g, gmlp_ws, gmlp_bs, q_norm_g, k_norm_g, na_rpb,
           out_norm_a_g, out_norm_b_g, w_out, norm_ffn_g, w_up, conv_w, conv_b, w_down,
           norm_ple_g, w_ple_gate, w_ple_proj, ple_post_g):
    batch, seq, d_model = x.shape
    t = batch * seq
    depth = w_in.shape[0]
    n_a = gmlp_ws.shape[1]
    n_b = na_rpb.shape[1]
    h = x.reshape(t, d_model)
    for i in range(depth):
        hn = _rmsnorm(h, norm_mix_g[i])
        w = w_in[i].astype(BF16)
        d_a, d_b = n_a * HEAD_DIM, n_b * HEAD_DIM
        d_ff = w_down.shape[1]
        g_ffn = jnp.broadcast_to(norm_ffn_g[i][:, None], (d_model, 128))
        g_ple = jnp.broadcast_to(norm_ple_g[i][:, None], (d_model, 128))
        zuv, w_gate = _inproj(hn, w, 0, 2 * d_a, "gelu",
                              side=(w_up[i], g_ffn, 64, d_ff, 0))
        zqk, w_val = _inproj(hn, w, 2 * d_a, 2 * d_b, "norm",
                             _na_qk_gain(q_norm_g[i], k_norm_g[i], n_b),
                             side=(w_up[i], g_ffn, 64, d_ff, 1))
        zv, w_o = _inproj(hn, w, 2 * d_a + 2 * d_b, d_b, "plain",
                          side=(w_out[i], None, 128, d_model, 0))
        mix_a = _gmlp(zuv, gmlp_ws[i], gmlp_bs[i], gmlp_v_g[i], out_norm_a_g[i], n_a)
        mix_b = _na(zqk, zv, _na_toeplitz(na_rpb[i]), out_norm_b_g[i], n_b, batch, seq)
        h, hb, ssq = _outproj(mix_a, mix_b, w_o, h)
        act, w_dn = _ffn_up(hb, _row_rstd(ssq, d_model), w_gate, w_val,
                            conv_w[i], conv_b[i].reshape(1, -1), seq,
                            side=(w_down[i], None, 32, d_model, 0))
        h, hb, ssq, w_pg = _ffn_down(act, w_dn, h, side=(w_ple_gate[i], g_ple, 32, d_model, 0))
        e = _ple_embed(p[i].reshape(t, -1), w_ple_proj[i].astype(BF16), ple_post_g[i])
        h = _ple_gate(hb, _row_rstd(ssq, d_model), w_pg, e, h)
    return h.reshape(batch, seq, d_model)
```

```python
import functools

import numpy as np
import jax
import jax.numpy as jnp
from jax import lax
from jax.experimental import pallas as pl
from jax.experimental.pallas import tpu as pltpu

F32 = jnp.float32
BF16 = jnp.bfloat16

LANES = 128
SUBLANES = 8

EPS = 1e-6
HEAD_DIM = 128
CHUNK = 128
GMLP_GROUP = 8
GRID_W = 64
WIN_ROWS = 8
WIN_COLS = 16
NA_G = 4
NA_WIN = NA_G + WIN_ROWS
MASKED = -1e30
NA_LOG2E = float(np.log2(np.e))
HALO = 16
EPI_ROWS = 8
SSQ_BN = 512
MIB = 1 << 20


def _params(semantics, vmem_mib):
    return pltpu.CompilerParams(dimension_semantics=semantics, vmem_limit_bytes=vmem_mib * MIB)


def _rms(x, g):
    ms = jnp.mean(x * x, axis=-1, keepdims=True)
    return x * lax.rsqrt(ms + EPS) * g


def _gelu(x):
    return 0.5 * x * (1.0 + lax.erf(x * np.float32(np.sqrt(0.5))))


def _rmsnorm_kernel(x_ref, g_ref, o_ref):
    o_ref[...] = _rms(x_ref[...], g_ref[...]).astype(o_ref.dtype)


def _rmsnorm(x, g, bm=512):
    t, d = x.shape
    return pl.pallas_call(
        _rmsnorm_kernel,
        grid=(t // bm,),
        in_specs=[pl.BlockSpec((bm, d), lambda i: (i, 0)),
                  pl.BlockSpec((1, d), lambda i: (0, 0))],
        out_specs=pl.BlockSpec((bm, d), lambda i: (i, 0)),
        out_shape=jax.ShapeDtypeStruct((t, d), BF16),
        compiler_params=_params(("parallel",), 40),
        name="rmsnorm",
    )(x, g.reshape(1, d))


def _dependent_zero(x):
    bits = pltpu.bitcast(x, jnp.uint32)
    return pltpu.bitcast((bits >> 16) >> 16, F32)


SIDE_COLS = 256


def _convert_weight_block(x_ref, g_ref, o_ref):
    g = None if g_ref is None else jnp.concatenate([g_ref[...]] * (SIDE_COLS // LANES), axis=1)
    for c in range(0, x_ref.shape[1], SIDE_COLS):
        blk = x_ref[:, c:c + SIDE_COLS]
        if g is not None:
            blk = blk * g
        o_ref[:, c:c + SIDE_COLS] = blk.astype(o_ref.dtype)


def _side_specs(side, n_steps):
    x, gain, rows, cols, cj = side
    n_side = x.shape[0] // rows
    assert n_side <= n_steps and x.shape[0] % rows == 0 and cols % SIDE_COLS == 0

    def blk(s):
        return jnp.minimum(s, n_side - 1), cj

    in_specs = [pl.BlockSpec((rows, cols), blk)]
    operands = [x]
    if gain is not None:
        in_specs.append(pl.BlockSpec((rows, LANES), lambda s: (blk(s)[0], 0)))
        operands.append(gain)
    out_spec = pl.BlockSpec((rows, cols), lambda s: (blk(s)[0], 0))
    out_shape = jax.ShapeDtypeStruct((x.shape[0], cols), BF16)
    return in_specs, operands, out_spec, out_shape


def _pipelined_matmul_kernel(*refs, n_lhs, n_epi, n_out, n_side, epilogue, bm, bn, epi_rows):
    lhs = refs[:n_lhs]
    rhs = refs[n_lhs:2 * n_lhs]
    epi = refs[2 * n_lhs:2 * n_lhs + n_epi]
    side_in = refs[2 * n_lhs + n_epi:2 * n_lhs + n_epi + n_side]
    n_in = 2 * n_lhs + n_epi + n_side
    outs = refs[n_in:n_in + n_out]
    side_out = refs[n_in + n_out:-2]
    acc_a_ref, acc_b_ref = refs[-2:]
    s = pl.program_id(0)

    @pl.when(s == 0)
    def _():
        acc_b_ref[...] = jnp.zeros(acc_b_ref.shape, acc_b_ref.dtype)

    def step(new_ref, old_ref):
        zero = None
        for r0 in range(0, bm, epi_rows):
            chunk = old_ref[r0:r0 + epi_rows, :]
            if zero is not None:
                chunk = chunk + zero
            last = epilogue(chunk, r0, epi, outs)
            zero = jnp.concatenate([_dependent_zero(last)] * (bn // LANES), axis=1)
        if side_in:
            _convert_weight_block(side_in[0], side_in[1] if n_side == 2 else None, side_out[0])
        acc = jnp.dot(lhs[0][...], rhs[0][...], preferred_element_type=F32)
        for a_ref, w_ref in zip(lhs[1:], rhs[1:]):
            acc = acc + jnp.dot(a_ref[...], w_ref[...], preferred_element_type=F32)
        new_ref[...] = acc

    @pl.when(s % 2 == 0)
    def _():
        step(acc_a_ref, acc_b_ref)

    @pl.when(s % 2 == 1)
    def _():
        step(acc_b_ref, acc_a_ref)


def _pipelined_matmul(name, lhs, rhs, rhs_blocks, epi, epi_specs, out_specs, out_shapes, epilogue,
                      *, nj, bm, bn, epi_rows, vmem_mib, side=None):
    ni = lhs[0].shape[0] // bm
    n_blocks = ni * nj

    def cur_ij(s):
        return jnp.minimum(s // nj, ni - 1), s % nj

    def prev_ij(s):
        sp = jnp.maximum(s - 1, 0)
        return sp // nj, sp % nj

    def cur_row(s):
        return cur_ij(s)[0], 0

    in_specs = [pl.BlockSpec((bm, a.shape[1]), cur_row) for a in lhs]
    in_specs += [pl.BlockSpec((a.shape[1], bn),
                              functools.partial(lambda s, blk: blk(cur_ij(s)[1]), blk=blk))
                 for a, blk in zip(lhs, rhs_blocks)]
    in_specs += [pl.BlockSpec(shape, functools.partial(lambda s, fn: fn(*prev_ij(s)), fn=fn))
                 for shape, fn in epi_specs]
    out_block_specs = [pl.BlockSpec(shape, functools.partial(lambda s, fn: fn(*prev_ij(s)), fn=fn))
                       for shape, fn in out_specs]
    out_shapes = list(out_shapes)
    side_operands = []
    if side is not None:
        side_in_specs, side_operands, side_out_spec, side_out_shape = _side_specs(side, n_blocks + 1)
        in_specs += side_in_specs
        out_block_specs.append(side_out_spec)
        out_shapes.append(side_out_shape)
    return pl.pallas_call(
        functools.partial(_pipelined_matmul_kernel, n_lhs=len(lhs), n_epi=len(epi),
                          n_out=len(out_specs), n_side=len(side_operands), epilogue=epilogue,
                          bm=bm, bn=bn, epi_rows=epi_rows),
        grid=(n_blocks + 1,),
        in_specs=in_specs,
        out_specs=out_block_specs,
        out_shape=out_shapes,
        scratch_shapes=[pltpu.VMEM((bm, bn), F32), pltpu.VMEM((bm, bn), F32)],
        compiler_params=_params(("arbitrary",), vmem_mib),
        name=name,
    )(*lhs, *rhs, *epi, *side_operands)


def _heads_epilogue(chunk, r0, epi, outs, *, rows, kind):
    (o_ref,) = outs
    y = None
    for hh in range(chunk.shape[1] // HEAD_DIM):
        cols = slice(hh * HEAD_DIM, (hh + 1) * HEAD_DIM)
        y = chunk[:, cols]
        if kind == "gelu":
            y = _gelu(y)
        elif kind == "norm":
            y = _rms(y, epi[0][:, cols])
        o_ref[hh, r0:r0 + rows, :] = y.astype(o_ref.dtype)
    return y[0:1, :]


def _inproj(hn, w, col0, n_cols, kind, gain=None, bm=1024, bn=1024, side=None):
    t = hn.shape[0]
    hpb = bn // HEAD_DIM
    epi_rows = {"norm": 32, "gelu": 8, "plain": 16}[kind]
    epi, epi_specs = [], []
    if kind == "norm":
        epi, epi_specs = [gain], [((1, bn), lambda i, j: (0, j))]
    return _pipelined_matmul(
        "inproj_" + kind, [hn], [w], [lambda j: (0, col0 // bn + j)], epi, epi_specs,
        [((hpb, bm, HEAD_DIM), lambda i, j: (j, i, 0))],
        [jax.ShapeDtypeStruct((n_cols // HEAD_DIM, t, HEAD_DIM), BF16)],
        functools.partial(_heads_epilogue, rows=epi_rows, kind=kind),
        nj=n_cols // bn, bm=bm, bn=bn, epi_rows=epi_rows, vmem_mib=62, side=side)


def _gmlp_kernel(u_ref, v_ref, ws_ref, bs_ref, gv_ref, og_ref, o_ref, *, n_groups):
    ws = ws_ref[0]
    bs = bs_ref[0]
    gv = gv_ref[0]
    og = og_ref[0]

    def group_body(gi, carry):
        base = pl.multiple_of(gi * (GMLP_GROUP * CHUNK), GMLP_GROUP * CHUNK)
        for c in range(GMLP_GROUP):
            sl = pl.ds(base + c * CHUNK, CHUNK)
            vn = _rms(v_ref[0, sl, :].astype(F32), gv).astype(BF16)
            mixed = jnp.dot(ws, vn, preferred_element_type=F32) + bs
            a = u_ref[0, sl, :].astype(F32) * mixed
            o_ref[sl, :] = _rms(a, og).astype(o_ref.dtype)
        return carry

    lax.fori_loop(0, n_groups, group_body, 0)


def _gmlp(z, ws, bs, gv, og, n_heads, tb=8192):
    t = z.shape[1]
    return pl.pallas_call(
        functools.partial(_gmlp_kernel, n_groups=tb // (GMLP_GROUP * CHUNK)),
        grid=(n_heads, t // tb),
        in_specs=[pl.BlockSpec((1, tb, HEAD_DIM), lambda h, i: (h, i, 0)),
                  pl.BlockSpec((1, tb, HEAD_DIM), lambda h, i: (n_heads + h, i, 0)),
                  pl.BlockSpec((1, CHUNK, CHUNK), lambda h, i: (h, 0, 0)),
                  pl.BlockSpec((1, CHUNK, 1), lambda h, i: (h, 0, 0)),
                  pl.BlockSpec((1, 1, HEAD_DIM), lambda h, i: (h, 0, 0)),
                  pl.BlockSpec((1, 1, HEAD_DIM), lambda h, i: (h, 0, 0))],
        out_specs=pl.BlockSpec((tb, HEAD_DIM), lambda h, i: (i, h)),
        out_shape=jax.ShapeDtypeStruct((t, n_heads * HEAD_DIM), BF16),
        compiler_params=_params(("parallel", "parallel"), 24),
        name="gmlp",
    )(z, z, ws.astype(BF16), bs.reshape(n_heads, CHUNK, 1),
      gv.reshape(n_heads, 1, HEAD_DIM), og.reshape(n_heads, 1, HEAD_DIM))


def _na_toeplitz(rpb):
    col = np.arange(GRID_W)
    cs = np.clip(col - WIN_COLS // 2, 0, GRID_W - WIN_COLS)
    kc = np.arange(GRID_W)
    valid = (kc[None, :] >= cs[:, None]) & (kc[None, :] < cs[:, None] + WIN_COLS)
    dc = kc[None, :] - col[:, None] + (WIN_COLS - 1)
    onehot = (dc[:, :, None] == np.arange(2 * WIN_COLS - 1)).astype(np.float32)
    t = jnp.einsum('hrd,ckd->hrck', rpb, onehot, precision=lax.Precision.HIGHEST)
    return jnp.where(valid[None, None], t, MASKED)


def _na_group_geometry(rows):
    def geom(g):
        r0 = g * NA_G
        ws = int(np.clip(r0 - WIN_ROWS // 2, 0, rows - NA_WIN))
        rs = [int(np.clip(r0 + i - WIN_ROWS // 2, 0, rows - WIN_ROWS)) for i in range(NA_G)]
        return r0 - ws, tuple(r - ws for r in rs)

    ng = rows // NA_G
    assert ng >= 3 and all(geom(g) == geom(1) for g in range(1, ng - 1))
    return geom(0), geom(1), geom(ng - 1)


def _na_kernel(q_ref, k_ref, v_ref, t_ref, og_ref, o_ref, bias_ref, s_a_ref, s_b_ref, *, rows):
    log2e = np.float32(NA_LOG2E)
    og = og_ref[0]

    masked_tile = jnp.full((GRID_W, GRID_W), MASKED, F32)
    for cls, (roff, rsoff) in enumerate(_na_group_geometry(rows)):
        for ri in range(NA_G):
            for kp in range(NA_WIN // 2):
                pair = []
                for kr in (2 * kp, 2 * kp + 1):
                    in_window = 0 <= kr - rsoff[ri] < WIN_ROWS
                    dr = kr - roff - ri + (WIN_ROWS - 1)
                    pair.append(t_ref[0, dr] * log2e if in_window else masked_tile)
                bias_ref[cls, ri * GRID_W:(ri + 1) * GRID_W, kp * LANES:(kp + 1) * LANES] = (
                    jnp.concatenate(pair, axis=1))

    ng = rows // NA_G
    nq = NA_G * GRID_W
    nk = NA_WIN * GRID_W

    def slices(g):
        r0 = g * NA_G
        ws = jnp.clip(r0 - WIN_ROWS // 2, 0, rows - NA_WIN)
        return (pl.ds(pl.multiple_of(r0 * GRID_W, nq), nq),
                pl.ds(pl.multiple_of(ws * GRID_W, nq), nk))

    def scores(g, s_ref):
        g = jnp.minimum(g, ng - 1)
        cls = jnp.minimum(g, 1) + jnp.maximum(g - (ng - 2), 0)
        qsl, ksl = slices(g)
        s = lax.dot_general(q_ref[0, qsl, :], k_ref[0, ksl, :], (((1,), (1,)), ((), ())),
                            preferred_element_type=F32)
        s_ref[...] = s + bias_ref[cls]

    def attend(g, s_ref):
        qsl, ksl = slices(g)
        s = s_ref[...]
        m = jnp.max(s, axis=-1, keepdims=True)
        e = jnp.exp2(s - m)
        l = jnp.sum(e, axis=-1, keepdims=True)
        o = jnp.dot(e.astype(BF16), v_ref[0, ksl, :], preferred_element_type=F32) / l
        o_ref[qsl, :] = _rms(o, og).astype(o_ref.dtype)

    scores(0, s_a_ref)

    def pair_body(tp, carry):
        g = 2 * tp
        scores(g + 1, s_b_ref)
        attend(g, s_a_ref)
        scores(g + 2, s_a_ref)
        attend(g + 1, s_b_ref)
        return carry

    lax.fori_loop(0, ng // 2, pair_body, 0, unroll=2)


def _na_qk_gain(qg, kg, n_heads):
    qg = qg * np.float32(HEAD_DIM ** -0.5 * NA_LOG2E)
    return jnp.concatenate([jnp.tile(qg, n_heads), jnp.tile(kg, n_heads)]).reshape(1, -1)


def _na(zqk, zv, toeplitz, og, n_heads, batch, seq):
    rows = seq // GRID_W
    assert rows % (2 * NA_G) == 0 and (WIN_ROWS // 2) % NA_G == 0
    n_dr = 2 * WIN_ROWS - 1
    return pl.pallas_call(
        functools.partial(_na_kernel, rows=rows),
        grid=(batch, n_heads),
        in_specs=[pl.BlockSpec((1, seq, HEAD_DIM), lambda b, h: (h, b, 0)),
                  pl.BlockSpec((1, seq, HEAD_DIM), lambda b, h: (n_heads + h, b, 0)),
                  pl.BlockSpec((1, seq, HEAD_DIM), lambda b, h: (h, b, 0)),
                  pl.BlockSpec((1, n_dr, GRID_W, GRID_W), lambda b, h: (h, 0, 0, 0)),
                  pl.BlockSpec((1, 1, HEAD_DIM), lambda b, h: (h, 0, 0))],
        out_specs=pl.BlockSpec((seq, HEAD_DIM), lambda b, h: (b, h)),
        out_shape=jax.ShapeDtypeStruct((batch * seq, n_heads * HEAD_DIM), BF16),
        scratch_shapes=[pltpu.VMEM((3, NA_G * GRID_W, NA_WIN * GRID_W), F32),
                        pltpu.VMEM((NA_G * GRID_W, NA_WIN * GRID_W), F32),
                        pltpu.VMEM((NA_G * GRID_W, NA_WIN * GRID_W), F32)],
        compiler_params=_params(("parallel", "parallel"), 40),
        name="natten",
    )(zqk, zqk, zv, toeplitz, og.reshape(n_heads, 1, HEAD_DIM))


def _residual_epilogue(chunk, r0, epi, outs, *, rows):
    (h_ref,) = epi
    o_ref, ob_ref, ssq_ref = outs
    sl = slice(r0, r0 + rows)
    h = h_ref[sl, :] + chunk
    o_ref[sl, :] = h
    ob_ref[sl, :] = h.astype(ob_ref.dtype)
    ssq = jnp.broadcast_to(jnp.sum(h * h, axis=-1, keepdims=True), (rows, LANES))
    ssq_ref[sl, :] = ssq
    return ssq[0:1, :]


def _residual_matmul(name, lhs, rhs, rhs_blocks, h, *, bm, epi_rows, vmem_mib, side=None):
    t, n = h.shape
    bn = SSQ_BN
    block = ((bm, bn), lambda i, j: (i, j))
    return _pipelined_matmul(
        name, lhs, rhs, rhs_blocks, [h], [block],
        [block, block, ((bm, LANES), lambda i, j: (i, j))],
        [jax.ShapeDtypeStruct((t, n), F32), jax.ShapeDtypeStruct((t, n), BF16),
         jax.ShapeDtypeStruct((t, n // bn * LANES), F32)],
        functools.partial(_residual_epilogue, rows=epi_rows),
        nj=n // bn, bm=bm, bn=bn, epi_rows=epi_rows, vmem_mib=vmem_mib, side=side)


def _row_rstd_kernel(s_ref, o_ref, *, d):
    tot = s_ref[:, 0:LANES]
    for c in range(1, s_ref.shape[1] // LANES):
        tot = tot + s_ref[:, c * LANES:(c + 1) * LANES]
    o_ref[...] = lax.rsqrt(tot * np.float32(1.0 / d) + EPS)


def _row_rstd(ssq, d, bm=2048):
    t, sw = ssq.shape
    return pl.pallas_call(
        functools.partial(_row_rstd_kernel, d=d),
        grid=(t // bm,),
        in_specs=[pl.BlockSpec((bm, sw), lambda i: (i, 0))],
        out_specs=pl.BlockSpec((bm, LANES), lambda i: (i, 0)),
        out_shape=jax.ShapeDtypeStruct((t, LANES), F32),
        compiler_params=_params(("parallel",), 32),
        name="row_rstd",
    )(ssq)


def _outproj(mix_a, mix_b, w, x, bm=1024):
    assert mix_a.shape[1] == mix_b.shape[1]
    return _residual_matmul("outproj", [mix_a, mix_b], [w, w],
                            [lambda j: (0, j), lambda j: (1, j)], x,
                            bm=bm, epi_rows=32, vmem_mib=48)


def _ffn_up_window_start(i, bm, t):
    return pl.multiple_of(jnp.clip(i * bm - HALO, 0, t - (bm + 2 * HALO)), HALO)


def _ffn_up_kernel(x_ref, r_ref, wg_ref, wv_ref, cwg_ref, cwv_ref, cbg_ref, cbv_ref, side_ref,
                   o_ref, side_o_ref, rstd_ref, up_a_ref, up_b_ref, *, bm, bn, nj, ni, seq):
    s = pl.program_id(0)
    win = bm + 2 * HALO
    t = ni * bm
    i = jnp.minimum(s // nj, ni - 1)
    start = _ffn_up_window_start(i, bm, t)
    ip = jnp.maximum(s - 1, 0) // nj
    off_prev = pl.multiple_of(ip * bm - _ffn_up_window_start(ip, bm, t), HALO)

    @pl.when((s % nj == 0) & (s < ni * nj))
    def _():
        tok = start + lax.broadcasted_iota(jnp.int32, (win, LANES), 0)
        seq_start = (i * bm) // seq * seq
        same_seq = (tok >= seq_start) & (tok < seq_start + seq)
        rstd_ref[...] = jnp.where(same_seq, r_ref[...], 0.0)

    @pl.when(s == 0)
    def _():
        up_a_ref[...] = jnp.zeros(up_a_ref.shape, up_a_ref.dtype)
        up_b_ref[...] = jnp.zeros(up_b_ref.shape, up_b_ref.dtype)

    def step(new_ref, old_ref):
        cw = jnp.concatenate([cwg_ref[...], cwv_ref[...]], axis=1)
        cb = jnp.concatenate([cbg_ref[...], cbv_ref[...]], axis=1)
        w_cur = cw[1:2, :]
        for r0 in range(0, bm, EPI_ROWS):
            rows = EPI_ROWS + 2 * SUBLANES
            mid = slice(SUBLANES, SUBLANES + EPI_ROWS)
            slab = old_ref[pl.ds(pl.multiple_of(off_prev + r0, SUBLANES), rows), :]
            prev = pltpu.roll(slab, 1, 0)[mid]
            cur = slab[mid]
            nxt = pltpu.roll(slab, rows - 1, 0)[mid]
            c = cb + (prev * cw[0:1, :] + cur * w_cur + nxt * cw[2:3, :])
            act = _gelu(c[:, :bn]) * c[:, bn:]
            o_ref[r0:r0 + EPI_ROWS, :] = act.astype(o_ref.dtype)
            zero = _dependent_zero(act[0:1, 0:LANES])
            w_cur = cw[1:2, :] + jnp.concatenate([zero] * (2 * bn // LANES), axis=1)

        _convert_weight_block(side_ref, None, side_o_ref)

        half = win // 2
        for h0 in (0, half):
            xh = x_ref[h0:h0 + half, :]
            rstd = jnp.concatenate([rstd_ref[h0:h0 + half, :]] * (bn // LANES), axis=1)
            dst = slice(SUBLANES + h0, SUBLANES + h0 + half)
            new_ref[dst, :bn] = jnp.dot(xh, wg_ref[...], preferred_element_type=F32) * rstd
            new_ref[dst, bn:] = jnp.dot(xh, wv_ref[...], preferred_element_type=F32) * rstd

    @pl.when(s % 2 == 0)
    def _():
        step(up_a_ref, up_b_ref)

    @pl.when(s % 2 == 1)
    def _():
        step(up_b_ref, up_a_ref)


def _ffn_up(xn, rstd, wg, wv, cw, cb, seq, side, bm=2048, bn=256):
    t, k = xn.shape
    n = wg.shape[1]
    ni, nj = t // bm, n // bn
    win = bm + 2 * HALO
    assert seq % bm == 0 and t >= win and win % (2 * HALO) == 0
    side_in_specs, side_operands, side_out_spec, side_out_shape = _side_specs(side, ni * nj + 1)
    assert len(side_operands) == 1

    def window(s):
        return _ffn_up_window_start(jnp.minimum(s // nj, ni - 1), bm, t), 0

    def prev_ij(s):
        sp = jnp.maximum(s - 1, 0)
        return sp // nj, sp % nj

    return pl.pallas_call(
        functools.partial(_ffn_up_kernel, bm=bm, bn=bn, nj=nj, ni=ni, seq=seq),
        grid=(ni * nj + 1,),
        in_specs=[pl.BlockSpec((pl.Element(win), pl.Element(k)), window,
                               pipeline_mode=pl.Buffered(1)),
                  pl.BlockSpec((pl.Element(win), pl.Element(LANES)), window),
                  pl.BlockSpec((k, bn), lambda s: (0, s % nj)),
                  pl.BlockSpec((k, bn), lambda s: (0, s % nj)),
                  pl.BlockSpec((3, bn), lambda s: (0, prev_ij(s)[1])),
                  pl.BlockSpec((3, bn), lambda s: (0, nj + prev_ij(s)[1])),
                  pl.BlockSpec((1, bn), lambda s: (0, prev_ij(s)[1])),
                  pl.BlockSpec((1, bn), lambda s: (0, nj + prev_ij(s)[1]))] + side_in_specs,
        out_specs=[pl.BlockSpec((bm, bn), lambda s: prev_ij(s)), side_out_spec],
        out_shape=[jax.ShapeDtypeStruct((t, n), BF16), side_out_shape],
        scratch_shapes=[pltpu.VMEM((win, LANES), F32),
                        pltpu.VMEM((win + 2 * SUBLANES, 2 * bn), F32),
                        pltpu.VMEM((win + 2 * SUBLANES, 2 * bn), F32)],
        compiler_params=_params(("arbitrary",), 48),
        name="ffn_up",
    )(xn, rstd, wg, wv, cw, cw, cb, cb, *side_operands)


def _ffn_down(act, w, h, bm=512, side=None):
    return _residual_matmul("ffn_down", [act], [w], [lambda j: (0, j)], h,
                            bm=bm, epi_rows=16, vmem_mib=56, side=side)


def _ple_embed_kernel(p_ref, w_ref, g_ref, o_ref):
    e = jnp.dot(p_ref[...].astype(BF16), w_ref[...], preferred_element_type=F32)
    o_ref[...] = _rms(e, g_ref[...]).astype(o_ref.dtype)


def _ple_embed(p, w, g, bm=512):
    t, k = p.shape
    n = w.shape[1]
    return pl.pallas_call(
        _ple_embed_kernel,
        grid=(t // bm,),
        in_specs=[pl.BlockSpec((bm, k), lambda i: (i, 0)),
                  pl.BlockSpec((k, n), lambda i: (0, 0)),
                  pl.BlockSpec((1, n), lambda i: (0, 0))],
        out_specs=pl.BlockSpec((bm, n), lambda i: (i, 0)),
        out_shape=jax.ShapeDtypeStruct((t, n), BF16),
        compiler_params=_params(("parallel",), 40),
        name="ple_embed",
    )(p, w, g.reshape(1, n))


def _ple_gate_epilogue(chunk, r0, epi, outs, *, rows):
    rstd_ref, e_ref, h_ref = epi
    (o_ref,) = outs
    sl = slice(r0, r0 + rows)
    rstd = jnp.concatenate([rstd_ref[sl, :]] * (chunk.shape[1] // LANES), axis=1)
    out = h_ref[sl, :] + jax.nn.sigmoid(chunk * rstd) * e_ref[sl, :].astype(F32)
    o_ref[sl, :] = out
    return out[0:1, 0:LANES]


def _ple_gate(hb, rstd, w, e, h, bm=1024, bn=512, epi_rows=8):
    t = hb.shape[0]
    n = w.shape[1]
    block = ((bm, bn), lambda i, j: (i, j))
    return _pipelined_matmul(
        "ple_gate", [hb], [w], [lambda j: (0, j)],
        [rstd, e, h], [((bm, LANES), lambda i, j: (i, 0)), block, block],
        [block], [jax.ShapeDtypeStruct((t, n), F32)],
        functools.partial(_ple_gate_epilogue, rows=epi_rows),
        nj=n // bn, bm=bm, bn=bn, epi_rows=epi_rows, vmem_mib=48)[0]


def kernel(x, p, norm_mix_g, w_in, gmlp_v_g, gmlp_ws, gmlp_bs, q_norm_g, k_norm_g, na_rpb,
           out_norm_a_g, out_norm_b_g, w_out, norm_ffn_g, w_up, conv_w, conv_b, w_down,
           norm_ple_g, w_ple_gate, w_ple_proj, ple_post_g):
    batch, seq, d_model = x.shape
    t = batch * seq
    depth = w_in.shape[0]
    n_a = gmlp_ws.shape[1]
    n_b = na_rpb.shape[1]
    h = x.reshape(t, d_model)
    for i in range(depth):
        hn = _rmsnorm(h, norm_mix_g[i])
        w = w_in[i].astype(BF16)
        d_a, d_b = n_a * HEAD_DIM, n_b * HEAD_DIM
        d_ff = w_down.shape[1]
        g_ffn = jnp.broadcast_to(norm_ffn_g[i][:, None], (d_model, LANES))
        g_ple = jnp.broadcast_to(norm_ple_g[i][:, None], (d_model, LANES))
        zuv, w_gate = _inproj(hn, w, 0, 2 * d_a, "gelu",
                              side=(w_up[i], g_ffn, 64, d_ff, 0))
        zqk, w_val = _inproj(hn, w, 2 * d_a, 2 * d_b, "norm",
                             _na_qk_gain(q_norm_g[i], k_norm_g[i], n_b),
                             side=(w_up[i], g_ffn, 64, d_ff, 1))
        zv, w_o = _inproj(hn, w, 2 * d_a + 2 * d_b, d_b, "plain",
                          side=(w_out[i], None, 128, d_model, 0))
        mix_a = _gmlp(zuv, gmlp_ws[i], gmlp_bs[i], gmlp_v_g[i], out_norm_a_g[i], n_a)
        mix_b = _na(zqk, zv, _na_toeplitz(na_rpb[i]), out_norm_b_g[i], n_b, batch, seq)
        h, hb, ssq = _outproj(mix_a, mix_b, w_o, h)
        act, w_dn = _ffn_up(hb, _row_rstd(ssq, d_model), w_gate, w_val,
                            conv_w[i], conv_b[i].reshape(1, -1), seq,
                            side=(w_down[i], None, 32, d_model, 0))
        h, hb, ssq, w_pg = _ffn_down(act, w_dn, h, side=(w_ple_gate[i], g_ple, 16, d_model, 0))
        e = _ple_embed(p[i].reshape(t, -1), w_ple_proj[i].astype(BF16), ple_post_g[i])
        h = _ple_gate(hb, _row_rstd(ssq, d_model), w_pg, e, h)
    return h.reshape(batch, seq, d_model)
```

```python
import functools

import numpy as np
import jax
import jax.numpy as jnp
from jax import lax
from jax.experimental import pallas as pl
from jax.experimental.pallas import tpu as pltpu

F32 = jnp.float32
BF16 = jnp.bfloat16

LANES = 128
SUBLANES = 8

EPS = 1e-6
HEAD_DIM = 128
CHUNK = 128
GMLP_GROUP = 8
GRID_W = 64
WIN_ROWS = 8
WIN_COLS = 16
NA_G = 4
NA_WIN = NA_G + WIN_ROWS
MASKED = -1e30
NA_LOG2E = float(np.log2(np.e))
HALO = 16
EPI_ROWS = 8
SSQ_BN = 512
MIB = 1 << 20


def _params(semantics, vmem_mib):
    return pltpu.CompilerParams(dimension_semantics=semantics, vmem_limit_bytes=vmem_mib * MIB)


def _rms(x, g):
    ms = jnp.mean(x * x, axis=-1, keepdims=True)
    return x * lax.rsqrt(ms + EPS) * g


def _gelu(x):
    return 0.5 * x * (1.0 + lax.erf(x * np.float32(np.sqrt(0.5))))


def _rmsnorm_kernel(x_ref, g_ref, o_ref):
    o_ref[...] = _rms(x_ref[...], g_ref[...]).astype(o_ref.dtype)


def _rmsnorm(x, g, bm=512):
    t, d = x.shape
    return pl.pallas_call(
        _rmsnorm_kernel,
        grid=(t // bm,),
        in_specs=[pl.BlockSpec((bm, d), lambda i: (i, 0)),
                  pl.BlockSpec((1, d), lambda i: (0, 0))],
        out_specs=pl.BlockSpec((bm, d), lambda i: (i, 0)),
        out_shape=jax.ShapeDtypeStruct((t, d), BF16),
        compiler_params=_params(("parallel",), 40),
        name="rmsnorm",
    )(x, g.reshape(1, d))


def _dependent_zero(x):
    bits = pltpu.bitcast(x, jnp.uint32)
    return pltpu.bitcast((bits >> 16) >> 16, F32)


SIDE_COLS = 256


def _convert_weight_block(x_ref, g_ref, o_ref):
    g = None if g_ref is None else jnp.concatenate([g_ref[...]] * (SIDE_COLS // LANES), axis=1)
    for c in range(0, x_ref.shape[1], SIDE_COLS):
        blk = x_ref[:, c:c + SIDE_COLS]
        if g is not None:
            blk = blk * g
        o_ref[:, c:c + SIDE_COLS] = blk.astype(o_ref.dtype)


def _side_specs(side, n_steps):
    x, gain, rows, cols, cj = side
    n_side = x.shape[0] // rows
    assert n_side <= n_steps and x.shape[0] % rows == 0 and cols % SIDE_COLS == 0

    def blk(s):
        return jnp.minimum(s, n_side - 1), cj

    in_specs = [pl.BlockSpec((rows, cols), blk)]
    operands = [x]
    if gain is not None:
        in_specs.append(pl.BlockSpec((rows, LANES), lambda s: (blk(s)[0], 0)))
        operands.append(gain)
    out_spec = pl.BlockSpec((rows, cols), lambda s: (blk(s)[0], 0))
    out_shape = jax.ShapeDtypeStruct((x.shape[0], cols), BF16)
    return in_specs, operands, out_spec, out_shape


def _pipelined_matmul_kernel(*refs, n_lhs, n_epi, n_out, n_side, n_blocks, epilogue, bm, bn,
                             epi_rows):
    lhs = refs[:n_lhs]
    rhs = refs[n_lhs:2 * n_lhs]
    epi = refs[2 * n_lhs:2 * n_lhs + n_epi]
    side_in = refs[2 * n_lhs + n_epi:2 * n_lhs + n_epi + n_side]
    n_in = 2 * n_lhs + n_epi + n_side
    outs = refs[n_in:n_in + n_out]
    side_out = refs[n_in + n_out:-2]
    acc_a_ref, acc_b_ref = refs[-2:]
    s = pl.program_id(0)

    @pl.when(s == 0)
    def _():
        acc_b_ref[...] = jnp.zeros(acc_b_ref.shape, acc_b_ref.dtype)

    def step(new_ref, old_ref):
        zero = None
        for r0 in range(0, bm, epi_rows):
            chunk = old_ref[r0:r0 + epi_rows, :]
            if zero is not None:
                chunk = chunk + zero
            last = epilogue(chunk, r0, epi, outs)
            zero = jnp.concatenate([_dependent_zero(last)] * (bn // LANES), axis=1)
        if new_ref is None:
            return
        if side_in:
            _convert_weight_block(side_in[0], side_in[1] if n_side == 2 else None, side_out[0])
        acc = jnp.dot(lhs[0][...], rhs[0][...], preferred_element_type=F32)
        for a_ref, w_ref in zip(lhs[1:], rhs[1:]):
            acc = acc + jnp.dot(a_ref[...], w_ref[...], preferred_element_type=F32)
        new_ref[...] = acc

    @pl.when((s % 2 == 0) & (s < n_blocks))
    def _():
        step(acc_a_ref, acc_b_ref)

    @pl.when((s % 2 == 1) & (s < n_blocks))
    def _():
        step(acc_b_ref, acc_a_ref)

    @pl.when(s == n_blocks)
    def _():
        step(None, acc_a_ref if n_blocks % 2 == 1 else acc_b_ref)


def _pipelined_matmul(name, lhs, rhs, rhs_blocks, epi, epi_specs, out_specs, out_shapes, epilogue,
                      *, nj, bm, bn, epi_rows, vmem_mib, side=None):
    ni = lhs[0].shape[0] // bm
    n_blocks = ni * nj

    def cur_ij(s):
        return jnp.minimum(s // nj, ni - 1), s % nj

    def prev_ij(s):
        sp = jnp.maximum(s - 1, 0)
        return sp // nj, sp % nj

    def cur_row(s):
        return cur_ij(s)[0], 0

    in_specs = [pl.BlockSpec((bm, a.shape[1]), cur_row) for a in lhs]
    in_specs += [pl.BlockSpec((a.shape[1], bn),
                              functools.partial(lambda s, blk: blk(cur_ij(s)[1]), blk=blk))
                 for a, blk in zip(lhs, rhs_blocks)]
    in_specs += [pl.BlockSpec(shape, functools.partial(lambda s, fn: fn(*prev_ij(s)), fn=fn))
                 for shape, fn in epi_specs]
    out_block_specs = [pl.BlockSpec(shape, functools.partial(lambda s, fn: fn(*prev_ij(s)), fn=fn))
                       for shape, fn in out_specs]
    out_shapes = list(out_shapes)
    side_operands = []
    if side is not None:
        side_in_specs, side_operands, side_out_spec, side_out_shape = _side_specs(side, n_blocks)
        in_specs += side_in_specs
        out_block_specs.append(side_out_spec)
        out_shapes.append(side_out_shape)
    return pl.pallas_call(
        functools.partial(_pipelined_matmul_kernel, n_lhs=len(lhs), n_epi=len(epi),
                          n_out=len(out_specs), n_side=len(side_operands), n_blocks=n_blocks,
                          epilogue=epilogue, bm=bm, bn=bn, epi_rows=epi_rows),
        grid=(n_blocks + 1,),
        in_specs=in_specs,
        out_specs=out_block_specs,
        out_shape=out_shapes,
        scratch_shapes=[pltpu.VMEM((bm, bn), F32), pltpu.VMEM((bm, bn), F32)],
        compiler_params=_params(("arbitrary",), vmem_mib),
        name=name,
    )(*lhs, *rhs, *epi, *side_operands)


def _heads_epilogue(chunk, r0, epi, outs, *, rows, kind):
    (o_ref,) = outs
    y = None
    for hh in range(chunk.shape[1] // HEAD_DIM):
        cols = slice(hh * HEAD_DIM, (hh + 1) * HEAD_DIM)
        y = chunk[:, cols]
        if kind == "gelu":
            y = _gelu(y)
        elif kind == "norm":
            y = _rms(y, epi[0][:, cols])
        o_ref[hh, r0:r0 + rows, :] = y.astype(o_ref.dtype)
    return y[0:1, :]


def _inproj(hn, w, col0, n_cols, kind, gain=None, bm=1024, bn=1024, side=None):
    t = hn.shape[0]
    hpb = bn // HEAD_DIM
    epi_rows = {"norm": 32, "gelu": 8, "plain": 16}[kind]
    epi, epi_specs = [], []
    if kind == "norm":
        epi, epi_specs = [gain], [((1, bn), lambda i, j: (0, j))]
    return _pipelined_matmul(
        "inproj_" + kind, [hn], [w], [lambda j: (0, col0 // bn + j)], epi, epi_specs,
        [((hpb, bm, HEAD_DIM), lambda i, j: (j, i, 0))],
        [jax.ShapeDtypeStruct((n_cols // HEAD_DIM, t, HEAD_DIM), BF16)],
        functools.partial(_heads_epilogue, rows=epi_rows, kind=kind),
        nj=n_cols // bn, bm=bm, bn=bn, epi_rows=epi_rows, vmem_mib=62, side=side)


def _gmlp_kernel(u_ref, v_ref, ws_ref, bs_ref, gv_ref, og_ref, o_ref, *, n_groups):
    ws = ws_ref[0]
    bs = bs_ref[0]
    gv = gv_ref[0]
    og = og_ref[0]

    def group_body(gi, carry):
        base = pl.multiple_of(gi * (GMLP_GROUP * CHUNK), GMLP_GROUP * CHUNK)
        for c in range(GMLP_GROUP):
            sl = pl.ds(base + c * CHUNK, CHUNK)
            vn = _rms(v_ref[0, sl, :].astype(F32), gv).astype(BF16)
            mixed = jnp.dot(ws, vn, preferred_element_type=F32) + bs
            a = u_ref[0, sl, :].astype(F32) * mixed
            o_ref[sl, :] = _rms(a, og).astype(o_ref.dtype)
        return carry

    lax.fori_loop(0, n_groups, group_body, 0)


def _gmlp(z, ws, bs, gv, og, n_heads, tb=8192):
    t = z.shape[1]
    return pl.pallas_call(
        functools.partial(_gmlp_kernel, n_groups=tb // (GMLP_GROUP * CHUNK)),
        grid=(n_heads, t // tb),
        in_specs=[pl.BlockSpec((1, tb, HEAD_DIM), lambda h, i: (h, i, 0)),
                  pl.BlockSpec((1, tb, HEAD_DIM), lambda h, i: (n_heads + h, i, 0)),
                  pl.BlockSpec((1, CHUNK, CHUNK), lambda h, i: (h, 0, 0)),
                  pl.BlockSpec((1, CHUNK, 1), lambda h, i: (h, 0, 0)),
                  pl.BlockSpec((1, 1, HEAD_DIM), lambda h, i: (h, 0, 0)),
                  pl.BlockSpec((1, 1, HEAD_DIM), lambda h, i: (h, 0, 0))],
        out_specs=pl.BlockSpec((tb, HEAD_DIM), lambda h, i: (i, h)),
        out_shape=jax.ShapeDtypeStruct((t, n_heads * HEAD_DIM), BF16),
        compiler_params=_params(("parallel", "parallel"), 24),
        name="gmlp",
    )(z, z, ws.astype(BF16), bs.reshape(n_heads, CHUNK, 1),
      gv.reshape(n_heads, 1, HEAD_DIM), og.reshape(n_heads, 1, HEAD_DIM))


def _na_toeplitz(rpb):
    col = np.arange(GRID_W)
    cs = np.clip(col - WIN_COLS // 2, 0, GRID_W - WIN_COLS)
    kc = np.arange(GRID_W)
    valid = (kc[None, :] >= cs[:, None]) & (kc[None, :] < cs[:, None] + WIN_COLS)
    dc = kc[None, :] - col[:, None] + (WIN_COLS - 1)
    onehot = (dc[:, :, None] == np.arange(2 * WIN_COLS - 1)).astype(np.float32)
    t = jnp.einsum('hrd,ckd->hrck', rpb, onehot, precision=lax.Precision.HIGHEST)
    return jnp.where(valid[None, None], t, MASKED)


def _na_group_geometry(rows):
    def geom(g):
        r0 = g * NA_G
        ws = int(np.clip(r0 - WIN_ROWS // 2, 0, rows - NA_WIN))
        rs = [int(np.clip(r0 + i - WIN_ROWS // 2, 0, rows - WIN_ROWS)) for i in range(NA_G)]
        return r0 - ws, tuple(r - ws for r in rs)

    ng = rows // NA_G
    assert ng >= 3 and all(geom(g) == geom(1) for g in range(1, ng - 1))
    return geom(0), geom(1), geom(ng - 1)


def _na_kernel(q_ref, k_ref, v_ref, t_ref, og_ref, o_ref, bias_ref, s_a_ref, s_b_ref, *, rows):
    log2e = np.float32(NA_LOG2E)
    og = og_ref[0]

    masked_tile = jnp.full((GRID_W, GRID_W), MASKED, F32)
    for cls, (roff, rsoff) in enumerate(_na_group_geometry(rows)):
        for ri in range(NA_G):
            for kp in range(NA_WIN // 2):
                pair = []
                for kr in (2 * kp, 2 * kp + 1):
                    in_window = 0 <= kr - rsoff[ri] < WIN_ROWS
                    dr = kr - roff - ri + (WIN_ROWS - 1)
                    pair.append(t_ref[0, dr] * log2e if in_window else masked_tile)
                bias_ref[cls, ri * GRID_W:(ri + 1) * GRID_W, kp * LANES:(kp + 1) * LANES] = (
                    jnp.concatenate(pair, axis=1))

    ng = rows // NA_G
    nq = NA_G * GRID_W
    nk = NA_WIN * GRID_W

    def slices(g):
        r0 = g * NA_G
        ws = jnp.clip(r0 - WIN_ROWS // 2, 0, rows - NA_WIN)
        return (pl.ds(pl.multiple_of(r0 * GRID_W, nq), nq),
                pl.ds(pl.multiple_of(ws * GRID_W, nq), nk))

    def scores(g, s_ref):
        g = jnp.minimum(g, ng - 1)
        cls = jnp.minimum(g, 1) + jnp.maximum(g - (ng - 2), 0)
        qsl, ksl = slices(g)
        s = lax.dot_general(q_ref[0, qsl, :], k_ref[0, ksl, :], (((1,), (1,)), ((), ())),
                            preferred_element_type=F32)
        s_ref[...] = s + bias_ref[cls]

    def attend(g, s_ref):
        qsl, ksl = slices(g)
        s = s_ref[...]
        m = jnp.max(s, axis=-1, keepdims=True)
        e = jnp.exp2(s - m)
        l = jnp.sum(e, axis=-1, keepdims=True)
        o = jnp.dot(e.astype(BF16), v_ref[0, ksl, :], preferred_element_type=F32) / l
        o_ref[qsl, :] = _rms(o, og).astype(o_ref.dtype)

    scores(0, s_a_ref)

    def pair_body(tp, carry):
        g = 2 * tp
        scores(g + 1, s_b_ref)
        attend(g, s_a_ref)
        scores(g + 2, s_a_ref)
        attend(g + 1, s_b_ref)
        return carry

    lax.fori_loop(0, ng // 2, pair_body, 0, unroll=2)


def _na_qk_gain(qg, kg, n_heads):
    qg = qg * np.float32(HEAD_DIM ** -0.5 * NA_LOG2E)
    return jnp.concatenate([jnp.tile(qg, n_heads), jnp.tile(kg, n_heads)]).reshape(1, -1)


def _na(zqk, zv, toeplitz, og, n_heads, batch, seq):
    rows = seq // GRID_W
    assert rows % (2 * NA_G) == 0 and (WIN_ROWS // 2) % NA_G == 0
    n_dr = 2 * WIN_ROWS - 1
    return pl.pallas_call(
        functools.partial(_na_kernel, rows=rows),
        grid=(batch, n_heads),
        in_specs=[pl.BlockSpec((1, seq, HEAD_DIM), lambda b, h: (h, b, 0)),
                  pl.BlockSpec((1, seq, HEAD_DIM), lambda b, h: (n_heads + h, b, 0)),
                  pl.BlockSpec((1, seq, HEAD_DIM), lambda b, h: (h, b, 0)),
                  pl.BlockSpec((1, n_dr, GRID_W, GRID_W), lambda b, h: (h, 0, 0, 0)),
                  pl.BlockSpec((1, 1, HEAD_DIM), lambda b, h: (h, 0, 0))],
        out_specs=pl.BlockSpec((seq, HEAD_DIM), lambda b, h: (b, h)),
        out_shape=jax.ShapeDtypeStruct((batch * seq, n_heads * HEAD_DIM), BF16),
        scratch_shapes=[pltpu.VMEM((3, NA_G * GRID_W, NA_WIN * GRID_W), F32),
                        pltpu.VMEM((NA_G * GRID_W, NA_WIN * GRID_W), F32),
                        pltpu.VMEM((NA_G * GRID_W, NA_WIN * GRID_W), F32)],
        compiler_params=_params(("parallel", "parallel"), 40),
        name="natten",
    )(zqk, zqk, zv, toeplitz, og.reshape(n_heads, 1, HEAD_DIM))


def _residual_epilogue(chunk, r0, epi, outs, *, rows):
    (h_ref,) = epi
    o_ref, ob_ref, ssq_ref = outs
    sl = slice(r0, r0 + rows)
    h = h_ref[sl, :] + chunk
    o_ref[sl, :] = h
    ob_ref[sl, :] = h.astype(ob_ref.dtype)
    ssq = jnp.broadcast_to(jnp.sum(h * h, axis=-1, keepdims=True), (rows, LANES))
    ssq_ref[sl, :] = ssq
    return ssq[0:1, :]


def _residual_matmul(name, lhs, rhs, rhs_blocks, h, *, bm, epi_rows, vmem_mib, side=None):
    t, n = h.shape
    bn = SSQ_BN
    block = ((bm, bn), lambda i, j: (i, j))
    return _pipelined_matmul(
        name, lhs, rhs, rhs_blocks, [h], [block],
        [block, block, ((bm, LANES), lambda i, j: (i, j))],
        [jax.ShapeDtypeStruct((t, n), F32), jax.ShapeDtypeStruct((t, n), BF16),
         jax.ShapeDtypeStruct((t, n // bn * LANES), F32)],
        functools.partial(_residual_epilogue, rows=epi_rows),
        nj=n // bn, bm=bm, bn=bn, epi_rows=epi_rows, vmem_mib=vmem_mib, side=side)


def _row_rstd_kernel(s_ref, o_ref, *, d):
    tot = s_ref[:, 0:LANES]
    for c in range(1, s_ref.shape[1] // LANES):
        tot = tot + s_ref[:, c * LANES:(c + 1) * LANES]
    o_ref[...] = lax.rsqrt(tot * np.float32(1.0 / d) + EPS)


def _row_rstd(ssq, d, bm=2048):
    t, sw = ssq.shape
    return pl.pallas_call(
        functools.partial(_row_rstd_kernel, d=d),
        grid=(t // bm,),
        in_specs=[pl.BlockSpec((bm, sw), lambda i: (i, 0))],
        out_specs=pl.BlockSpec((bm, LANES), lambda i: (i, 0)),
        out_shape=jax.ShapeDtypeStruct((t, LANES), F32),
        compiler_params=_params(("parallel",), 32),
        name="row_rstd",
    )(ssq)


def _outproj(mix_a, mix_b, w, x, bm=1024):
    assert mix_a.shape[1] == mix_b.shape[1]
    return _residual_matmul("outproj", [mix_a, mix_b], [w, w],
                            [lambda j: (0, j), lambda j: (1, j)], x,
                            bm=bm, epi_rows=32, vmem_mib=48)


def _ffn_up_window_start(i, bm, t):
    return pl.multiple_of(jnp.clip(i * bm - HALO, 0, t - (bm + 2 * HALO)), HALO)


def _ffn_up_kernel(x_ref, r_ref, wg_ref, wv_ref, cwg_ref, cwv_ref, cbg_ref, cbv_ref, side_ref,
                   o_ref, side_o_ref, rstd_ref, up_a_ref, up_b_ref, *, bm, bn, nj, ni, seq):
    s = pl.program_id(0)
    win = bm + 2 * HALO
    t = ni * bm
    i = jnp.minimum(s // nj, ni - 1)
    start = _ffn_up_window_start(i, bm, t)
    ip = jnp.maximum(s - 1, 0) // nj
    off_prev = pl.multiple_of(ip * bm - _ffn_up_window_start(ip, bm, t), HALO)

    @pl.when((s % nj == 0) & (s < ni * nj))
    def _():
        tok = start + lax.broadcasted_iota(jnp.int32, (win, LANES), 0)
        seq_start = (i * bm) // seq * seq
        same_seq = (tok >= seq_start) & (tok < seq_start + seq)
        rstd_ref[...] = jnp.where(same_seq, r_ref[...], 0.0)

    @pl.when(s == 0)
    def _():
        up_a_ref[...] = jnp.zeros(up_a_ref.shape, up_a_ref.dtype)
        up_b_ref[...] = jnp.zeros(up_b_ref.shape, up_b_ref.dtype)

    def step(new_ref, old_ref):
        cw = jnp.concatenate([cwg_ref[...], cwv_ref[...]], axis=1)
        cb = jnp.concatenate([cbg_ref[...], cbv_ref[...]], axis=1)
        w_cur = cw[1:2, :]
        for r0 in range(0, bm, EPI_ROWS):
            rows = EPI_ROWS + 2 * SUBLANES
            mid = slice(SUBLANES, SUBLANES + EPI_ROWS)
            slab = old_ref[pl.ds(pl.multiple_of(off_prev + r0, SUBLANES), rows), :]
            prev = pltpu.roll(slab, 1, 0)[mid]
            cur = slab[mid]
            nxt = pltpu.roll(slab, rows - 1, 0)[mid]
            c = cb + (prev * cw[0:1, :] + cur * w_cur + nxt * cw[2:3, :])
            act = _gelu(c[:, :bn]) * c[:, bn:]
            o_ref[r0:r0 + EPI_ROWS, :] = act.astype(o_ref.dtype)
            zero = _dependent_zero(act[0:1, 0:LANES])
            w_cur = cw[1:2, :] + jnp.concatenate([zero] * (2 * bn // LANES), axis=1)

        if new_ref is None:
            return
        _convert_weight_block(side_ref, None, side_o_ref)

        half = win // 2
        for h0 in (0, half):
            xh = x_ref[h0:h0 + half, :]
            rstd = jnp.concatenate([rstd_ref[h0:h0 + half, :]] * (bn // LANES), axis=1)
            dst = slice(SUBLANES + h0, SUBLANES + h0 + half)
            new_ref[dst, :bn] = jnp.dot(xh, wg_ref[...], preferred_element_type=F32) * rstd
            new_ref[dst, bn:] = jnp.dot(xh, wv_ref[...], preferred_element_type=F32) * rstd

    @pl.when((s % 2 == 0) & (s < ni * nj))
    def _():
        step(up_a_ref, up_b_ref)

    @pl.when((s % 2 == 1) & (s < ni * nj))
    def _():
        step(up_b_ref, up_a_ref)

    @pl.when(s == ni * nj)
    def _():
        step(None, up_a_ref if (ni * nj) % 2 == 1 else up_b_ref)


def _ffn_up(xn, rstd, wg, wv, cw, cb, seq, side, bm=2048, bn=256):
    t, k = xn.shape
    n = wg.shape[1]
    ni, nj = t // bm, n // bn
    win = bm + 2 * HALO
    assert seq % bm == 0 and t >= win and win % (2 * HALO) == 0
    side_in_specs, side_operands, side_out_spec, side_out_shape = _side_specs(side, ni * nj)
    assert len(side_operands) == 1

    def window(s):
        return _ffn_up_window_start(jnp.minimum(s // nj, ni - 1), bm, t), 0

    def prev_ij(s):
        sp = jnp.maximum(s - 1, 0)
        return sp // nj, sp % nj

    return pl.pallas_call(
        functools.partial(_ffn_up_kernel, bm=bm, bn=bn, nj=nj, ni=ni, seq=seq),
        grid=(ni * nj + 1,),
        in_specs=[pl.BlockSpec((pl.Element(win), pl.Element(k)), window),
                  pl.BlockSpec((pl.Element(win), pl.Element(LANES)), window),
                  pl.BlockSpec((k, bn), lambda s: (0, s % nj)),
                  pl.BlockSpec((k, bn), lambda s: (0, s % nj)),
                  pl.BlockSpec((3, bn), lambda s: (0, prev_ij(s)[1])),
                  pl.BlockSpec((3, bn), lambda s: (0, nj + prev_ij(s)[1])),
                  pl.BlockSpec((1, bn), lambda s: (0, prev_ij(s)[1])),
                  pl.BlockSpec((1, bn), lambda s: (0, nj + prev_ij(s)[1]))] + side_in_specs,
        out_specs=[pl.BlockSpec((bm, bn), lambda s: prev_ij(s)), side_out_spec],
        out_shape=[jax.ShapeDtypeStruct((t, n), BF16), side_out_shape],
        scratch_shapes=[pltpu.VMEM((win, LANES), F32),
                        pltpu.VMEM((win + 2 * SUBLANES, 2 * bn), F32),
                        pltpu.VMEM((win + 2 * SUBLANES, 2 * bn), F32)],
        compiler_params=_params(("arbitrary",), 60),
        name="ffn_up",
    )(xn, rstd, wg, wv, cw, cw, cb, cb, *side_operands)


def _ffn_down(act, w, h, bm=512, side=None):
    return _residual_matmul("ffn_down", [act], [w], [lambda j: (0, j)], h,
                            bm=bm, epi_rows=16, vmem_mib=56, side=side)


def _ple_embed_kernel(p_ref, w_ref, g_ref, o_ref):
    e = jnp.dot(p_ref[...].astype(BF16), w_ref[...], preferred_element_type=F32)
    o_ref[...] = _rms(e, g_ref[...]).astype(o_ref.dtype)


def _ple_embed(p, w, g, bm=512):
    t, k = p.shape
    n = w.shape[1]
    return pl.pallas_call(
        _ple_embed_kernel,
        grid=(t // bm,),
        in_specs=[pl.BlockSpec((bm, k), lambda i: (i, 0)),
                  pl.BlockSpec((k, n), lambda i: (0, 0)),
                  pl.BlockSpec((1, n), lambda i: (0, 0))],
        out_specs=pl.BlockSpec((bm, n), lambda i: (i, 0)),
        out_shape=jax.ShapeDtypeStruct((t, n), BF16),
        compiler_params=_params(("parallel",), 40),
        name="ple_embed",
    )(p, w, g.reshape(1, n))


def _ple_gate_epilogue(chunk, r0, epi, outs, *, rows):
    rstd_ref, e_ref, h_ref = epi
    (o_ref,) = outs
    sl = slice(r0, r0 + rows)
    rstd = jnp.concatenate([rstd_ref[sl, :]] * (chunk.shape[1] // LANES), axis=1)
    out = h_ref[sl, :] + jax.nn.sigmoid(chunk * rstd) * e_ref[sl, :].astype(F32)
    o_ref[sl, :] = out
    return out[0:1, 0:LANES]


def _ple_gate(hb, rstd, w, e, h, bm=1024, bn=512, epi_rows=8):
    t = hb.shape[0]
    n = w.shape[1]
    block = ((bm, bn), lambda i, j: (i, j))
    return _pipelined_matmul(
        "ple_gate", [hb], [w], [lambda j: (0, j)],
        [rstd, e, h], [((bm, LANES), lambda i, j: (i, 0)), block, block],
        [block], [jax.ShapeDtypeStruct((t, n), F32)],
        functools.partial(_ple_gate_epilogue, rows=epi_rows),
        nj=n // bn, bm=bm, bn=bn, epi_rows=epi_rows, vmem_mib=48)[0]


def kernel(x, p, norm_mix_g, w_in, gmlp_v_g, gmlp_ws, gmlp_bs, q_norm_g, k_norm_g, na_rpb,
           out_norm_a_g, out_norm_b_g, w_out, norm_ffn_g, w_up, conv_w, conv_b, w_down,
           norm_ple_g, w_ple_gate, w_ple_proj, ple_post_g):
    batch, seq, d_model = x.shape
    t = batch * seq
    depth = w_in.shape[0]
    n_a = gmlp_ws.shape[1]
    n_b = na_rpb.shape[1]
    h = x.reshape(t, d_model)
    for i in range(depth):
        hn = _rmsnorm(h, norm_mix_g[i])
        w = w_in[i].astype(BF16)
        d_a, d_b = n_a * HEAD_DIM, n_b * HEAD_DIM
        d_ff = w_down.shape[1]
        g_ffn = jnp.broadcast_to(norm_ffn_g[i][:, None], (d_model, LANES))
        g_ple = jnp.broadcast_to(norm_ple_g[i][:, None], (d_model, LANES))
        zuv, w_gate = _inproj(hn, w, 0, 2 * d_a, "gelu",
                              side=(w_up[i], g_ffn, 64, d_ff, 0))
        zqk, w_val = _inproj(hn, w, 2 * d_a, 2 * d_b, "norm",
                             _na_qk_gain(q_norm_g[i], k_norm_g[i], n_b),
                             side=(w_up[i], g_ffn, 64, d_ff, 1))
        zv, w_o = _inproj(hn, w, 2 * d_a + 2 * d_b, d_b, "plain",
                          side=(w_out[i], None, 128, d_model, 0))
        mix_a = _gmlp(zuv, gmlp_ws[i], gmlp_bs[i], gmlp_v_g[i], out_norm_a_g[i], n_a)
        mix_b = _na(zqk, zv, _na_toeplitz(na_rpb[i]), out_norm_b_g[i], n_b, batch, seq)
        h, hb, ssq = _outproj(mix_a, mix_b, w_o, h)
        act, w_dn = _ffn_up(hb, _row_rstd(ssq, d_model), w_gate, w_val,
                            conv_w[i], conv_b[i].reshape(1, -1), seq,
                            side=(w_down[i], None, 32, d_model, 0))
        h, hb, ssq, w_pg = _ffn_down(act, w_dn, h, side=(w_ple_gate[i], g_ple, 16, d_model, 0))
        e = _ple_embed(p[i].reshape(t, -1), w_ple_proj[i].astype(BF16), ple_post_g[i])
        h = _ple_gate(hb, _row_rstd(ssq, d_model), w_pg, e, h)
    return h.reshape(batch, seq, d_model)
```

```python
import functools

import numpy as np
import jax
import jax.numpy as jnp
from jax import lax
from jax.experimental import pallas as pl
from jax.experimental.pallas import tpu as pltpu

F32 = jnp.float32
BF16 = jnp.bfloat16

LANES = 128
SUBLANES = 8

EPS = 1e-6
HEAD_DIM = 128
CHUNK = 128
GMLP_GROUP = 8
GRID_W = 64
WIN_ROWS = 8
WIN_COLS = 16
NA_G = 4
NA_WIN = NA_G + WIN_ROWS
MASKED = -1e30
NA_LOG2E = float(np.log2(np.e))
HALO = 16
EPI_ROWS = 8
RES_BN = 512
MIB = 1 << 20


def _params(semantics, vmem_mib):
    return pltpu.CompilerParams(dimension_semantics=semantics, vmem_limit_bytes=vmem_mib * MIB)


def _rms(x, g):
    ms = jnp.mean(x * x, axis=-1, keepdims=True)
    return x * lax.rsqrt(ms + EPS) * g


def _gelu(x):
    return 0.5 * x * (1.0 + lax.erf(x * np.float32(np.sqrt(0.5))))


def _rmsnorm_kernel(x_ref, g_ref, o_ref):
    o_ref[...] = _rms(x_ref[...], g_ref[...]).astype(o_ref.dtype)


def _rmsnorm(x, g, bm=512):
    t, d = x.shape
    return pl.pallas_call(
        _rmsnorm_kernel,
        grid=(t // bm,),
        in_specs=[pl.BlockSpec((bm, d), lambda i: (i, 0)),
                  pl.BlockSpec((1, d), lambda i: (0, 0))],
        out_specs=pl.BlockSpec((bm, d), lambda i: (i, 0)),
        out_shape=jax.ShapeDtypeStruct((t, d), BF16),
        compiler_params=_params(("parallel",), 40),
        name="rmsnorm",
    )(x, g.reshape(1, d))


def _dependent_zero(x):
    bits = pltpu.bitcast(x, jnp.uint32)
    return pltpu.bitcast((bits >> 16) >> 16, F32)


SIDE_COLS = 256


def _convert_weight_block(x_ref, g_ref, o_ref):
    g = None if g_ref is None else jnp.concatenate([g_ref[...]] * (SIDE_COLS // LANES), axis=1)
    for c in range(0, x_ref.shape[1], SIDE_COLS):
        blk = x_ref[:, c:c + SIDE_COLS]
        if g is not None:
            blk = blk * g
        o_ref[:, c:c + SIDE_COLS] = blk.astype(o_ref.dtype)


def _side_specs(side, n_steps):
    x, gain, rows, cols, cj = side
    n_side = x.shape[0] // rows
    assert n_side <= n_steps and x.shape[0] % rows == 0 and cols % SIDE_COLS == 0

    def blk(s):
        return jnp.minimum(s, n_side - 1), cj

    in_specs = [pl.BlockSpec((rows, cols), blk)]
    operands = [x]
    if gain is not None:
        in_specs.append(pl.BlockSpec((rows, LANES), lambda s: (blk(s)[0], 0)))
        operands.append(gain)
    out_spec = pl.BlockSpec((rows, cols), lambda s: (blk(s)[0], 0))
    out_shape = jax.ShapeDtypeStruct((x.shape[0], cols), BF16)
    return in_specs, operands, out_spec, out_shape


def _pipelined_matmul_kernel(*refs, n_lhs, n_epi, n_out, n_side, n_tmp, n_blocks, epilogue, bm, bn,
                             epi_rows):
    lhs = refs[:n_lhs]
    rhs = refs[n_lhs:2 * n_lhs]
    epi = refs[2 * n_lhs:2 * n_lhs + n_epi]
    side_in = refs[2 * n_lhs + n_epi:2 * n_lhs + n_epi + n_side]
    n_in = 2 * n_lhs + n_epi + n_side
    outs = refs[n_in:n_in + n_out]
    side_out = refs[n_in + n_out:len(refs) - 2 - n_tmp]
    tmp = refs[len(refs) - 2 - n_tmp:-2]
    acc_a_ref, acc_b_ref = refs[-2:]
    s = pl.program_id(0)

    @pl.when(s == 0)
    def _():
        acc_b_ref[...] = jnp.zeros(acc_b_ref.shape, acc_b_ref.dtype)
        for t_ref in tmp:
            t_ref[...] = jnp.zeros(t_ref.shape, t_ref.dtype)

    def step(new_ref, old_ref):
        zero = None
        for r0 in range(0, bm, epi_rows):
            chunk = old_ref[r0:r0 + epi_rows, :]
            if zero is not None:
                chunk = chunk + zero
            last = epilogue(chunk, r0, epi, outs, tmp)
            zero = jnp.concatenate([_dependent_zero(last)] * (bn // LANES), axis=1)
        if new_ref is None:
            return
        if side_in:
            _convert_weight_block(side_in[0], side_in[1] if n_side == 2 else None, side_out[0])
        acc = jnp.dot(lhs[0][...], rhs[0][...], preferred_element_type=F32)
        for a_ref, w_ref in zip(lhs[1:], rhs[1:]):
            acc = acc + jnp.dot(a_ref[...], w_ref[...], preferred_element_type=F32)
        new_ref[...] = acc

    @pl.when((s % 2 == 0) & (s < n_blocks))
    def _():
        step(acc_a_ref, acc_b_ref)

    @pl.when((s % 2 == 1) & (s < n_blocks))
    def _():
        step(acc_b_ref, acc_a_ref)

    @pl.when(s == n_blocks)
    def _():
        step(None, acc_a_ref if n_blocks % 2 == 1 else acc_b_ref)


def _pipelined_matmul(name, lhs, rhs, rhs_blocks, epi, epi_specs, out_specs, out_shapes, epilogue,
                      *, nj, bm, bn, epi_rows, vmem_mib, side=None, tmp_shapes=()):
    ni = lhs[0].shape[0] // bm
    n_blocks = ni * nj

    def cur_ij(s):
        return jnp.minimum(s // nj, ni - 1), s % nj

    def prev_ij(s):
        sp = jnp.maximum(s - 1, 0)
        return sp // nj, sp % nj

    def cur_row(s):
        return cur_ij(s)[0], 0

    in_specs = [pl.BlockSpec((bm, a.shape[1]), cur_row) for a in lhs]
    in_specs += [pl.BlockSpec((a.shape[1], bn),
                              functools.partial(lambda s, blk: blk(cur_ij(s)[1]), blk=blk))
                 for a, blk in zip(lhs, rhs_blocks)]
    in_specs += [pl.BlockSpec(shape, functools.partial(lambda s, fn: fn(*prev_ij(s)), fn=fn))
                 for shape, fn in epi_specs]
    out_block_specs = [pl.BlockSpec(shape, functools.partial(lambda s, fn: fn(*prev_ij(s)), fn=fn))
                       for shape, fn in out_specs]
    out_shapes = list(out_shapes)
    side_operands = []
    if side is not None:
        side_in_specs, side_operands, side_out_spec, side_out_shape = _side_specs(side, n_blocks)
        in_specs += side_in_specs
        out_block_specs.append(side_out_spec)
        out_shapes.append(side_out_shape)
    return pl.pallas_call(
        functools.partial(_pipelined_matmul_kernel, n_lhs=len(lhs), n_epi=len(epi),
                          n_out=len(out_specs), n_side=len(side_operands),
                          n_tmp=len(tmp_shapes), n_blocks=n_blocks,
                          epilogue=epilogue, bm=bm, bn=bn, epi_rows=epi_rows),
        grid=(n_blocks + 1,),
        in_specs=in_specs,
        out_specs=out_block_specs,
        out_shape=out_shapes,
        scratch_shapes=[pltpu.VMEM(shape, F32) for shape in tmp_shapes]
        + [pltpu.VMEM((bm, bn), F32), pltpu.VMEM((bm, bn), F32)],
        compiler_params=_params(("arbitrary",), vmem_mib),
        name=name,
    )(*lhs, *rhs, *epi, *side_operands)


def _heads_epilogue(chunk, r0, epi, outs, tmp, *, rows, kind):
    (o_ref,) = outs
    y = None
    for hh in range(chunk.shape[1] // HEAD_DIM):
        cols = slice(hh * HEAD_DIM, (hh + 1) * HEAD_DIM)
        y = chunk[:, cols]
        if kind == "gelu":
            y = _gelu(y)
        elif kind == "norm":
            y = _rms(y, epi[0][:, cols])
        o_ref[hh, r0:r0 + rows, :] = y.astype(o_ref.dtype)
    return y[0:1, :]


def _inproj(hn, w, col0, n_cols, kind, gain=None, bm=1024, bn=1024, side=None):
    t = hn.shape[0]
    hpb = bn // HEAD_DIM
    epi_rows = {"norm": 32, "gelu": 8, "plain": 16}[kind]
    epi, epi_specs = [], []
    if kind == "norm":
        epi, epi_specs = [gain], [((1, bn), lambda i, j: (0, j))]
    return _pipelined_matmul(
        "inproj_" + kind, [hn], [w], [lambda j: (0, col0 // bn + j)], epi, epi_specs,
        [((hpb, bm, HEAD_DIM), lambda i, j: (j, i, 0))],
        [jax.ShapeDtypeStruct((n_cols // HEAD_DIM, t, HEAD_DIM), BF16)],
        functools.partial(_heads_epilogue, rows=epi_rows, kind=kind),
        nj=n_cols // bn, bm=bm, bn=bn, epi_rows=epi_rows, vmem_mib=62, side=side)


def _gmlp_kernel(u_ref, v_ref, ws_ref, bs_ref, gv_ref, og_ref, o_ref, *, n_groups):
    ws = ws_ref[0]
    bs = bs_ref[0]
    gv = gv_ref[0]
    og = og_ref[0]

    def group_body(gi, carry):
        base = pl.multiple_of(gi * (GMLP_GROUP * CHUNK), GMLP_GROUP * CHUNK)
        for c in range(GMLP_GROUP):
            sl = pl.ds(base + c * CHUNK, CHUNK)
            vn = _rms(v_ref[0, sl, :].astype(F32), gv).astype(BF16)
            mixed = jnp.dot(ws, vn, preferred_element_type=F32) + bs
            a = u_ref[0, sl, :].astype(F32) * mixed
            o_ref[sl, :] = _rms(a, og).astype(o_ref.dtype)
        return carry

    lax.fori_loop(0, n_groups, group_body, 0)


def _gmlp(z, ws, bs, gv, og, n_heads, tb=8192):
    t = z.shape[1]
    return pl.pallas_call(
        functools.partial(_gmlp_kernel, n_groups=tb // (GMLP_GROUP * CHUNK)),
        grid=(n_heads, t // tb),
        in_specs=[pl.BlockSpec((1, tb, HEAD_DIM), lambda h, i: (h, i, 0)),
                  pl.BlockSpec((1, tb, HEAD_DIM), lambda h, i: (n_heads + h, i, 0)),
                  pl.BlockSpec((1, CHUNK, CHUNK), lambda h, i: (h, 0, 0)),
                  pl.BlockSpec((1, CHUNK, 1), lambda h, i: (h, 0, 0)),
                  pl.BlockSpec((1, 1, HEAD_DIM), lambda h, i: (h, 0, 0)),
                  pl.BlockSpec((1, 1, HEAD_DIM), lambda h, i: (h, 0, 0))],
        out_specs=pl.BlockSpec((tb, HEAD_DIM), lambda h, i: (i, h)),
        out_shape=jax.ShapeDtypeStruct((t, n_heads * HEAD_DIM), BF16),
        compiler_params=_params(("parallel", "parallel"), 24),
        name="gmlp",
    )(z, z, ws.astype(BF16), bs.reshape(n_heads, CHUNK, 1),
      gv.reshape(n_heads, 1, HEAD_DIM), og.reshape(n_heads, 1, HEAD_DIM))


def _na_toeplitz(rpb):
    col = np.arange(GRID_W)
    cs = np.clip(col - WIN_COLS // 2, 0, GRID_W - WIN_COLS)
    kc = np.arange(GRID_W)
    valid = (kc[None, :] >= cs[:, None]) & (kc[None, :] < cs[:, None] + WIN_COLS)
    dc = kc[None, :] - col[:, None] + (WIN_COLS - 1)
    onehot = (dc[:, :, None] == np.arange(2 * WIN_COLS - 1)).astype(np.float32)
    t = jnp.einsum('hrd,ckd->hrck', rpb, onehot, precision=lax.Precision.HIGHEST)
    return jnp.where(valid[None, None], t, MASKED)


def _na_group_geometry(rows):
    def geom(g):
        r0 = g * NA_G
        ws = int(np.clip(r0 - WIN_ROWS // 2, 0, rows - NA_WIN))
        rs = [int(np.clip(r0 + i - WIN_ROWS // 2, 0, rows - WIN_ROWS)) for i in range(NA_G)]
        return r0 - ws, tuple(r - ws for r in rs)

    ng = rows // NA_G
    assert ng >= 3 and all(geom(g) == geom(1) for g in range(1, ng - 1))
    return geom(0), geom(1), geom(ng - 1)


def _na_kernel(q_ref, k_ref, v_ref, t_ref, og_ref, o_ref, bias_ref, s_a_ref, s_b_ref, *, rows):
    log2e = np.float32(NA_LOG2E)
    og = og_ref[0]

    masked_tile = jnp.full((GRID_W, GRID_W), MASKED, F32)
    for cls, (roff, rsoff) in enumerate(_na_group_geometry(rows)):
        for ri in range(NA_G):
            for kp in range(NA_WIN // 2):
                pair = []
                for kr in (2 * kp, 2 * kp + 1):
                    in_window = 0 <= kr - rsoff[ri] < WIN_ROWS
                    dr = kr - roff - ri + (WIN_ROWS - 1)
                    pair.append(t_ref[0, dr] * log2e if in_window else masked_tile)
                bias_ref[cls, ri * GRID_W:(ri + 1) * GRID_W, kp * LANES:(kp + 1) * LANES] = (
                    jnp.concatenate(pair, axis=1))

    ng = rows // NA_G
    nq = NA_G * GRID_W
    nk = NA_WIN * GRID_W

    def slices(g):
        r0 = g * NA_G
        ws = jnp.clip(r0 - WIN_ROWS // 2, 0, rows - NA_WIN)
        return (pl.ds(pl.multiple_of(r0 * GRID_W, nq), nq),
                pl.ds(pl.multiple_of(ws * GRID_W, nq), nk))

    def scores(g, s_ref):
        g = jnp.minimum(g, ng - 1)
        cls = jnp.minimum(g, 1) + jnp.maximum(g - (ng - 2), 0)
        qsl, ksl = slices(g)
        s = lax.dot_general(q_ref[0, qsl, :], k_ref[0, ksl, :], (((1,), (1,)), ((), ())),
                            preferred_element_type=F32)
        s_ref[...] = s + bias_ref[cls]

    def attend(g, s_ref):
        qsl, ksl = slices(g)
        s = s_ref[...]
        m = jnp.max(s, axis=-1, keepdims=True)
        e = jnp.exp2(s - m)
        l = jnp.sum(e, axis=-1, keepdims=True)
        o = jnp.dot(e.astype(BF16), v_ref[0, ksl, :], preferred_element_type=F32) / l
        o_ref[qsl, :] = _rms(o, og).astype(o_ref.dtype)

    scores(0, s_a_ref)

    def pair_body(tp, carry):
        g = 2 * tp
        scores(g + 1, s_b_ref)
        attend(g, s_a_ref)
        scores(g + 2, s_a_ref)
        attend(g + 1, s_b_ref)
        return carry

    lax.fori_loop(0, ng // 2, pair_body, 0, unroll=2)


def _na_qk_gain(qg, kg, n_heads):
    qg = qg * np.float32(HEAD_DIM ** -0.5 * NA_LOG2E)
    return jnp.concatenate([jnp.tile(qg, n_heads), jnp.tile(kg, n_heads)]).reshape(1, -1)


def _na(zqk, zv, toeplitz, og, n_heads, batch, seq):
    rows = seq // GRID_W
    assert rows % (2 * NA_G) == 0 and (WIN_ROWS // 2) % NA_G == 0
    n_dr = 2 * WIN_ROWS - 1
    return pl.pallas_call(
        functools.partial(_na_kernel, rows=rows),
        grid=(batch, n_heads),
        in_specs=[pl.BlockSpec((1, seq, HEAD_DIM), lambda b, h: (h, b, 0)),
                  pl.BlockSpec((1, seq, HEAD_DIM), lambda b, h: (n_heads + h, b, 0)),
                  pl.BlockSpec((1, seq, HEAD_DIM), lambda b, h: (h, b, 0)),
                  pl.BlockSpec((1, n_dr, GRID_W, GRID_W), lambda b, h: (h, 0, 0, 0)),
                  pl.BlockSpec((1, 1, HEAD_DIM), lambda b, h: (h, 0, 0))],
        out_specs=pl.BlockSpec((seq, HEAD_DIM), lambda b, h: (b, h)),
        out_shape=jax.ShapeDtypeStruct((batch * seq, n_heads * HEAD_DIM), BF16),
        scratch_shapes=[pltpu.VMEM((3, NA_G * GRID_W, NA_WIN * GRID_W), F32),
                        pltpu.VMEM((NA_G * GRID_W, NA_WIN * GRID_W), F32),
                        pltpu.VMEM((NA_G * GRID_W, NA_WIN * GRID_W), F32)],
        compiler_params=_params(("parallel", "parallel"), 40),
        name="natten",
    )(zqk, zqk, zv, toeplitz, og.reshape(n_heads, 1, HEAD_DIM))


def _residual_epilogue(chunk, r0, epi, outs, tmp, *, rows, nj, d):
    (h_ref,) = epi
    o_ref, ob_ref, rstd_ref = outs
    (ssq_ref,) = tmp
    sl = slice(r0, r0 + rows)
    j = jnp.maximum(pl.program_id(0) - 1, 0) % nj
    h = h_ref[sl, :] + chunk
    o_ref[sl, :] = h
    ob_ref[sl, :] = h.astype(ob_ref.dtype)
    ssq = jnp.broadcast_to(jnp.sum(h * h, axis=-1, keepdims=True), (rows, LANES))
    ssq = jnp.where(j == 0, ssq, ssq_ref[sl, :] + ssq)
    ssq_ref[sl, :] = ssq
    rstd = lax.rsqrt(ssq * np.float32(1.0 / d) + EPS)
    rstd_ref[sl, :] = rstd
    return rstd[0:1, :]


def _residual_matmul(name, lhs, rhs, rhs_blocks, h, *, bm, epi_rows, vmem_mib, side=None):
    t, n = h.shape
    bn = RES_BN
    block = ((bm, bn), lambda i, j: (i, j))
    return _pipelined_matmul(
        name, lhs, rhs, rhs_blocks, [h], [block],
        [block, block, ((bm, LANES), lambda i, j: (i, 0))],
        [jax.ShapeDtypeStruct((t, n), F32), jax.ShapeDtypeStruct((t, n), BF16),
         jax.ShapeDtypeStruct((t, LANES), F32)],
        functools.partial(_residual_epilogue, rows=epi_rows, nj=n // bn, d=n),
        nj=n // bn, bm=bm, bn=bn, epi_rows=epi_rows, vmem_mib=vmem_mib, side=side,
        tmp_shapes=[(bm, LANES)])


def _outproj(mix_a, mix_b, w, x, bm=1024):
    assert mix_a.shape[1] == mix_b.shape[1]
    return _residual_matmul("outproj", [mix_a, mix_b], [w, w],
                            [lambda j: (0, j), lambda j: (1, j)], x,
                            bm=bm, epi_rows=32, vmem_mib=48)


def _ffn_up_window_start(i, bm, t):
    return pl.multiple_of(jnp.clip(i * bm - HALO, 0, t - (bm + 2 * HALO)), HALO)


def _ffn_up_kernel(x_ref, r_ref, wg_ref, wv_ref, cwg_ref, cwv_ref, cbg_ref, cbv_ref, side_ref,
                   o_ref, side_o_ref, rstd_ref, up_a_ref, up_b_ref, *, bm, bn, nj, ni, seq):
    s = pl.program_id(0)
    win = bm + 2 * HALO
    t = ni * bm
    i = jnp.minimum(s // nj, ni - 1)
    start = _ffn_up_window_start(i, bm, t)
    ip = jnp.maximum(s - 1, 0) // nj
    off_prev = pl.multiple_of(ip * bm - _ffn_up_window_start(ip, bm, t), HALO)

    @pl.when((s % nj == 0) & (s < ni * nj))
    def _():
        tok = start + lax.broadcasted_iota(jnp.int32, (win, LANES), 0)
        seq_start = (i * bm) // seq * seq
        same_seq = (tok >= seq_start) & (tok < seq_start + seq)
        rstd_ref[...] = jnp.where(same_seq, r_ref[...], 0.0)

    @pl.when(s == 0)
    def _():
        up_a_ref[...] = jnp.zeros(up_a_ref.shape, up_a_ref.dtype)
        up_b_ref[...] = jnp.zeros(up_b_ref.shape, up_b_ref.dtype)

    def step(new_ref, old_ref):
        cw = jnp.concatenate([cwg_ref[...], cwv_ref[...]], axis=1)
        cb = jnp.concatenate([cbg_ref[...], cbv_ref[...]], axis=1)
        w_cur = cw[1:2, :]
        for r0 in range(0, bm, EPI_ROWS):
            rows = EPI_ROWS + 2 * SUBLANES
            mid = slice(SUBLANES, SUBLANES + EPI_ROWS)
            slab = old_ref[pl.ds(pl.multiple_of(off_prev + r0, SUBLANES), rows), :]
            prev = pltpu.roll(slab, 1, 0)[mid]
            cur = slab[mid]
            nxt = pltpu.roll(slab, rows - 1, 0)[mid]
            c = cb + (prev * cw[0:1, :] + cur * w_cur + nxt * cw[2:3, :])
            act = _gelu(c[:, :bn]) * c[:, bn:]
            o_ref[r0:r0 + EPI_ROWS, :] = act.astype(o_ref.dtype)
            zero = _dependent_zero(act[0:1, 0:LANES])
            w_cur = cw[1:2, :] + jnp.concatenate([zero] * (2 * bn // LANES), axis=1)

        if new_ref is None:
            return
        _convert_weight_block(side_ref, None, side_o_ref)

        half = win // 2
        for h0 in (0, half):
            xh = x_ref[h0:h0 + half, :]
            rstd = jnp.concatenate([rstd_ref[h0:h0 + half, :]] * (bn // LANES), axis=1)
            dst = slice(SUBLANES + h0, SUBLANES + h0 + half)
            new_ref[dst, :bn] = jnp.dot(xh, wg_ref[...], preferred_element_type=F32) * rstd
            new_ref[dst, bn:] = jnp.dot(xh, wv_ref[...], preferred_element_type=F32) * rstd

    @pl.when((s % 2 == 0) & (s < ni * nj))
    def _():
        step(up_a_ref, up_b_ref)

    @pl.when((s % 2 == 1) & (s < ni * nj))
    def _():
        step(up_b_ref, up_a_ref)

    @pl.when(s == ni * nj)
    def _():
        step(None, up_a_ref if (ni * nj) % 2 == 1 else up_b_ref)


def _ffn_up(xn, rstd, wg, wv, cw, cb, seq, side, bm=2048, bn=256):
    t, k = xn.shape
    n = wg.shape[1]
    ni, nj = t // bm, n // bn
    win = bm + 2 * HALO
    assert seq % bm == 0 and t >= win and win % (2 * HALO) == 0
    side_in_specs, side_operands, side_out_spec, side_out_shape = _side_specs(side, ni * nj)
    assert len(side_operands) == 1

    def window(s):
        return _ffn_up_window_start(jnp.minimum(s // nj, ni - 1), bm, t), 0

    def prev_ij(s):
        sp = jnp.maximum(s - 1, 0)
        return sp // nj, sp % nj

    return pl.pallas_call(
        functools.partial(_ffn_up_kernel, bm=bm, bn=bn, nj=nj, ni=ni, seq=seq),
        grid=(ni * nj + 1,),
        in_specs=[pl.BlockSpec((pl.Element(win), pl.Element(k)), window),
                  pl.BlockSpec((pl.Element(win), pl.Element(LANES)), window),
                  pl.BlockSpec((k, bn), lambda s: (0, s % nj)),
                  pl.BlockSpec((k, bn), lambda s: (0, s % nj)),
                  pl.BlockSpec((3, bn), lambda s: (0, prev_ij(s)[1])),
                  pl.BlockSpec((3, bn), lambda s: (0, nj + prev_ij(s)[1])),
                  pl.BlockSpec((1, bn), lambda s: (0, prev_ij(s)[1])),
                  pl.BlockSpec((1, bn), lambda s: (0, nj + prev_ij(s)[1]))] + side_in_specs,
        out_specs=[pl.BlockSpec((bm, bn), lambda s: prev_ij(s)), side_out_spec],
        out_shape=[jax.ShapeDtypeStruct((t, n), BF16), side_out_shape],
        scratch_shapes=[pltpu.VMEM((win, LANES), F32),
                        pltpu.VMEM((win + 2 * SUBLANES, 2 * bn), F32),
                        pltpu.VMEM((win + 2 * SUBLANES, 2 * bn), F32)],
        compiler_params=_params(("arbitrary",), 60),
        name="ffn_up",
    )(xn, rstd, wg, wv, cw, cw, cb, cb, *side_operands)


def _ffn_down(act, w, h, bm=512, side=None):
    return _residual_matmul("ffn_down", [act], [w], [lambda j: (0, j)], h,
                            bm=bm, epi_rows=16, vmem_mib=56, side=side)


def _ple_embed_kernel(p_ref, w_ref, g_ref, o_ref):
    e = jnp.dot(p_ref[...].astype(BF16), w_ref[...], preferred_element_type=F32)
    o_ref[...] = _rms(e, g_ref[...]).astype(o_ref.dtype)


def _ple_embed(p, w, g, bm=512):
    t, k = p.shape
    n = w.shape[1]
    return pl.pallas_call(
        _ple_embed_kernel,
        grid=(t // bm,),
        in_specs=[pl.BlockSpec((bm, k), lambda i: (i, 0)),
                  pl.BlockSpec((k, n), lambda i: (0, 0)),
                  pl.BlockSpec((1, n), lambda i: (0, 0))],
        out_specs=pl.BlockSpec((bm, n), lambda i: (i, 0)),
        out_shape=jax.ShapeDtypeStruct((t, n), BF16),
        compiler_params=_params(("parallel",), 40),
        name="ple_embed",
    )(p, w, g.reshape(1, n))


def _ple_gate_epilogue(chunk, r0, epi, outs, tmp, *, rows):
    rstd_ref, e_ref, h_ref = epi
    (o_ref,) = outs
    sl = slice(r0, r0 + rows)
    rstd = jnp.concatenate([rstd_ref[sl, :]] * (chunk.shape[1] // LANES), axis=1)
    out = h_ref[sl, :] + jax.nn.sigmoid(chunk * rstd) * e_ref[sl, :].astype(F32)
    o_ref[sl, :] = out
    return out[0:1, 0:LANES]


def _ple_gate(hb, rstd, w, e, h, bm=1024, bn=512, epi_rows=8):
    t = hb.shape[0]
    n = w.shape[1]
    block = ((bm, bn), lambda i, j: (i, j))
    return _pipelined_matmul(
        "ple_gate", [hb], [w], [lambda j: (0, j)],
        [rstd, e, h], [((bm, LANES), lambda i, j: (i, 0)), block, block],
        [block], [jax.ShapeDtypeStruct((t, n), F32)],
        functools.partial(_ple_gate_epilogue, rows=epi_rows),
        nj=n // bn, bm=bm, bn=bn, epi_rows=epi_rows, vmem_mib=48)[0]


def kernel(x, p, norm_mix_g, w_in, gmlp_v_g, gmlp_ws, gmlp_bs, q_norm_g, k_norm_g, na_rpb,
           out_norm_a_g, out_norm_b_g, w_out, norm_ffn_g, w_up, conv_w, conv_b, w_down,
           norm_ple_g, w_ple_gate, w_ple_proj, ple_post_g):
    batch, seq, d_model = x.shape
    t = batch * seq
    depth = w_in.shape[0]
    n_a = gmlp_ws.shape[1]
    n_b = na_rpb.shape[1]
    h = x.reshape(t, d_model)
    for i in range(depth):
        hn = _rmsnorm(h, norm_mix_g[i])
        w = w_in[i].astype(BF16)
        d_a, d_b = n_a * HEAD_DIM, n_b * HEAD_DIM
        d_ff = w_down.shape[1]
        g_ffn = jnp.broadcast_to(norm_ffn_g[i][:, None], (d_model, LANES))
        g_ple = jnp.broadcast_to(norm_ple_g[i][:, None], (d_model, LANES))
        zuv, w_gate = _inproj(hn, w, 0, 2 * d_a, "gelu",
                              side=(w_up[i], g_ffn, 64, d_ff, 0))
        zqk, w_val = _inproj(hn, w, 2 * d_a, 2 * d_b, "norm",
                             _na_qk_gain(q_norm_g[i], k_norm_g[i], n_b),
                             side=(w_up[i], g_ffn, 64, d_ff, 1))
        zv, w_o = _inproj(hn, w, 2 * d_a + 2 * d_b, d_b, "plain",
                          side=(w_out[i], None, 128, d_model, 0))
        mix_a = _gmlp(zuv, gmlp_ws[i], gmlp_bs[i], gmlp_v_g[i], out_norm_a_g[i], n_a)
        mix_b = _na(zqk, zv, _na_toeplitz(na_rpb[i]), out_norm_b_g[i], n_b, batch, seq)
        h, hb, rstd = _outproj(mix_a, mix_b, w_o, h)
        act, w_dn = _ffn_up(hb, rstd, w_gate, w_val,
                            conv_w[i], conv_b[i].reshape(1, -1), seq,
                            side=(w_down[i], None, 32, d_model, 0))
        h, hb, rstd, w_pg = _ffn_down(act, w_dn, h, side=(w_ple_gate[i], g_ple, 16, d_model, 0))
        e = _ple_embed(p[i].reshape(t, -1), w_ple_proj[i].astype(BF16), ple_post_g[i])
        h = _ple_gate(hb, rstd, w_pg, e, h)
    return h.reshape(batch, seq, d_model)
```

```python
import functools

import numpy as np
import jax
import jax.numpy as jnp
from jax import lax
from jax.experimental import pallas as pl
from jax.experimental.pallas import tpu as pltpu

F32 = jnp.float32
BF16 = jnp.bfloat16

LANES = 128
SUBLANES = 8

EPS = 1e-6
HEAD_DIM = 128
CHUNK = 128
GMLP_GROUP = 8
GRID_W = 64
WIN_ROWS = 8
WIN_COLS = 16
NA_G = 4
NA_WIN = NA_G + WIN_ROWS
MASKED = -1e30
NA_LOG2E = float(np.log2(np.e))
HALO = 16
EPI_ROWS = 8
RES_BN = 512
MIB = 1 << 20


def _params(semantics, vmem_mib):
    return pltpu.CompilerParams(dimension_semantics=semantics, vmem_limit_bytes=vmem_mib * MIB)


def _rms(x, g):
    ms = jnp.mean(x * x, axis=-1, keepdims=True)
    return x * lax.rsqrt(ms + EPS) * g


def _gelu(x):
    return 0.5 * x * (1.0 + lax.erf(x * np.float32(np.sqrt(0.5))))


def _rmsnorm_kernel(x_ref, g_ref, o_ref):
    o_ref[...] = _rms(x_ref[...], g_ref[...]).astype(o_ref.dtype)


def _rmsnorm(x, g, bm=512):
    t, d = x.shape
    return pl.pallas_call(
        _rmsnorm_kernel,
        grid=(t // bm,),
        in_specs=[pl.BlockSpec((bm, d), lambda i: (i, 0)),
                  pl.BlockSpec((1, d), lambda i: (0, 0))],
        out_specs=pl.BlockSpec((bm, d), lambda i: (i, 0)),
        out_shape=jax.ShapeDtypeStruct((t, d), BF16),
        compiler_params=_params(("parallel",), 40),
        name="rmsnorm",
    )(x, g.reshape(1, d))


def _dependent_zero(x):
    bits = pltpu.bitcast(x, jnp.uint32)
    return pltpu.bitcast((bits >> 16) >> 16, F32)


SIDE_COLS = 256


def _convert_weight_block(x_ref, g_ref, o_ref):
    g = None if g_ref is None else jnp.concatenate([g_ref[...]] * (SIDE_COLS // LANES), axis=1)
    for c in range(0, x_ref.shape[1], SIDE_COLS):
        blk = x_ref[:, c:c + SIDE_COLS]
        if g is not None:
            blk = blk * g
        o_ref[:, c:c + SIDE_COLS] = blk.astype(o_ref.dtype)


def _side_specs(side, n_steps):
    x, gain, rows, cols, cj = side
    n_side = x.shape[0] // rows
    assert n_side <= n_steps and x.shape[0] % rows == 0 and cols % SIDE_COLS == 0

    def blk(s):
        return jnp.minimum(s, n_side - 1), cj

    in_specs = [pl.BlockSpec((rows, cols), blk)]
    operands = [x]
    if gain is not None:
        in_specs.append(pl.BlockSpec((rows, LANES), lambda s: (blk(s)[0], 0)))
        operands.append(gain)
    out_spec = pl.BlockSpec((rows, cols), lambda s: (blk(s)[0], 0))
    out_shape = jax.ShapeDtypeStruct((x.shape[0], cols), BF16)
    return in_specs, operands, out_spec, out_shape


def _pipelined_matmul_kernel(*refs, n_lhs, n_epi, n_out, n_side, n_tmp, n_blocks, epilogue, bm, bn,
                             epi_rows):
    lhs = refs[:n_lhs]
    rhs = refs[n_lhs:2 * n_lhs]
    epi = refs[2 * n_lhs:2 * n_lhs + n_epi]
    side_in = refs[2 * n_lhs + n_epi:2 * n_lhs + n_epi + n_side]
    n_in = 2 * n_lhs + n_epi + n_side
    outs = refs[n_in:n_in + n_out]
    side_out = refs[n_in + n_out:len(refs) - 2 - n_tmp]
    tmp = refs[len(refs) - 2 - n_tmp:-2]
    acc_a_ref, acc_b_ref = refs[-2:]
    s = pl.program_id(0)

    @pl.when(s == 0)
    def _():
        acc_b_ref[...] = jnp.zeros(acc_b_ref.shape, acc_b_ref.dtype)
        for t_ref in tmp:
            t_ref[...] = jnp.zeros(t_ref.shape, t_ref.dtype)

    def step(new_ref, old_ref):
        zero = None
        for r0 in range(0, bm, epi_rows):
            chunk = old_ref[r0:r0 + epi_rows, :]
            if zero is not None:
                chunk = chunk + zero
            last = epilogue(chunk, r0, epi, outs, tmp)
            zero = jnp.concatenate([_dependent_zero(last)] * (bn // LANES), axis=1)
        if new_ref is None:
            return
        if side_in:
            _convert_weight_block(side_in[0], side_in[1] if n_side == 2 else None, side_out[0])
        acc = jnp.dot(lhs[0][...], rhs[0][...], preferred_element_type=F32)
        for a_ref, w_ref in zip(lhs[1:], rhs[1:]):
            acc = acc + jnp.dot(a_ref[...], w_ref[...], preferred_element_type=F32)
        new_ref[...] = acc

    @pl.when((s % 2 == 0) & (s < n_blocks))
    def _():
        step(acc_a_ref, acc_b_ref)

    @pl.when((s % 2 == 1) & (s < n_blocks))
    def _():
        step(acc_b_ref, acc_a_ref)

    @pl.when(s == n_blocks)
    def _():
        step(None, acc_a_ref if n_blocks % 2 == 1 else acc_b_ref)


def _pipelined_matmul(name, lhs, rhs, rhs_blocks, epi, epi_specs, out_specs, out_shapes, epilogue,
                      *, nj, bm, bn, epi_rows, vmem_mib, side=None, tmp_shapes=()):
    ni = lhs[0].shape[0] // bm
    n_blocks = ni * nj

    def cur_ij(s):
        return jnp.minimum(s // nj, ni - 1), s % nj

    def prev_ij(s):
        sp = jnp.maximum(s - 1, 0)
        return sp // nj, sp % nj

    def cur_row(s):
        return cur_ij(s)[0], 0

    in_specs = [pl.BlockSpec((bm, a.shape[1]), cur_row) for a in lhs]
    in_specs += [pl.BlockSpec((a.shape[1], bn),
                              functools.partial(lambda s, blk: blk(cur_ij(s)[1]), blk=blk))
                 for a, blk in zip(lhs, rhs_blocks)]
    in_specs += [pl.BlockSpec(shape, functools.partial(lambda s, fn: fn(*prev_ij(s)), fn=fn))
                 for shape, fn in epi_specs]
    out_block_specs = [pl.BlockSpec(shape, functools.partial(lambda s, fn: fn(*prev_ij(s)), fn=fn))
                       for shape, fn in out_specs]
    out_shapes = list(out_shapes)
    side_operands = []
    if side is not None:
        side_in_specs, side_operands, side_out_spec, side_out_shape = _side_specs(side, n_blocks)
        in_specs += side_in_specs
        out_block_specs.append(side_out_spec)
        out_shapes.append(side_out_shape)
    return pl.pallas_call(
        functools.partial(_pipelined_matmul_kernel, n_lhs=len(lhs), n_epi=len(epi),
                          n_out=len(out_specs), n_side=len(side_operands),
                          n_tmp=len(tmp_shapes), n_blocks=n_blocks,
                          epilogue=epilogue, bm=bm, bn=bn, epi_rows=epi_rows),
        grid=(n_blocks + 1,),
        in_specs=in_specs,
        out_specs=out_block_specs,
        out_shape=out_shapes,
        scratch_shapes=[pltpu.VMEM(shape, F32) for shape in tmp_shapes]
        + [pltpu.VMEM((bm, bn), F32), pltpu.VMEM((bm, bn), F32)],
        compiler_params=_params(("arbitrary",), vmem_mib),
        name=name,
    )(*lhs, *rhs, *epi, *side_operands)


def _heads_epilogue(chunk, r0, epi, outs, tmp, *, rows, kind):
    (o_ref,) = outs
    y = None
    for hh in range(chunk.shape[1] // HEAD_DIM):
        cols = slice(hh * HEAD_DIM, (hh + 1) * HEAD_DIM)
        y = chunk[:, cols]
        if kind == "gelu":
            y = _gelu(y)
        elif kind == "norm":
            y = _rms(y, epi[0][:, cols])
        o_ref[hh, r0:r0 + rows, :] = y.astype(o_ref.dtype)
    return y[0:1, :]


def _inproj(hn, w, col0, n_cols, kind, gain=None, bm=1024, bn=1024, side=None):
    t = hn.shape[0]
    hpb = bn // HEAD_DIM
    epi_rows = {"norm": 32, "gelu": 8, "plain": 16}[kind]
    epi, epi_specs = [], []
    if kind == "norm":
        epi, epi_specs = [gain], [((1, bn), lambda i, j: (0, j))]
    return _pipelined_matmul(
        "inproj_" + kind, [hn], [w], [lambda j: (0, col0 // bn + j)], epi, epi_specs,
        [((hpb, bm, HEAD_DIM), lambda i, j: (j, i, 0))],
        [jax.ShapeDtypeStruct((n_cols // HEAD_DIM, t, HEAD_DIM), BF16)],
        functools.partial(_heads_epilogue, rows=epi_rows, kind=kind),
        nj=n_cols // bn, bm=bm, bn=bn, epi_rows=epi_rows, vmem_mib=62, side=side)


def _gmlp_kernel(u_ref, v_ref, ws_ref, bs_ref, gv_ref, og_ref, o_ref, *, n_groups):
    ws = ws_ref[0]
    bs = bs_ref[0]
    gv = gv_ref[0]
    og = og_ref[0]

    def group_body(gi, carry):
        base = pl.multiple_of(gi * (GMLP_GROUP * CHUNK), GMLP_GROUP * CHUNK)
        for c in range(GMLP_GROUP):
            sl = pl.ds(base + c * CHUNK, CHUNK)
            vn = _rms(v_ref[0, sl, :].astype(F32), gv).astype(BF16)
            mixed = jnp.dot(ws, vn, preferred_element_type=F32) + bs
            a = u_ref[0, sl, :].astype(F32) * mixed
            o_ref[sl, :] = _rms(a, og).astype(o_ref.dtype)
        return carry

    lax.fori_loop(0, n_groups, group_body, 0)


def _gmlp(z, ws, bs, gv, og, n_heads, tb=8192):
    t = z.shape[1]
    return pl.pallas_call(
        functools.partial(_gmlp_kernel, n_groups=tb // (GMLP_GROUP * CHUNK)),
        grid=(n_heads, t // tb),
        in_specs=[pl.BlockSpec((1, tb, HEAD_DIM), lambda h, i: (h, i, 0)),
                  pl.BlockSpec((1, tb, HEAD_DIM), lambda h, i: (n_heads + h, i, 0)),
                  pl.BlockSpec((1, CHUNK, CHUNK), lambda h, i: (h, 0, 0)),
                  pl.BlockSpec((1, CHUNK, 1), lambda h, i: (h, 0, 0)),
                  pl.BlockSpec((1, 1, HEAD_DIM), lambda h, i: (h, 0, 0)),
                  pl.BlockSpec((1, 1, HEAD_DIM), lambda h, i: (h, 0, 0))],
        out_specs=pl.BlockSpec((tb, HEAD_DIM), lambda h, i: (i, h)),
        out_shape=jax.ShapeDtypeStruct((t, n_heads * HEAD_DIM), BF16),
        compiler_params=_params(("parallel", "parallel"), 24),
        name="gmlp",
    )(z, z, ws.astype(BF16), bs.reshape(n_heads, CHUNK, 1),
      gv.reshape(n_heads, 1, HEAD_DIM), og.reshape(n_heads, 1, HEAD_DIM))


def _na_toeplitz(rpb):
    col = np.arange(GRID_W)
    cs = np.clip(col - WIN_COLS // 2, 0, GRID_W - WIN_COLS)
    kc = np.arange(GRID_W)
    valid = (kc[None, :] >= cs[:, None]) & (kc[None, :] < cs[:, None] + WIN_COLS)
    dc = kc[None, :] - col[:, None] + (WIN_COLS - 1)
    onehot = (dc[:, :, None] == np.arange(2 * WIN_COLS - 1)).astype(np.float32)
    t = jnp.einsum('hrd,ckd->hrck', rpb, onehot, precision=lax.Precision.HIGHEST)
    return jnp.where(valid[None, None], t, MASKED)


def _na_group_geometry(rows):
    def geom(g):
        r0 = g * NA_G
        ws = int(np.clip(r0 - WIN_ROWS // 2, 0, rows - NA_WIN))
        rs = [int(np.clip(r0 + i - WIN_ROWS // 2, 0, rows - WIN_ROWS)) for i in range(NA_G)]
        return r0 - ws, tuple(r - ws for r in rs)

    ng = rows // NA_G
    assert ng >= 3 and all(geom(g) == geom(1) for g in range(1, ng - 1))
    return geom(0), geom(1), geom(ng - 1)


def _na_kernel(q_ref, k_ref, v_ref, t_ref, og_ref, o_ref, bias_ref, s_a_ref, s_b_ref, *, rows):
    log2e = np.float32(NA_LOG2E)
    og = og_ref[0]

    masked_tile = jnp.full((GRID_W, GRID_W), MASKED, F32)
    for cls, (roff, rsoff) in enumerate(_na_group_geometry(rows)):
        for ri in range(NA_G):
            for kp in range(NA_WIN // 2):
                pair = []
                for kr in (2 * kp, 2 * kp + 1):
                    in_window = 0 <= kr - rsoff[ri] < WIN_ROWS
                    dr = kr - roff - ri + (WIN_ROWS - 1)
                    pair.append(t_ref[0, dr] * log2e if in_window else masked_tile)
                bias_ref[cls, ri * GRID_W:(ri + 1) * GRID_W, kp * LANES:(kp + 1) * LANES] = (
                    jnp.concatenate(pair, axis=1))

    ng = rows // NA_G
    nq = NA_G * GRID_W
    nk = NA_WIN * GRID_W

    def slices(g):
        r0 = g * NA_G
        ws = jnp.clip(r0 - WIN_ROWS // 2, 0, rows - NA_WIN)
        return (pl.ds(pl.multiple_of(r0 * GRID_W, nq), nq),
                pl.ds(pl.multiple_of(ws * GRID_W, nq), nk))

    def scores(g, s_ref):
        g = jnp.minimum(g, ng - 1)
        cls = jnp.minimum(g, 1) + jnp.maximum(g - (ng - 2), 0)
        qsl, ksl = slices(g)
        s = lax.dot_general(q_ref[0, qsl, :], k_ref[0, ksl, :], (((1,), (1,)), ((), ())),
                            preferred_element_type=F32)
        s_ref[...] = s + bias_ref[cls]

    def attend(g, s_ref):
        qsl, ksl = slices(g)
        s = s_ref[...]
        m = jnp.max(s, axis=-1, keepdims=True)
        e = jnp.exp2(s - m)
        l = jnp.sum(e, axis=-1, keepdims=True)
        o = jnp.dot(e.astype(BF16), v_ref[0, ksl, :], preferred_element_type=F32) / l
        o_ref[qsl, :] = _rms(o, og).astype(o_ref.dtype)

    scores(0, s_a_ref)

    def pair_body(tp, carry):
        g = 2 * tp
        scores(g + 1, s_b_ref)
        attend(g, s_a_ref)
        scores(g + 2, s_a_ref)
        attend(g + 1, s_b_ref)
        return carry

    lax.fori_loop(0, ng // 2, pair_body, 0, unroll=2)


def _na_qk_gain(qg, kg, n_heads):
    qg = qg * np.float32(HEAD_DIM ** -0.5 * NA_LOG2E)
    return jnp.concatenate([jnp.tile(qg, n_heads), jnp.tile(kg, n_heads)]).reshape(1, -1)


def _na(zqk, zv, toeplitz, og, n_heads, batch, seq):
    rows = seq // GRID_W
    assert rows % (2 * NA_G) == 0 and (WIN_ROWS // 2) % NA_G == 0
    n_dr = 2 * WIN_ROWS - 1
    return pl.pallas_call(
        functools.partial(_na_kernel, rows=rows),
        grid=(batch, n_heads),
        in_specs=[pl.BlockSpec((1, seq, HEAD_DIM), lambda b, h: (h, b, 0)),
                  pl.BlockSpec((1, seq, HEAD_DIM), lambda b, h: (n_heads + h, b, 0)),
                  pl.BlockSpec((1, seq, HEAD_DIM), lambda b, h: (h, b, 0)),
                  pl.BlockSpec((1, n_dr, GRID_W, GRID_W), lambda b, h: (h, 0, 0, 0)),
                  pl.BlockSpec((1, 1, HEAD_DIM), lambda b, h: (h, 0, 0))],
        out_specs=pl.BlockSpec((seq, HEAD_DIM), lambda b, h: (b, h)),
        out_shape=jax.ShapeDtypeStruct((batch * seq, n_heads * HEAD_DIM), BF16),
        scratch_shapes=[pltpu.VMEM((3, NA_G * GRID_W, NA_WIN * GRID_W), F32),
                        pltpu.VMEM((NA_G * GRID_W, NA_WIN * GRID_W), F32),
                        pltpu.VMEM((NA_G * GRID_W, NA_WIN * GRID_W), F32)],
        compiler_params=_params(("parallel", "parallel"), 40),
        name="natten",
    )(zqk, zqk, zv, toeplitz, og.reshape(n_heads, 1, HEAD_DIM))


def _residual_epilogue(chunk, r0, epi, outs, tmp, *, rows, nj, d):
    (h_ref,) = epi
    o_ref, ob_ref, rstd_ref = outs
    (ssq_ref,) = tmp
    sl = slice(r0, r0 + rows)
    j = jnp.maximum(pl.program_id(0) - 1, 0) % nj
    h = h_ref[sl, :] + chunk
    o_ref[sl, :] = h
    ob_ref[sl, :] = h.astype(ob_ref.dtype)
    ssq = jnp.broadcast_to(jnp.sum(h * h, axis=-1, keepdims=True), (rows, LANES))
    ssq = jnp.where(j == 0, ssq, ssq_ref[sl, :] + ssq)
    ssq_ref[sl, :] = ssq
    rstd = lax.rsqrt(ssq * np.float32(1.0 / d) + EPS)
    rstd_ref[sl, :] = rstd
    return rstd[0:1, :]


def _residual_matmul(name, lhs, rhs, rhs_blocks, h, *, bm, epi_rows, vmem_mib, side=None):
    t, n = h.shape
    bn = RES_BN
    block = ((bm, bn), lambda i, j: (i, j))
    return _pipelined_matmul(
        name, lhs, rhs, rhs_blocks, [h], [block],
        [block, block, ((bm, LANES), lambda i, j: (i, 0))],
        [jax.ShapeDtypeStruct((t, n), F32), jax.ShapeDtypeStruct((t, n), BF16),
         jax.ShapeDtypeStruct((t, LANES), F32)],
        functools.partial(_residual_epilogue, rows=epi_rows, nj=n // bn, d=n),
        nj=n // bn, bm=bm, bn=bn, epi_rows=epi_rows, vmem_mib=vmem_mib, side=side,
        tmp_shapes=[(bm, LANES)])


def _outproj(mix_a, mix_b, w, x, bm=1024):
    assert mix_a.shape[1] == mix_b.shape[1]
    return _residual_matmul("outproj", [mix_a, mix_b], [w, w],
                            [lambda j: (0, j), lambda j: (1, j)], x,
                            bm=bm, epi_rows=32, vmem_mib=48)


def _ffn_up_window_start(i, bm, t):
    return pl.multiple_of(jnp.clip(i * bm - HALO, 0, t - (bm + 2 * HALO)), HALO)


def _ffn_up_kernel(x_ref, r_ref, wg_ref, wv_ref, cw_ref, cb_ref, side_ref,
                   o_ref, side_o_ref, rstd_ref, up_a_ref, up_b_ref, *, bm, bn, nj, ni, seq):
    s = pl.program_id(0)
    win = bm + 2 * HALO
    t = ni * bm
    i = jnp.minimum(s // nj, ni - 1)
    start = _ffn_up_window_start(i, bm, t)
    ip = jnp.maximum(s - 1, 0) // nj
    off_prev = pl.multiple_of(ip * bm - _ffn_up_window_start(ip, bm, t), HALO)

    @pl.when((s % nj == 0) & (s < ni * nj))
    def _():
        tok = start + lax.broadcasted_iota(jnp.int32, (win, LANES), 0)
        seq_start = (i * bm) // seq * seq
        same_seq = (tok >= seq_start) & (tok < seq_start + seq)
        rstd_ref[...] = jnp.where(same_seq, r_ref[...], 0.0)

    @pl.when(s == 0)
    def _():
        up_a_ref[...] = jnp.zeros(up_a_ref.shape, up_a_ref.dtype)
        up_b_ref[...] = jnp.zeros(up_b_ref.shape, up_b_ref.dtype)

    def step(new_ref, old_ref):
        jp = jnp.maximum(s - 1, 0) % nj
        gate_cols = pl.ds(pl.multiple_of(jp * bn, bn), bn)
        value_cols = pl.ds(pl.multiple_of((nj + jp) * bn, bn), bn)
        cw = jnp.concatenate([cw_ref[:, gate_cols], cw_ref[:, value_cols]], axis=1)
        cb = jnp.concatenate([cb_ref[:, gate_cols], cb_ref[:, value_cols]], axis=1)
        w_cur = cw[1:2, :]
        for r0 in range(0, bm, EPI_ROWS):
            rows = EPI_ROWS + 2 * SUBLANES
            mid = slice(SUBLANES, SUBLANES + EPI_ROWS)
            slab = old_ref[pl.ds(pl.multiple_of(off_prev + r0, SUBLANES), rows), :]
            prev = pltpu.roll(slab, 1, 0)[mid]
            cur = slab[mid]
            nxt = pltpu.roll(slab, rows - 1, 0)[mid]
            c = cb + (prev * cw[0:1, :] + cur * w_cur + nxt * cw[2:3, :])
            act = _gelu(c[:, :bn]) * c[:, bn:]
            o_ref[r0:r0 + EPI_ROWS, :] = act.astype(o_ref.dtype)
            zero = _dependent_zero(act[0:1, 0:LANES])
            w_cur = cw[1:2, :] + jnp.concatenate([zero] * (2 * bn // LANES), axis=1)

        if new_ref is None:
            return
        _convert_weight_block(side_ref, None, side_o_ref)

        half = win // 2
        for h0 in (0, half):
            xh = x_ref[h0:h0 + half, :]
            rstd = jnp.concatenate([rstd_ref[h0:h0 + half, :]] * (bn // LANES), axis=1)
            dst = slice(SUBLANES + h0, SUBLANES + h0 + half)
            new_ref[dst, :bn] = jnp.dot(xh, wg_ref[...], preferred_element_type=F32) * rstd
            new_ref[dst, bn:] = jnp.dot(xh, wv_ref[...], preferred_element_type=F32) * rstd

    @pl.when((s % 2 == 0) & (s < ni * nj))
    def _():
        step(up_a_ref, up_b_ref)

    @pl.when((s % 2 == 1) & (s < ni * nj))
    def _():
        step(up_b_ref, up_a_ref)

    @pl.when(s == ni * nj)
    def _():
        step(None, up_a_ref if (ni * nj) % 2 == 1 else up_b_ref)


def _ffn_up(xn, rstd, wg, wv, cw, cb, seq, side, bm=2048, bn=256):
    t, k = xn.shape
    n = wg.shape[1]
    ni, nj = t // bm, n // bn
    win = bm + 2 * HALO
    assert seq % bm == 0 and t >= win and win % (2 * HALO) == 0
    side_in_specs, side_operands, side_out_spec, side_out_shape = _side_specs(side, ni * nj)
    assert len(side_operands) == 1

    def window(s):
        return _ffn_up_window_start(jnp.minimum(s // nj, ni - 1), bm, t), 0

    def prev_ij(s):
        sp = jnp.maximum(s - 1, 0)
        return sp // nj, sp % nj

    return pl.pallas_call(
        functools.partial(_ffn_up_kernel, bm=bm, bn=bn, nj=nj, ni=ni, seq=seq),
        grid=(ni * nj + 1,),
        in_specs=[pl.BlockSpec((pl.Element(win), pl.Element(k)), window),
                  pl.BlockSpec((pl.Element(win), pl.Element(LANES)), window),
                  pl.BlockSpec((k, bn), lambda s: (0, s % nj)),
                  pl.BlockSpec((k, bn), lambda s: (0, s % nj)),
                  pl.BlockSpec(cw.shape, lambda s: (0, 0)),
                  pl.BlockSpec(cb.shape, lambda s: (0, 0))] + side_in_specs,
        out_specs=[pl.BlockSpec((bm, bn), lambda s: prev_ij(s)), side_out_spec],
        out_shape=[jax.ShapeDtypeStruct((t, n), BF16), side_out_shape],
        scratch_shapes=[pltpu.VMEM((win, LANES), F32),
                        pltpu.VMEM((win + 2 * SUBLANES, 2 * bn), F32),
                        pltpu.VMEM((win + 2 * SUBLANES, 2 * bn), F32)],
        compiler_params=_params(("arbitrary",), 60),
        name="ffn_up",
    )(xn, rstd, wg, wv, cw, cb, *side_operands)


def _ffn_down(act, w, h, bm=512, side=None):
    return _residual_matmul("ffn_down", [act], [w], [lambda j: (0, j)], h,
                            bm=bm, epi_rows=16, vmem_mib=56, side=side)


def _ple_embed_kernel(p_ref, w_ref, g_ref, o_ref):
    e = jnp.dot(p_ref[...].astype(BF16), w_ref[...], preferred_element_type=F32)
    o_ref[...] = _rms(e, g_ref[...]).astype(o_ref.dtype)


def _ple_embed(p, w, g, bm=512):
    t, k = p.shape
    n = w.shape[1]
    return pl.pallas_call(
        _ple_embed_kernel,
        grid=(t // bm,),
        in_specs=[pl.BlockSpec((bm, k), lambda i: (i, 0)),
                  pl.BlockSpec((k, n), lambda i: (0, 0)),
                  pl.BlockSpec((1, n), lambda i: (0, 0))],
        out_specs=pl.BlockSpec((bm, n), lambda i: (i, 0)),
        out_shape=jax.ShapeDtypeStruct((t, n), BF16),
        compiler_params=_params(("parallel",), 40),
        name="ple_embed",
    )(p, w, g.reshape(1, n))


def _ple_gate_epilogue(chunk, r0, epi, outs, tmp, *, rows):
    rstd_ref, e_ref, h_ref = epi
    (o_ref,) = outs
    sl = slice(r0, r0 + rows)
    rstd = jnp.concatenate([rstd_ref[sl, :]] * (chunk.shape[1] // LANES), axis=1)
    out = h_ref[sl, :] + jax.nn.sigmoid(chunk * rstd) * e_ref[sl, :].astype(F32)
    o_ref[sl, :] = out
    return out[0:1, 0:LANES]


def _ple_gate(hb, rstd, w, e, h, bm=1024, bn=512, epi_rows=8):
    t = hb.shape[0]
    n = w.shape[1]
    block = ((bm, bn), lambda i, j: (i, j))
    return _pipelined_matmul(
        "ple_gate", [hb], [w], [lambda j: (0, j)],
        [rstd, e, h], [((bm, LANES), lambda i, j: (i, 0)), block, block],
        [block], [jax.ShapeDtypeStruct((t, n), F32)],
        functools.partial(_ple_gate_epilogue, rows=epi_rows),
        nj=n // bn, bm=bm, bn=bn, epi_rows=epi_rows, vmem_mib=48)[0]


def kernel(x, p, norm_mix_g, w_in, gmlp_v_g, gmlp_ws, gmlp_bs, q_norm_g, k_norm_g, na_rpb,
           out_norm_a_g, out_norm_b_g, w_out, norm_ffn_g, w_up, conv_w, conv_b, w_down,
           norm_ple_g, w_ple_gate, w_ple_proj, ple_post_g):
    batch, seq, d_model = x.shape
    t = batch * seq
    depth = w_in.shape[0]
    n_a = gmlp_ws.shape[1]
    n_b = na_rpb.shape[1]
    h = x.reshape(t, d_model)
    for i in range(depth):
        hn = _rmsnorm(h, norm_mix_g[i])
        w = w_in[i].astype(BF16)
        d_a, d_b = n_a * HEAD_DIM, n_b * HEAD_DIM
        d_ff = w_down.shape[1]
        g_ffn = jnp.broadcast_to(norm_ffn_g[i][:, None], (d_model, LANES))
        g_ple = jnp.broadcast_to(norm_ple_g[i][:, None], (d_model, LANES))
        zuv, w_gate = _inproj(hn, w, 0, 2 * d_a, "gelu",
                              side=(w_up[i], g_ffn, 64, d_ff, 0))
        zqk, w_val = _inproj(hn, w, 2 * d_a, 2 * d_b, "norm",
                             _na_qk_gain(q_norm_g[i], k_norm_g[i], n_b),
                             side=(w_up[i], g_ffn, 64, d_ff, 1))
        zv, w_o = _inproj(hn, w, 2 * d_a + 2 * d_b, d_b, "plain",
                          side=(w_out[i], None, 128, d_model, 0))
        mix_a = _gmlp(zuv, gmlp_ws[i], gmlp_bs[i], gmlp_v_g[i], out_norm_a_g[i], n_a)
        mix_b = _na(zqk, zv, _na_toeplitz(na_rpb[i]), out_norm_b_g[i], n_b, batch, seq)
        h, hb, rstd = _outproj(mix_a, mix_b, w_o, h)
        act, w_dn = _ffn_up(hb, rstd, w_gate, w_val,
                            conv_w[i], conv_b[i].reshape(1, -1), seq,
                            side=(w_down[i], None, 32, d_model, 0))
        h, hb, rstd, w_pg = _ffn_down(act, w_dn, h, side=(w_ple_gate[i], g_ple, 16, d_model, 0))
        e = _ple_embed(p[i].reshape(t, -1), w_ple_proj[i].astype(BF16), ple_post_g[i])
        h = _ple_gate(hb, rstd, w_pg, e, h)
    return h.reshape(batch, seq, d_model)
```

```python
import functools

import numpy as np
import jax
import jax.numpy as jnp
from jax import lax
from jax.experimental import pallas as pl
from jax.experimental.pallas import tpu as pltpu

F32 = jnp.float32
BF16 = jnp.bfloat16

LANES = 128
SUBLANES = 8

EPS = 1e-6
HEAD_DIM = 128
CHUNK = 128
GMLP_GROUP = 8
GRID_W = 64
WIN_ROWS = 8
WIN_COLS = 16
NA_G = 4
NA_WIN = NA_G + WIN_ROWS
MASKED = -1e30
NA_LOG2E = float(np.log2(np.e))
HALO = 16
EPI_ROWS = 8
RES_BN = 512
FFN_BN = 256
PLE_BN = 512
MIB = 1 << 20


def _params(semantics, vmem_mib):
    return pltpu.CompilerParams(dimension_semantics=semantics, vmem_limit_bytes=vmem_mib * MIB)


def _rms(x, g):
    ms = jnp.mean(x * x, axis=-1, keepdims=True)
    return x * lax.rsqrt(ms + EPS) * g


def _gelu(x):
    return 0.5 * x * (1.0 + lax.erf(x * np.float32(np.sqrt(0.5))))


def _rmsnorm_kernel(x_ref, g_ref, o_ref):
    o_ref[...] = _rms(x_ref[...], g_ref[...]).astype(o_ref.dtype)


def _rmsnorm(x, g, bm=512):
    t, d = x.shape
    return pl.pallas_call(
        _rmsnorm_kernel,
        grid=(t // bm,),
        in_specs=[pl.BlockSpec((bm, d), lambda i: (i, 0)),
                  pl.BlockSpec((1, d), lambda i: (0, 0))],
        out_specs=pl.BlockSpec((bm, d), lambda i: (i, 0)),
        out_shape=jax.ShapeDtypeStruct((t, d), BF16),
        compiler_params=_params(("parallel",), 40),
        name="rmsnorm",
    )(x, g.reshape(1, d))


def _dependent_zero(x):
    bits = pltpu.bitcast(x, jnp.uint32)
    return pltpu.bitcast((bits >> 16) >> 16, F32)


def _convert_weight_block(x_ref, g_ref, o_ref):
    n_cb, _, bw = o_ref.shape
    g = None if g_ref is None else jnp.concatenate([g_ref[...]] * (bw // LANES), axis=1)
    for c in range(n_cb):
        blk = x_ref[:, c * bw:(c + 1) * bw]
        if g is not None:
            blk = blk * g
        o_ref[c] = blk.astype(o_ref.dtype)


def _side_specs(side, n_steps):
    x, gain, rows, cols, cj, bw = side
    n_side = x.shape[0] // rows
    assert n_side <= n_steps and x.shape[0] % rows == 0 and cols % bw == 0 and bw % LANES == 0

    def blk(s):
        return jnp.minimum(s, n_side - 1), cj

    in_specs = [pl.BlockSpec((rows, cols), blk)]
    operands = [x]
    if gain is not None:
        in_specs.append(pl.BlockSpec((rows, LANES), lambda s: (blk(s)[0], 0)))
        operands.append(gain)
    out_spec = pl.BlockSpec((cols // bw, rows, bw), lambda s: (0, blk(s)[0], 0))
    out_shape = jax.ShapeDtypeStruct((cols // bw, x.shape[0], bw), BF16)
    return in_specs, operands, out_spec, out_shape


def _pipelined_matmul_kernel(*refs, n_lhs, n_epi, n_out, n_side, n_tmp, n_blocks, epilogue, bm, bn,
                             epi_rows):
    lhs = refs[:n_lhs]
    rhs = refs[n_lhs:2 * n_lhs]
    epi = refs[2 * n_lhs:2 * n_lhs + n_epi]
    side_in = refs[2 * n_lhs + n_epi:2 * n_lhs + n_epi + n_side]
    n_in = 2 * n_lhs + n_epi + n_side
    outs = refs[n_in:n_in + n_out]
    side_out = refs[n_in + n_out:len(refs) - 2 - n_tmp]
    tmp = refs[len(refs) - 2 - n_tmp:-2]
    acc_a_ref, acc_b_ref = refs[-2:]
    s = pl.program_id(0)

    @pl.when(s == 0)
    def _():
        acc_b_ref[...] = jnp.zeros(acc_b_ref.shape, acc_b_ref.dtype)
        for t_ref in tmp:
            t_ref[...] = jnp.zeros(t_ref.shape, t_ref.dtype)

    def step(new_ref, old_ref):
        zero = None
        for r0 in range(0, bm, epi_rows):
            chunk = old_ref[r0:r0 + epi_rows, :]
            if zero is not None:
                chunk = chunk + zero
            last = epilogue(chunk, r0, epi, outs, tmp)
            zero = jnp.concatenate([_dependent_zero(last)] * (bn // LANES), axis=1)
        if new_ref is None:
            return
        if side_in:
            _convert_weight_block(side_in[0], side_in[1] if n_side == 2 else None, side_out[0])
        acc = jnp.dot(lhs[0][...], rhs[0][...], preferred_element_type=F32)
        for a_ref, w_ref in zip(lhs[1:], rhs[1:]):
            acc = acc + jnp.dot(a_ref[...], w_ref[...], preferred_element_type=F32)
        new_ref[...] = acc

    @pl.when((s % 2 == 0) & (s < n_blocks))
    def _():
        step(acc_a_ref, acc_b_ref)

    @pl.when((s % 2 == 1) & (s < n_blocks))
    def _():
        step(acc_b_ref, acc_a_ref)

    @pl.when(s == n_blocks)
    def _():
        step(None, acc_a_ref if n_blocks % 2 == 1 else acc_b_ref)


def _pipelined_matmul(name, lhs, rhs, rhs_blocks, epi, epi_specs, out_specs, out_shapes, epilogue,
                      *, nj, bm, bn, epi_rows, vmem_mib, side=None, tmp_shapes=()):
    ni = lhs[0].shape[0] // bm
    n_blocks = ni * nj

    def cur_ij(s):
        return jnp.minimum(s // nj, ni - 1), s % nj

    def prev_ij(s):
        sp = jnp.maximum(s - 1, 0)
        return sp // nj, sp % nj

    def cur_row(s):
        return cur_ij(s)[0], 0

    in_specs = [pl.BlockSpec((bm, a.shape[1]), cur_row) for a in lhs]
    in_specs += [pl.BlockSpec((None,) * (w.ndim - 2) + (a.shape[1], bn),
                              functools.partial(lambda s, blk: blk(cur_ij(s)[1]), blk=blk))
                 for a, w, blk in zip(lhs, rhs, rhs_blocks)]
    in_specs += [pl.BlockSpec(shape, functools.partial(lambda s, fn: fn(*prev_ij(s)), fn=fn))
                 for shape, fn in epi_specs]
    out_block_specs = [pl.BlockSpec(shape, functools.partial(lambda s, fn: fn(*prev_ij(s)), fn=fn))
                       for shape, fn in out_specs]
    out_shapes = list(out_shapes)
    side_operands = []
    if side is not None:
        side_in_specs, side_operands, side_out_spec, side_out_shape = _side_specs(side, n_blocks)
        in_specs += side_in_specs
        out_block_specs.append(side_out_spec)
        out_shapes.append(side_out_shape)
    return pl.pallas_call(
        functools.partial(_pipelined_matmul_kernel, n_lhs=len(lhs), n_epi=len(epi),
                          n_out=len(out_specs), n_side=len(side_operands),
                          n_tmp=len(tmp_shapes), n_blocks=n_blocks,
                          epilogue=epilogue, bm=bm, bn=bn, epi_rows=epi_rows),
        grid=(n_blocks + 1,),
        in_specs=in_specs,
        out_specs=out_block_specs,
        out_shape=out_shapes,
        scratch_shapes=[pltpu.VMEM(shape, F32) for shape in tmp_shapes]
        + [pltpu.VMEM((bm, bn), F32), pltpu.VMEM((bm, bn), F32)],
        compiler_params=_params(("arbitrary",), vmem_mib),
        name=name,
    )(*lhs, *rhs, *epi, *side_operands)


def _heads_epilogue(chunk, r0, epi, outs, tmp, *, rows, kind):
    (o_ref,) = outs
    y = None
    for hh in range(chunk.shape[1] // HEAD_DIM):
        cols = slice(hh * HEAD_DIM, (hh + 1) * HEAD_DIM)
        y = chunk[:, cols]
        if kind == "gelu":
            y = _gelu(y)
        elif kind == "norm":
            y = _rms(y, epi[0][:, cols])
        o_ref[hh, r0:r0 + rows, :] = y.astype(o_ref.dtype)
    return y[0:1, :]


def _inproj(hn, w, col0, n_cols, kind, gain=None, bm=1024, bn=1024, side=None):
    t = hn.shape[0]
    hpb = bn // HEAD_DIM
    epi_rows = {"norm": 32, "gelu": 8, "plain": 16}[kind]
    epi, epi_specs = [], []
    if kind == "norm":
        epi, epi_specs = [gain], [((1, bn), lambda i, j: (0, j))]
    return _pipelined_matmul(
        "inproj_" + kind, [hn], [w], [lambda j: (0, col0 // bn + j)], epi, epi_specs,
        [((hpb, bm, HEAD_DIM), lambda i, j: (j, i, 0))],
        [jax.ShapeDtypeStruct((n_cols // HEAD_DIM, t, HEAD_DIM), BF16)],
        functools.partial(_heads_epilogue, rows=epi_rows, kind=kind),
        nj=n_cols // bn, bm=bm, bn=bn, epi_rows=epi_rows, vmem_mib=62, side=side)


def _gmlp_kernel(u_ref, v_ref, ws_ref, bs_ref, gv_ref, og_ref, o_ref, *, n_groups):
    ws = ws_ref[0]
    bs = bs_ref[0]
    gv = gv_ref[0]
    og = og_ref[0]

    def group_body(gi, carry):
        base = pl.multiple_of(gi * (GMLP_GROUP * CHUNK), GMLP_GROUP * CHUNK)
        for c in range(GMLP_GROUP):
            sl = pl.ds(base + c * CHUNK, CHUNK)
            vn = _rms(v_ref[0, sl, :].astype(F32), gv).astype(BF16)
            mixed = jnp.dot(ws, vn, preferred_element_type=F32) + bs
            a = u_ref[0, sl, :].astype(F32) * mixed
            o_ref[sl, :] = _rms(a, og).astype(o_ref.dtype)
        return carry

    lax.fori_loop(0, n_groups, group_body, 0)


def _gmlp(z, ws, bs, gv, og, n_heads, tb=8192):
    t = z.shape[1]
    return pl.pallas_call(
        functools.partial(_gmlp_kernel, n_groups=tb // (GMLP_GROUP * CHUNK)),
        grid=(n_heads, t // tb),
        in_specs=[pl.BlockSpec((1, tb, HEAD_DIM), lambda h, i: (h, i, 0)),
                  pl.BlockSpec((1, tb, HEAD_DIM), lambda h, i: (n_heads + h, i, 0)),
                  pl.BlockSpec((1, CHUNK, CHUNK), lambda h, i: (h, 0, 0)),
                  pl.BlockSpec((1, CHUNK, 1), lambda h, i: (h, 0, 0)),
                  pl.BlockSpec((1, 1, HEAD_DIM), lambda h, i: (h, 0, 0)),
                  pl.BlockSpec((1, 1, HEAD_DIM), lambda h, i: (h, 0, 0))],
        out_specs=pl.BlockSpec((tb, HEAD_DIM), lambda h, i: (i, h)),
        out_shape=jax.ShapeDtypeStruct((t, n_heads * HEAD_DIM), BF16),
        compiler_params=_params(("parallel", "parallel"), 24),
        name="gmlp",
    )(z, z, ws.astype(BF16), bs.reshape(n_heads, CHUNK, 1),
      gv.reshape(n_heads, 1, HEAD_DIM), og.reshape(n_heads, 1, HEAD_DIM))


def _na_toeplitz(rpb):
    col = np.arange(GRID_W)
    cs = np.clip(col - WIN_COLS // 2, 0, GRID_W - WIN_COLS)
    kc = np.arange(GRID_W)
    valid = (kc[None, :] >= cs[:, None]) & (kc[None, :] < cs[:, None] + WIN_COLS)
    dc = kc[None, :] - col[:, None] + (WIN_COLS - 1)
    onehot = (dc[:, :, None] == np.arange(2 * WIN_COLS - 1)).astype(np.float32)
    t = jnp.einsum('hrd,ckd->hrck', rpb, onehot, precision=lax.Precision.HIGHEST)
    return jnp.where(valid[None, None], t, MASKED)


def _na_group_geometry(rows):
    def geom(g):
        r0 = g * NA_G
        ws = int(np.clip(r0 - WIN_ROWS // 2, 0, rows - NA_WIN))
        rs = [int(np.clip(r0 + i - WIN_ROWS // 2, 0, rows - WIN_ROWS)) for i in range(NA_G)]
        return r0 - ws, tuple(r - ws for r in rs)

    ng = rows // NA_G
    assert ng >= 3 and all(geom(g) == geom(1) for g in range(1, ng - 1))
    return geom(0), geom(1), geom(ng - 1)


def _na_kernel(q_ref, k_ref, v_ref, t_ref, og_ref, o_ref, bias_ref, s_a_ref, s_b_ref, *, rows):
    log2e = np.float32(NA_LOG2E)
    og = og_ref[0]

    masked_tile = jnp.full((GRID_W, GRID_W), MASKED, F32)
    for cls, (roff, rsoff) in enumerate(_na_group_geometry(rows)):
        for ri in range(NA_G):
            for kp in range(NA_WIN // 2):
                pair = []
                for kr in (2 * kp, 2 * kp + 1):
                    in_window = 0 <= kr - rsoff[ri] < WIN_ROWS
                    dr = kr - roff - ri + (WIN_ROWS - 1)
                    pair.append(t_ref[0, dr] * log2e if in_window else masked_tile)
                bias_ref[cls, ri * GRID_W:(ri + 1) * GRID_W, kp * LANES:(kp + 1) * LANES] = (
                    jnp.concatenate(pair, axis=1))

    ng = rows // NA_G
    nq = NA_G * GRID_W
    nk = NA_WIN * GRID_W

    def slices(g):
        r0 = g * NA_G
        ws = jnp.clip(r0 - WIN_ROWS // 2, 0, rows - NA_WIN)
        return (pl.ds(pl.multiple_of(r0 * GRID_W, nq), nq),
                pl.ds(pl.multiple_of(ws * GRID_W, nq), nk))

    def scores(g, s_ref):
        g = jnp.minimum(g, ng - 1)
        cls = jnp.minimum(g, 1) + jnp.maximum(g - (ng - 2), 0)
        qsl, ksl = slices(g)
        s = lax.dot_general(q_ref[0, qsl, :], k_ref[0, ksl, :], (((1,), (1,)), ((), ())),
                            preferred_element_type=F32)
        s_ref[...] = s + bias_ref[cls]

    def attend(g, s_ref):
        qsl, ksl = slices(g)
        s = s_ref[...]
        m = jnp.max(s, axis=-1, keepdims=True)
        e = jnp.exp2(s - m)
        l = jnp.sum(e, axis=-1, keepdims=True)
        o = jnp.dot(e.astype(BF16), v_ref[0, ksl, :], preferred_element_type=F32) / l
        o_ref[qsl, :] = _rms(o, og).astype(o_ref.dtype)

    scores(0, s_a_ref)

    def pair_body(tp, carry):
        g = 2 * tp
        scores(g + 1, s_b_ref)
        attend(g, s_a_ref)
        scores(g + 2, s_a_ref)
        attend(g + 1, s_b_ref)
        return carry

    lax.fori_loop(0, ng // 2, pair_body, 0, unroll=2)


def _na_qk_gain(qg, kg, n_heads):
    qg = qg * np.float32(HEAD_DIM ** -0.5 * NA_LOG2E)
    return jnp.concatenate([jnp.tile(qg, n_heads), jnp.tile(kg, n_heads)]).reshape(1, -1)


def _na(zqk, zv, toeplitz, og, n_heads, batch, seq):
    rows = seq // GRID_W
    assert rows % (2 * NA_G) == 0 and (WIN_ROWS // 2) % NA_G == 0
    n_dr = 2 * WIN_ROWS - 1
    return pl.pallas_call(
        functools.partial(_na_kernel, rows=rows),
        grid=(batch, n_heads),
        in_specs=[pl.BlockSpec((1, seq, HEAD_DIM), lambda b, h: (h, b, 0)),
                  pl.BlockSpec((1, seq, HEAD_DIM), lambda b, h: (n_heads + h, b, 0)),
                  pl.BlockSpec((1, seq, HEAD_DIM), lambda b, h: (h, b, 0)),
                  pl.BlockSpec((1, n_dr, GRID_W, GRID_W), lambda b, h: (h, 0, 0, 0)),
                  pl.BlockSpec((1, 1, HEAD_DIM), lambda b, h: (h, 0, 0))],
        out_specs=pl.BlockSpec((seq, HEAD_DIM), lambda b, h: (b, h)),
        out_shape=jax.ShapeDtypeStruct((batch * seq, n_heads * HEAD_DIM), BF16),
        scratch_shapes=[pltpu.VMEM((3, NA_G * GRID_W, NA_WIN * GRID_W), F32),
                        pltpu.VMEM((NA_G * GRID_W, NA_WIN * GRID_W), F32),
                        pltpu.VMEM((NA_G * GRID_W, NA_WIN * GRID_W), F32)],
        compiler_params=_params(("parallel", "parallel"), 40),
        name="natten",
    )(zqk, zqk, zv, toeplitz, og.reshape(n_heads, 1, HEAD_DIM))


def _residual_epilogue(chunk, r0, epi, outs, tmp, *, rows, nj, d):
    (h_ref,) = epi
    o_ref, ob_ref, rstd_ref = outs
    (ssq_ref,) = tmp
    sl = slice(r0, r0 + rows)
    j = jnp.maximum(pl.program_id(0) - 1, 0) % nj
    h = h_ref[sl, :] + chunk
    o_ref[sl, :] = h
    ob_ref[sl, :] = h.astype(ob_ref.dtype)
    ssq = jnp.broadcast_to(jnp.sum(h * h, axis=-1, keepdims=True), (rows, LANES))
    ssq = jnp.where(j == 0, ssq, ssq_ref[sl, :] + ssq)
    ssq_ref[sl, :] = ssq
    rstd = lax.rsqrt(ssq * np.float32(1.0 / d) + EPS)
    rstd_ref[sl, :] = rstd
    return rstd[0:1, :]


def _residual_matmul(name, lhs, rhs, rhs_blocks, h, *, bm, epi_rows, vmem_mib, side=None):
    t, n = h.shape
    bn = RES_BN
    block = ((bm, bn), lambda i, j: (i, j))
    return _pipelined_matmul(
        name, lhs, rhs, rhs_blocks, [h], [block],
        [block, block, ((bm, LANES), lambda i, j: (i, 0))],
        [jax.ShapeDtypeStruct((t, n), F32), jax.ShapeDtypeStruct((t, n), BF16),
         jax.ShapeDtypeStruct((t, LANES), F32)],
        functools.partial(_residual_epilogue, rows=epi_rows, nj=n // bn, d=n),
        nj=n // bn, bm=bm, bn=bn, epi_rows=epi_rows, vmem_mib=vmem_mib, side=side,
        tmp_shapes=[(bm, LANES)])


def _outproj(mix_a, mix_b, w, x, bm=1024):
    assert mix_a.shape[1] == mix_b.shape[1]
    return _residual_matmul("outproj", [mix_a, mix_b], [w, w],
                            [lambda j: (j, 0, 0), lambda j: (j, 1, 0)], x,
                            bm=bm, epi_rows=32, vmem_mib=48)


def _ffn_up_window_start(i, bm, t):
    return pl.multiple_of(jnp.clip(i * bm - HALO, 0, t - (bm + 2 * HALO)), HALO)


def _ffn_up_kernel(x_ref, r_ref, wg_ref, wv_ref, cwg_ref, cwv_ref, cbg_ref, cbv_ref, side_ref,
                   o_ref, side_o_ref, rstd_ref, up_a_ref, up_b_ref, *, bm, bn, nj, ni, seq):
    s = pl.program_id(0)
    win = bm + 2 * HALO
    t = ni * bm
    i = jnp.minimum(s // nj, ni - 1)
    start = _ffn_up_window_start(i, bm, t)
    ip = jnp.maximum(s - 1, 0) // nj
    off_prev = pl.multiple_of(ip * bm - _ffn_up_window_start(ip, bm, t), HALO)

    @pl.when((s % nj == 0) & (s < ni * nj))
    def _():
        tok = start + lax.broadcasted_iota(jnp.int32, (win, LANES), 0)
        seq_start = (i * bm) // seq * seq
        same_seq = (tok >= seq_start) & (tok < seq_start + seq)
        rstd_ref[...] = jnp.where(same_seq, r_ref[...], 0.0)

    @pl.when(s == 0)
    def _():
        up_a_ref[...] = jnp.zeros(up_a_ref.shape, up_a_ref.dtype)
        up_b_ref[...] = jnp.zeros(up_b_ref.shape, up_b_ref.dtype)

    def step(new_ref, old_ref):
        cw = jnp.concatenate([cwg_ref[...], cwv_ref[...]], axis=1)
        cb = jnp.concatenate([cbg_ref[...], cbv_ref[...]], axis=1)
        w_cur = cw[1:2, :]
        for r0 in range(0, bm, EPI_ROWS):
            rows = EPI_ROWS + 2 * SUBLANES
            mid = slice(SUBLANES, SUBLANES + EPI_ROWS)
            slab = old_ref[pl.ds(pl.multiple_of(off_prev + r0, SUBLANES), rows), :]
            prev = pltpu.roll(slab, 1, 0)[mid]
            cur = slab[mid]
            nxt = pltpu.roll(slab, rows - 1, 0)[mid]
            c = cb + (prev * cw[0:1, :] + cur * w_cur + nxt * cw[2:3, :])
            act = _gelu(c[:, :bn]) * c[:, bn:]
            o_ref[r0:r0 + EPI_ROWS, :] = act.astype(o_ref.dtype)
            zero = _dependent_zero(act[0:1, 0:LANES])
            w_cur = cw[1:2, :] + jnp.concatenate([zero] * (2 * bn // LANES), axis=1)

        if new_ref is None:
            return
        _convert_weight_block(side_ref, None, side_o_ref)

        half = win // 2
        for h0 in (0, half):
            xh = x_ref[h0:h0 + half, :]
            rstd = jnp.concatenate([rstd_ref[h0:h0 + half, :]] * (bn // LANES), axis=1)
            dst = slice(SUBLANES + h0, SUBLANES + h0 + half)
            new_ref[dst, :bn] = jnp.dot(xh, wg_ref[...], preferred_element_type=F32) * rstd
            new_ref[dst, bn:] = jnp.dot(xh, wv_ref[...], preferred_element_type=F32) * rstd

    @pl.when((s % 2 == 0) & (s < ni * nj))
    def _():
        step(up_a_ref, up_b_ref)

    @pl.when((s % 2 == 1) & (s < ni * nj))
    def _():
        step(up_b_ref, up_a_ref)

    @pl.when(s == ni * nj)
    def _():
        step(None, up_a_ref if (ni * nj) % 2 == 1 else up_b_ref)


def _ffn_up(xn, rstd, wg, wv, cw, cb, seq, side, bm=2048):
    t, k = xn.shape
    nj, _, bn = wg.shape
    n = nj * bn
    ni = t // bm
    win = bm + 2 * HALO
    assert seq % bm == 0 and t >= win and win % (2 * HALO) == 0
    side_in_specs, side_operands, side_out_spec, side_out_shape = _side_specs(side, ni * nj)
    assert len(side_operands) == 1

    def window(s):
        return _ffn_up_window_start(jnp.minimum(s // nj, ni - 1), bm, t), 0

    def prev_ij(s):
        sp = jnp.maximum(s - 1, 0)
        return sp // nj, sp % nj

    return pl.pallas_call(
        functools.partial(_ffn_up_kernel, bm=bm, bn=bn, nj=nj, ni=ni, seq=seq),
        grid=(ni * nj + 1,),
        in_specs=[pl.BlockSpec((pl.Element(win), pl.Element(k)), window),
                  pl.BlockSpec((pl.Element(win), pl.Element(LANES)), window),
                  pl.BlockSpec((None, k, bn), lambda s: (s % nj, 0, 0)),
                  pl.BlockSpec((None, k, bn), lambda s: (s % nj, 0, 0)),
                  pl.BlockSpec((3, bn), lambda s: (0, prev_ij(s)[1])),
                  pl.BlockSpec((3, bn), lambda s: (0, nj + prev_ij(s)[1])),
                  pl.BlockSpec((1, bn), lambda s: (0, prev_ij(s)[1])),
                  pl.BlockSpec((1, bn), lambda s: (0, nj + prev_ij(s)[1]))] + side_in_specs,
        out_specs=[pl.BlockSpec((bm, bn), lambda s: prev_ij(s)), side_out_spec],
        out_shape=[jax.ShapeDtypeStruct((t, n), BF16), side_out_shape],
        scratch_shapes=[pltpu.VMEM((win, LANES), F32),
                        pltpu.VMEM((win + 2 * SUBLANES, 2 * bn), F32),
                        pltpu.VMEM((win + 2 * SUBLANES, 2 * bn), F32)],
        compiler_params=_params(("arbitrary",), 60),
        name="ffn_up",
    )(xn, rstd, wg, wv, cw, cw, cb, cb, *side_operands)


def _ffn_down(act, w, h, bm=512, side=None):
    return _residual_matmul("ffn_down", [act], [w], [lambda j: (j, 0, 0)], h,
                            bm=bm, epi_rows=16, vmem_mib=56, side=side)


def _ple_embed_kernel(p_ref, w_ref, g_ref, o_ref):
    e = jnp.dot(p_ref[...].astype(BF16), w_ref[...], preferred_element_type=F32)
    o_ref[...] = _rms(e, g_ref[...]).astype(o_ref.dtype)


def _ple_embed(p, w, g, bm=512):
    t, k = p.shape
    n = w.shape[1]
    return pl.pallas_call(
        _ple_embed_kernel,
        grid=(t // bm,),
        in_specs=[pl.BlockSpec((bm, k), lambda i: (i, 0)),
                  pl.BlockSpec((k, n), lambda i: (0, 0)),
                  pl.BlockSpec((1, n), lambda i: (0, 0))],
        out_specs=pl.BlockSpec((bm, n), lambda i: (i, 0)),
        out_shape=jax.ShapeDtypeStruct((t, n), BF16),
        compiler_params=_params(("parallel",), 40),
        name="ple_embed",
    )(p, w, g.reshape(1, n))


def _ple_gate_epilogue(chunk, r0, epi, outs, tmp, *, rows):
    rstd_ref, e_ref, h_ref = epi
    (o_ref,) = outs
    sl = slice(r0, r0 + rows)
    rstd = jnp.concatenate([rstd_ref[sl, :]] * (chunk.shape[1] // LANES), axis=1)
    out = h_ref[sl, :] + jax.nn.sigmoid(chunk * rstd) * e_ref[sl, :].astype(F32)
    o_ref[sl, :] = out
    return out[0:1, 0:LANES]


def _ple_gate(hb, rstd, w, e, h, bm=1024, epi_rows=8):
    t = hb.shape[0]
    bn = w.shape[2]
    n = w.shape[0] * bn
    block = ((bm, bn), lambda i, j: (i, j))
    return _pipelined_matmul(
        "ple_gate", [hb], [w], [lambda j: (j, 0, 0)],
        [rstd, e, h], [((bm, LANES), lambda i, j: (i, 0)), block, block],
        [block], [jax.ShapeDtypeStruct((t, n), F32)],
        functools.partial(_ple_gate_epilogue, rows=epi_rows),
        nj=n // bn, bm=bm, bn=bn, epi_rows=epi_rows, vmem_mib=48)[0]


def kernel(x, p, norm_mix_g, w_in, gmlp_v_g, gmlp_ws, gmlp_bs, q_norm_g, k_norm_g, na_rpb,
           out_norm_a_g, out_norm_b_g, w_out, norm_ffn_g, w_up, conv_w, conv_b, w_down,
           norm_ple_g, w_ple_gate, w_ple_proj, ple_post_g):
    batch, seq, d_model = x.shape
    t = batch * seq
    depth = w_in.shape[0]
    n_a = gmlp_ws.shape[1]
    n_b = na_rpb.shape[1]
    h = x.reshape(t, d_model)
    for i in range(depth):
        hn = _rmsnorm(h, norm_mix_g[i])
        w = w_in[i].astype(BF16)
        d_a, d_b = n_a * HEAD_DIM, n_b * HEAD_DIM
        d_ff = w_down.shape[1]
        g_ffn = jnp.broadcast_to(norm_ffn_g[i][:, None], (d_model, LANES))
        g_ple = jnp.broadcast_to(norm_ple_g[i][:, None], (d_model, LANES))
        zuv, w_gate = _inproj(hn, w, 0, 2 * d_a, "gelu",
                              side=(w_up[i], g_ffn, 64, d_ff, 0, FFN_BN))
        zqk, w_val = _inproj(hn, w, 2 * d_a, 2 * d_b, "norm",
                             _na_qk_gain(q_norm_g[i], k_norm_g[i], n_b),
                             side=(w_up[i], g_ffn, 64, d_ff, 1, FFN_BN))
        zv, w_o = _inproj(hn, w, 2 * d_a + 2 * d_b, d_b, "plain",
                          side=(w_out[i], None, 128, d_model, 0, RES_BN))
        mix_a = _gmlp(zuv, gmlp_ws[i], gmlp_bs[i], gmlp_v_g[i], out_norm_a_g[i], n_a)
        mix_b = _na(zqk, zv, _na_toeplitz(na_rpb[i]), out_norm_b_g[i], n_b, batch, seq)
        h, hb, rstd = _outproj(mix_a, mix_b, w_o, h)
        act, w_dn = _ffn_up(hb, rstd, w_gate, w_val,
                            conv_w[i], conv_b[i].reshape(1, -1), seq,
                            side=(w_down[i], None, 32, d_model, 0, RES_BN))
        h, hb, rstd, w_pg = _ffn_down(act, w_dn, h,
                                      side=(w_ple_gate[i], g_ple, 16, d_model, 0, PLE_BN))
        e = _ple_embed(p[i].reshape(t, -1), w_ple_proj[i].astype(BF16), ple_post_g[i])
        h = _ple_gate(hb, rstd, w_pg, e, h)
    return h.reshape(batch, seq, d_model)
```

```python
import functools

import numpy as np
import jax
import jax.numpy as jnp
from jax import lax
from jax.experimental import pallas as pl
from jax.experimental.pallas import tpu as pltpu

F32 = jnp.float32
BF16 = jnp.bfloat16

LANES = 128
SUBLANES = 8

EPS = 1e-6
HEAD_DIM = 128
CHUNK = 128
GMLP_GROUP = 8
GRID_W = 64
WIN_ROWS = 8
WIN_COLS = 16
NA_G = 4
NA_WIN = NA_G + WIN_ROWS
MASKED = -1e30
NA_LOG2E = float(np.log2(np.e))
HALO = 16
EPI_ROWS = 8
RES_BN = 512
FFN_BN = 256
PLE_BN = 512
MIB = 1 << 20


def _params(semantics, vmem_mib):
    return pltpu.CompilerParams(dimension_semantics=semantics, vmem_limit_bytes=vmem_mib * MIB)


def _rms(x, g):
    ms = jnp.mean(x * x, axis=-1, keepdims=True)
    return x * lax.rsqrt(ms + EPS) * g


def _gelu(x):
    return 0.5 * x * (1.0 + lax.erf(x * np.float32(np.sqrt(0.5))))


def _rmsnorm_kernel(x_ref, g_ref, o_ref):
    o_ref[...] = _rms(x_ref[...], g_ref[...]).astype(o_ref.dtype)


def _rmsnorm(x, g, bm=512):
    t, d = x.shape
    return pl.pallas_call(
        _rmsnorm_kernel,
        grid=(t // bm,),
        in_specs=[pl.BlockSpec((bm, d), lambda i: (i, 0)),
                  pl.BlockSpec((1, d), lambda i: (0, 0))],
        out_specs=pl.BlockSpec((bm, d), lambda i: (i, 0)),
        out_shape=jax.ShapeDtypeStruct((t, d), BF16),
        compiler_params=_params(("parallel",), 40),
        name="rmsnorm",
    )(x, g.reshape(1, d))


def _dependent_zero(x):
    bits = pltpu.bitcast(x, jnp.uint32)
    return pltpu.bitcast((bits >> 16) >> 16, F32)


def _convert_weight_block(x_ref, g_ref, o_ref):
    n_cb, _, bw = o_ref.shape
    g = None if g_ref is None else jnp.concatenate([g_ref[...]] * (bw // LANES), axis=1)
    for c in range(n_cb):
        blk = x_ref[:, c * bw:(c + 1) * bw]
        if g is not None:
            blk = blk * g
        o_ref[c] = blk.astype(o_ref.dtype)


def _side_specs(side, n_steps):
    x, gain, rows, cols, cj, bw = side
    n_side = x.shape[0] // rows
    assert n_side <= n_steps and x.shape[0] % rows == 0 and cols % bw == 0 and bw % LANES == 0

    def blk(s):
        return jnp.minimum(s, n_side - 1), cj

    in_specs = [pl.BlockSpec((rows, cols), blk)]
    operands = [x]
    if gain is not None:
        in_specs.append(pl.BlockSpec((rows, LANES), lambda s: (blk(s)[0], 0)))
        operands.append(gain)
    out_spec = pl.BlockSpec((cols // bw, rows, bw), lambda s: (0, blk(s)[0], 0))
    out_shape = jax.ShapeDtypeStruct((cols // bw, x.shape[0], bw), BF16)
    return in_specs, operands, out_spec, out_shape


def _pipelined_matmul_kernel(*refs, n_lhs, n_epi, n_out, n_side, n_tmp, n_blocks, epilogue, bm, bn,
                             epi_rows):
    lhs = refs[:n_lhs]
    rhs = refs[n_lhs:2 * n_lhs]
    epi = refs[2 * n_lhs:2 * n_lhs + n_epi]
    side_in = refs[2 * n_lhs + n_epi:2 * n_lhs + n_epi + n_side]
    n_in = 2 * n_lhs + n_epi + n_side
    outs = refs[n_in:n_in + n_out]
    side_out = refs[n_in + n_out:len(refs) - 2 - n_tmp]
    tmp = refs[len(refs) - 2 - n_tmp:-2]
    acc_a_ref, acc_b_ref = refs[-2:]
    s = pl.program_id(0)

    @pl.when(s == 0)
    def _():
        acc_b_ref[...] = jnp.zeros(acc_b_ref.shape, acc_b_ref.dtype)
        for t_ref in tmp:
            t_ref[...] = jnp.zeros(t_ref.shape, t_ref.dtype)

    def step(new_ref, old_ref):
        zero = None
        for r0 in range(0, bm, epi_rows):
            chunk = old_ref[r0:r0 + epi_rows, :]
            if zero is not None:
                chunk = chunk + zero
            last = epilogue(chunk, r0, epi, outs, tmp)
            zero = jnp.concatenate([_dependent_zero(last)] * (bn // LANES), axis=1)
        if new_ref is None:
            return
        if side_in:
            _convert_weight_block(side_in[0], side_in[1] if n_side == 2 else None, side_out[0])
        acc = jnp.dot(lhs[0][...], rhs[0][...], preferred_element_type=F32)
        for a_ref, w_ref in zip(lhs[1:], rhs[1:]):
            acc = acc + jnp.dot(a_ref[...], w_ref[...], preferred_element_type=F32)
        new_ref[...] = acc

    @pl.when((s % 2 == 0) & (s < n_blocks))
    def _():
        step(acc_a_ref, acc_b_ref)

    @pl.when((s % 2 == 1) & (s < n_blocks))
    def _():
        step(acc_b_ref, acc_a_ref)

    @pl.when(s == n_blocks)
    def _():
        step(None, acc_a_ref if n_blocks % 2 == 1 else acc_b_ref)


def _pipelined_matmul(name, lhs, rhs, rhs_blocks, epi, epi_specs, out_specs, out_shapes, epilogue,
                      *, nj, bm, bn, epi_rows, vmem_mib, side=None, tmp_shapes=()):
    ni = lhs[0].shape[0] // bm
    n_blocks = ni * nj

    def cur_ij(s):
        return jnp.minimum(s // nj, ni - 1), s % nj

    def prev_ij(s):
        sp = jnp.maximum(s - 1, 0)
        return sp // nj, sp % nj

    def cur_row(s):
        return cur_ij(s)[0], 0

    in_specs = [pl.BlockSpec((bm, a.shape[1]), cur_row) for a in lhs]
    in_specs += [pl.BlockSpec((None,) * (w.ndim - 2) + (a.shape[1], bn),
                              functools.partial(lambda s, blk: blk(cur_ij(s)[1]), blk=blk))
                 for a, w, blk in zip(lhs, rhs, rhs_blocks)]
    in_specs += [pl.BlockSpec(shape, functools.partial(lambda s, fn: fn(*prev_ij(s)), fn=fn))
                 for shape, fn in epi_specs]
    out_block_specs = [pl.BlockSpec(shape, functools.partial(lambda s, fn: fn(*prev_ij(s)), fn=fn))
                       for shape, fn in out_specs]
    out_shapes = list(out_shapes)
    side_operands = []
    if side is not None:
        side_in_specs, side_operands, side_out_spec, side_out_shape = _side_specs(side, n_blocks)
        in_specs += side_in_specs
        out_block_specs.append(side_out_spec)
        out_shapes.append(side_out_shape)
    return pl.pallas_call(
        functools.partial(_pipelined_matmul_kernel, n_lhs=len(lhs), n_epi=len(epi),
                          n_out=len(out_specs), n_side=len(side_operands),
                          n_tmp=len(tmp_shapes), n_blocks=n_blocks,
                          epilogue=epilogue, bm=bm, bn=bn, epi_rows=epi_rows),
        grid=(n_blocks + 1,),
        in_specs=in_specs,
        out_specs=out_block_specs,
        out_shape=out_shapes,
        scratch_shapes=[pltpu.VMEM(shape, F32) for shape in tmp_shapes]
        + [pltpu.VMEM((bm, bn), F32), pltpu.VMEM((bm, bn), F32)],
        compiler_params=_params(("arbitrary",), vmem_mib),
        name=name,
    )(*lhs, *rhs, *epi, *side_operands)


def _heads_epilogue(chunk, r0, epi, outs, tmp, *, rows, kind):
    (o_ref,) = outs
    y = None
    for hh in range(chunk.shape[1] // HEAD_DIM):
        cols = slice(hh * HEAD_DIM, (hh + 1) * HEAD_DIM)
        y = chunk[:, cols]
        if kind == "gelu":
            y = _gelu(y)
        elif kind == "norm":
            y = _rms(y, epi[0][:, cols])
        o_ref[hh, r0:r0 + rows, :] = y.astype(o_ref.dtype)
    return y[0:1, :]


def _inproj(hn, w, col0, n_cols, kind, gain=None, bm=1024, bn=1024, side=None):
    t = hn.shape[0]
    hpb = bn // HEAD_DIM
    epi_rows = {"norm": 32, "gelu": 8, "plain": 16}[kind]
    epi, epi_specs = [], []
    if kind == "norm":
        epi, epi_specs = [gain], [((1, bn), lambda i, j: (0, j))]
    return _pipelined_matmul(
        "inproj_" + kind, [hn], [w], [lambda j: (0, col0 // bn + j)], epi, epi_specs,
        [((hpb, bm, HEAD_DIM), lambda i, j: (j, i, 0))],
        [jax.ShapeDtypeStruct((n_cols // HEAD_DIM, t, HEAD_DIM), BF16)],
        functools.partial(_heads_epilogue, rows=epi_rows, kind=kind),
        nj=n_cols // bn, bm=bm, bn=bn, epi_rows=epi_rows, vmem_mib=62, side=side)


def _gmlp_kernel(u_ref, v_ref, ws_ref, bs_ref, gv_ref, og_ref, o_ref, *, n_groups):
    ws = ws_ref[0]
    bs = bs_ref[0]
    gv = gv_ref[0]
    og = og_ref[0]

    def group_body(gi, carry):
        base = pl.multiple_of(gi * (GMLP_GROUP * CHUNK), GMLP_GROUP * CHUNK)
        for c in range(GMLP_GROUP):
            sl = pl.ds(base + c * CHUNK, CHUNK)
            vn = _rms(v_ref[0, sl, :].astype(F32), gv).astype(BF16)
            mixed = jnp.dot(ws, vn, preferred_element_type=F32) + bs
            a = u_ref[0, sl, :].astype(F32) * mixed
            o_ref[sl, :] = _rms(a, og).astype(o_ref.dtype)
        return carry

    lax.fori_loop(0, n_groups, group_body, 0)


def _gmlp(z, ws, bs, gv, og, n_heads, tb=8192):
    t = z.shape[1]
    return pl.pallas_call(
        functools.partial(_gmlp_kernel, n_groups=tb // (GMLP_GROUP * CHUNK)),
        grid=(n_heads, t // tb),
        in_specs=[pl.BlockSpec((1, tb, HEAD_DIM), lambda h, i: (h, i, 0)),
                  pl.BlockSpec((1, tb, HEAD_DIM), lambda h, i: (n_heads + h, i, 0)),
                  pl.BlockSpec((1, CHUNK, CHUNK), lambda h, i: (h, 0, 0)),
                  pl.BlockSpec((1, CHUNK, 1), lambda h, i: (h, 0, 0)),
                  pl.BlockSpec((1, 1, HEAD_DIM), lambda h, i: (h, 0, 0)),
                  pl.BlockSpec((1, 1, HEAD_DIM), lambda h, i: (h, 0, 0))],
        out_specs=pl.BlockSpec((tb, HEAD_DIM), lambda h, i: (i, h)),
        out_shape=jax.ShapeDtypeStruct((t, n_heads * HEAD_DIM), BF16),
        compiler_params=_params(("parallel", "parallel"), 24),
        name="gmlp",
    )(z, z, ws.astype(BF16), bs.reshape(n_heads, CHUNK, 1),
      gv.reshape(n_heads, 1, HEAD_DIM), og.reshape(n_heads, 1, HEAD_DIM))


def _na_toeplitz(rpb):
    col = np.arange(GRID_W)
    cs = np.clip(col - WIN_COLS // 2, 0, GRID_W - WIN_COLS)
    kc = np.arange(GRID_W)
    valid = (kc[None, :] >= cs[:, None]) & (kc[None, :] < cs[:, None] + WIN_COLS)
    dc = kc[None, :] - col[:, None] + (WIN_COLS - 1)
    onehot = (dc[:, :, None] == np.arange(2 * WIN_COLS - 1)).astype(np.float32)
    t = jnp.einsum('hrd,ckd->hrck', rpb, onehot, precision=lax.Precision.HIGHEST)
    return jnp.where(valid[None, None], t, MASKED)


def _na_group_geometry(rows):
    def geom(g):
        r0 = g * NA_G
        ws = int(np.clip(r0 - WIN_ROWS // 2, 0, rows - NA_WIN))
        rs = [int(np.clip(r0 + i - WIN_ROWS // 2, 0, rows - WIN_ROWS)) for i in range(NA_G)]
        return r0 - ws, tuple(r - ws for r in rs)

    ng = rows // NA_G
    assert ng >= 3 and all(geom(g) == geom(1) for g in range(1, ng - 1))
    return geom(0), geom(1), geom(ng - 1)


def _na_kernel(q_ref, k_ref, v_ref, t_ref, og_ref, o_ref, bias_ref, s_a_ref, s_b_ref, *, rows):
    log2e = np.float32(NA_LOG2E)
    og = og_ref[0]

    masked_tile = jnp.full((GRID_W, GRID_W), MASKED, F32)
    for cls, (roff, rsoff) in enumerate(_na_group_geometry(rows)):
        for ri in range(NA_G):
            for kp in range(NA_WIN // 2):
                pair = []
                for kr in (2 * kp, 2 * kp + 1):
                    in_window = 0 <= kr - rsoff[ri] < WIN_ROWS
                    dr = kr - roff - ri + (WIN_ROWS - 1)
                    pair.append(t_ref[0, dr] * log2e if in_window else masked_tile)
                bias_ref[cls, ri * GRID_W:(ri + 1) * GRID_W, kp * LANES:(kp + 1) * LANES] = (
                    jnp.concatenate(pair, axis=1))

    ng = rows // NA_G
    nq = NA_G * GRID_W
    nk = NA_WIN * GRID_W

    def slices(g):
        r0 = g * NA_G
        ws = jnp.clip(r0 - WIN_ROWS // 2, 0, rows - NA_WIN)
        return (pl.ds(pl.multiple_of(r0 * GRID_W, nq), nq),
                pl.ds(pl.multiple_of(ws * GRID_W, nq), nk))

    def scores(g, s_ref):
        g = jnp.minimum(g, ng - 1)
        cls = jnp.minimum(g, 1) + jnp.maximum(g - (ng - 2), 0)
        qsl, ksl = slices(g)
        s = lax.dot_general(q_ref[0, qsl, :], k_ref[0, ksl, :], (((1,), (1,)), ((), ())),
                            preferred_element_type=F32)
        s_ref[...] = s + bias_ref[cls]

    def attend(g, s_ref):
        qsl, ksl = slices(g)
        s = s_ref[...]
        m = jnp.max(s, axis=-1, keepdims=True)
        e = jnp.exp2(s - m)
        l = jnp.sum(e, axis=-1, keepdims=True)
        o = jnp.dot(e.astype(BF16), v_ref[0, ksl, :], preferred_element_type=F32) / l
        o_ref[qsl, :] = _rms(o, og).astype(o_ref.dtype)

    scores(0, s_a_ref)

    def pair_body(tp, carry):
        g = 2 * tp
        scores(g + 1, s_b_ref)
        attend(g, s_a_ref)
        scores(g + 2, s_a_ref)
        attend(g + 1, s_b_ref)
        return carry

    lax.fori_loop(0, ng // 2, pair_body, 0, unroll=4)


def _na_qk_gain(qg, kg, n_heads):
    qg = qg * np.float32(HEAD_DIM ** -0.5 * NA_LOG2E)
    return jnp.concatenate([jnp.tile(qg, n_heads), jnp.tile(kg, n_heads)]).reshape(1, -1)


def _na(zqk, zv, toeplitz, og, n_heads, batch, seq):
    rows = seq // GRID_W
    assert rows % (2 * NA_G) == 0 and (WIN_ROWS // 2) % NA_G == 0
    n_dr = 2 * WIN_ROWS - 1
    return pl.pallas_call(
        functools.partial(_na_kernel, rows=rows),
        grid=(batch, n_heads),
        in_specs=[pl.BlockSpec((1, seq, HEAD_DIM), lambda b, h: (h, b, 0)),
                  pl.BlockSpec((1, seq, HEAD_DIM), lambda b, h: (n_heads + h, b, 0)),
                  pl.BlockSpec((1, seq, HEAD_DIM), lambda b, h: (h, b, 0)),
                  pl.BlockSpec((1, n_dr, GRID_W, GRID_W), lambda b, h: (h, 0, 0, 0)),
                  pl.BlockSpec((1, 1, HEAD_DIM), lambda b, h: (h, 0, 0))],
        out_specs=pl.BlockSpec((seq, HEAD_DIM), lambda b, h: (b, h)),
        out_shape=jax.ShapeDtypeStruct((batch * seq, n_heads * HEAD_DIM), BF16),
        scratch_shapes=[pltpu.VMEM((3, NA_G * GRID_W, NA_WIN * GRID_W), F32),
                        pltpu.VMEM((NA_G * GRID_W, NA_WIN * GRID_W), F32),
                        pltpu.VMEM((NA_G * GRID_W, NA_WIN * GRID_W), F32)],
        compiler_params=_params(("parallel", "parallel"), 40),
        name="natten",
    )(zqk, zqk, zv, toeplitz, og.reshape(n_heads, 1, HEAD_DIM))


def _residual_epilogue(chunk, r0, epi, outs, tmp, *, rows, nj, d):
    (h_ref,) = epi
    o_ref, ob_ref, rstd_ref = outs
    (ssq_ref,) = tmp
    sl = slice(r0, r0 + rows)
    j = jnp.maximum(pl.program_id(0) - 1, 0) % nj
    h = h_ref[sl, :] + chunk
    o_ref[sl, :] = h
    ob_ref[sl, :] = h.astype(ob_ref.dtype)
    ssq = jnp.broadcast_to(jnp.sum(h * h, axis=-1, keepdims=True), (rows, LANES))
    ssq = jnp.where(j == 0, ssq, ssq_ref[sl, :] + ssq)
    ssq_ref[sl, :] = ssq
    rstd = lax.rsqrt(ssq * np.float32(1.0 / d) + EPS)
    rstd_ref[sl, :] = rstd
    return rstd[0:1, :]


def _residual_matmul(name, lhs, rhs, rhs_blocks, h, *, bm, epi_rows, vmem_mib, side=None):
    t, n = h.shape
    bn = RES_BN
    block = ((bm, bn), lambda i, j: (i, j))
    return _pipelined_matmul(
        name, lhs, rhs, rhs_blocks, [h], [block],
        [block, block, ((bm, LANES), lambda i, j: (i, 0))],
        [jax.ShapeDtypeStruct((t, n), F32), jax.ShapeDtypeStruct((t, n), BF16),
         jax.ShapeDtypeStruct((t, LANES), F32)],
        functools.partial(_residual_epilogue, rows=epi_rows, nj=n // bn, d=n),
        nj=n // bn, bm=bm, bn=bn, epi_rows=epi_rows, vmem_mib=vmem_mib, side=side,
        tmp_shapes=[(bm, LANES)])


def _outproj(mix_a, mix_b, w, x, bm=1024):
    assert mix_a.shape[1] == mix_b.shape[1]
    return _residual_matmul("outproj", [mix_a, mix_b], [w, w],
                            [lambda j: (j, 0, 0), lambda j: (j, 1, 0)], x,
                            bm=bm, epi_rows=32, vmem_mib=48)


def _ffn_up_window_start(i, bm, t):
    return pl.multiple_of(jnp.clip(i * bm - HALO, 0, t - (bm + 2 * HALO)), HALO)


def _ffn_up_kernel(x_ref, r_ref, wg_ref, wv_ref, cwg_ref, cwv_ref, cbg_ref, cbv_ref, side_ref,
                   o_ref, side_o_ref, rstd_ref, up_a_ref, up_b_ref, *, bm, bn, nj, ni, seq):
    s = pl.program_id(0)
    win = bm + 2 * HALO
    t = ni * bm
    i = jnp.minimum(s // nj, ni - 1)
    start = _ffn_up_window_start(i, bm, t)
    ip = jnp.maximum(s - 1, 0) // nj
    off_prev = pl.multiple_of(ip * bm - _ffn_up_window_start(ip, bm, t), HALO)

    @pl.when((s % nj == 0) & (s < ni * nj))
    def _():
        tok = start + lax.broadcasted_iota(jnp.int32, (win, LANES), 0)
        seq_start = (i * bm) // seq * seq
        same_seq = (tok >= seq_start) & (tok < seq_start + seq)
        rstd_ref[...] = jnp.where(same_seq, r_ref[...], 0.0)

    @pl.when(s == 0)
    def _():
        up_a_ref[...] = jnp.zeros(up_a_ref.shape, up_a_ref.dtype)
        up_b_ref[...] = jnp.zeros(up_b_ref.shape, up_b_ref.dtype)

    def step(new_ref, old_ref):
        cw = jnp.concatenate([cwg_ref[...], cwv_ref[...]], axis=1)
        cb = jnp.concatenate([cbg_ref[...], cbv_ref[...]], axis=1)
        w_cur = cw[1:2, :]
        for r0 in range(0, bm, EPI_ROWS):
            rows = EPI_ROWS + 2 * SUBLANES
            mid = slice(SUBLANES, SUBLANES + EPI_ROWS)
            slab = old_ref[pl.ds(pl.multiple_of(off_prev + r0, SUBLANES), rows), :]
            prev = pltpu.roll(slab, 1, 0)[mid]
            cur = slab[mid]
            nxt = pltpu.roll(slab, rows - 1, 0)[mid]
            c = cb + (prev * cw[0:1, :] + cur * w_cur + nxt * cw[2:3, :])
            act = _gelu(c[:, :bn]) * c[:, bn:]
            o_ref[r0:r0 + EPI_ROWS, :] = act.astype(o_ref.dtype)
            zero = _dependent_zero(act[0:1, 0:LANES])
            w_cur = cw[1:2, :] + jnp.concatenate([zero] * (2 * bn // LANES), axis=1)

        if new_ref is None:
            return
        _convert_weight_block(side_ref, None, side_o_ref)

        half = win // 2
        for h0 in (0, half):
            xh = x_ref[h0:h0 + half, :]
            rstd = jnp.concatenate([rstd_ref[h0:h0 + half, :]] * (bn // LANES), axis=1)
            dst = slice(SUBLANES + h0, SUBLANES + h0 + half)
            new_ref[dst, :bn] = jnp.dot(xh, wg_ref[...], preferred_element_type=F32) * rstd
            new_ref[dst, bn:] = jnp.dot(xh, wv_ref[...], preferred_element_type=F32) * rstd

    @pl.when((s % 2 == 0) & (s < ni * nj))
    def _():
        step(up_a_ref, up_b_ref)

    @pl.when((s % 2 == 1) & (s < ni * nj))
    def _():
        step(up_b_ref, up_a_ref)

    @pl.when(s == ni * nj)
    def _():
        step(None, up_a_ref if (ni * nj) % 2 == 1 else up_b_ref)


def _ffn_up(xn, rstd, wg, wv, cw, cb, seq, side, bm=2048):
    t, k = xn.shape
    nj, _, bn = wg.shape
    n = nj * bn
    ni = t // bm
    win = bm + 2 * HALO
    assert seq % bm == 0 and t >= win and win % (2 * HALO) == 0
    side_in_specs, side_operands, side_out_spec, side_out_shape = _side_specs(side, ni * nj)
    assert len(side_operands) == 1

    def window(s):
        return _ffn_up_window_start(jnp.minimum(s // nj, ni - 1), bm, t), 0

    def prev_ij(s):
        sp = jnp.maximum(s - 1, 0)
        return sp // nj, sp % nj

    return pl.pallas_call(
        functools.partial(_ffn_up_kernel, bm=bm, bn=bn, nj=nj, ni=ni, seq=seq),
        grid=(ni * nj + 1,),
        in_specs=[pl.BlockSpec((pl.Element(win), pl.Element(k)), window),
                  pl.BlockSpec((pl.Element(win), pl.Element(LANES)), window),
                  pl.BlockSpec((None, k, bn), lambda s: (s % nj, 0, 0)),
                  pl.BlockSpec((None, k, bn), lambda s: (s % nj, 0, 0)),
                  pl.BlockSpec((3, bn), lambda s: (0, prev_ij(s)[1])),
                  pl.BlockSpec((3, bn), lambda s: (0, nj + prev_ij(s)[1])),
                  pl.BlockSpec((1, bn), lambda s: (0, prev_ij(s)[1])),
                  pl.BlockSpec((1, bn), lambda s: (0, nj + prev_ij(s)[1]))] + side_in_specs,
        out_specs=[pl.BlockSpec((bm, bn), lambda s: prev_ij(s)), side_out_spec],
        out_shape=[jax.ShapeDtypeStruct((t, n), BF16), side_out_shape],
        scratch_shapes=[pltpu.VMEM((win, LANES), F32),
                        pltpu.VMEM((win + 2 * SUBLANES, 2 * bn), F32),
                        pltpu.VMEM((win + 2 * SUBLANES, 2 * bn), F32)],
        compiler_params=_params(("arbitrary",), 60),
        name="ffn_up",
    )(xn, rstd, wg, wv, cw, cw, cb, cb, *side_operands)


def _ffn_down(act, w, h, bm=512, side=None):
    return _residual_matmul("ffn_down", [act], [w], [lambda j: (j, 0, 0)], h,
                            bm=bm, epi_rows=16, vmem_mib=56, side=side)


def _ple_embed_kernel(p_ref, w_ref, g_ref, o_ref):
    e = jnp.dot(p_ref[...].astype(BF16), w_ref[...], preferred_element_type=F32)
    o_ref[...] = _rms(e, g_ref[...]).astype(o_ref.dtype)


def _ple_embed(p, w, g, bm=512):
    t, k = p.shape
    n = w.shape[1]
    return pl.pallas_call(
        _ple_embed_kernel,
        grid=(t // bm,),
        in_specs=[pl.BlockSpec((bm, k), lambda i: (i, 0)),
                  pl.BlockSpec((k, n), lambda i: (0, 0)),
                  pl.BlockSpec((1, n), lambda i: (0, 0))],
        out_specs=pl.BlockSpec((bm, n), lambda i: (i, 0)),
        out_shape=jax.ShapeDtypeStruct((t, n), BF16),
        compiler_params=_params(("parallel",), 40),
        name="ple_embed",
    )(p, w, g.reshape(1, n))


def _ple_gate_epilogue(chunk, r0, epi, outs, tmp, *, rows):
    rstd_ref, e_ref, h_ref = epi
    (o_ref,) = outs
    sl = slice(r0, r0 + rows)
    rstd = jnp.concatenate([rstd_ref[sl, :]] * (chunk.shape[1] // LANES), axis=1)
    out = h_ref[sl, :] + jax.nn.sigmoid(chunk * rstd) * e_ref[sl, :].astype(F32)
    o_ref[sl, :] = out
    return out[0:1, 0:LANES]


def _ple_gate(hb, rstd, w, e, h, bm=1024, epi_rows=8):
    t = hb.shape[0]
    bn = w.shape[2]
    n = w.shape[0] * bn
    block = ((bm, bn), lambda i, j: (i, j))
    return _pipelined_matmul(
        "ple_gate", [hb], [w], [lambda j: (j, 0, 0)],
        [rstd, e, h], [((bm, LANES), lambda i, j: (i, 0)), block, block],
        [block], [jax.ShapeDtypeStruct((t, n), F32)],
        functools.partial(_ple_gate_epilogue, rows=epi_rows),
        nj=n // bn, bm=bm, bn=bn, epi_rows=epi_rows, vmem_mib=48)[0]


def kernel(x, p, norm_mix_g, w_in, gmlp_v_g, gmlp_ws, gmlp_bs, q_norm_g, k_norm_g, na_rpb,
           out_norm_a_g, out_norm_b_g, w_out, norm_ffn_g, w_up, conv_w, conv_b, w_down,
           norm_ple_g, w_ple_gate, w_ple_proj, ple_post_g):
    batch, seq, d_model = x.shape
    t = batch * seq
    depth = w_in.shape[0]
    n_a = gmlp_ws.shape[1]
    n_b = na_rpb.shape[1]
    h = x.reshape(t, d_model)
    for i in range(depth):
        hn = _rmsnorm(h, norm_mix_g[i])
        w = w_in[i].astype(BF16)
        d_a, d_b = n_a * HEAD_DIM, n_b * HEAD_DIM
        d_ff = w_down.shape[1]
        g_ffn = jnp.broadcast_to(norm_ffn_g[i][:, None], (d_model, LANES))
        g_ple = jnp.broadcast_to(norm_ple_g[i][:, None], (d_model, LANES))
        zuv, w_gate = _inproj(hn, w, 0, 2 * d_a, "gelu",
                              side=(w_up[i], g_ffn, 64, d_ff, 0, FFN_BN))
        zqk, w_val = _inproj(hn, w, 2 * d_a, 2 * d_b, "norm",
                             _na_qk_gain(q_norm_g[i], k_norm_g[i], n_b),
                             side=(w_up[i], g_ffn, 64, d_ff, 1, FFN_BN))
        zv, w_o = _inproj(hn, w, 2 * d_a + 2 * d_b, d_b, "plain",
                          side=(w_out[i], None, 128, d_model, 0, RES_BN))
        mix_a = _gmlp(zuv, gmlp_ws[i], gmlp_bs[i], gmlp_v_g[i], out_norm_a_g[i], n_a)
        mix_b = _na(zqk, zv, _na_toeplitz(na_rpb[i]), out_norm_b_g[i], n_b, batch, seq)
        h, hb, rstd = _outproj(mix_a, mix_b, w_o, h)
        act, w_dn = _ffn_up(hb, rstd, w_gate, w_val,
                            conv_w[i], conv_b[i].reshape(1, -1), seq,
                            side=(w_down[i], None, 32, d_model, 0, RES_BN))
        h, hb, rstd, w_pg = _ffn_down(act, w_dn, h,
                                      side=(w_ple_gate[i], g_ple, 16, d_model, 0, PLE_BN))
        e = _ple_embed(p[i].reshape(t, -1), w_ple_proj[i].astype(BF16), ple_post_g[i])
        h = _ple_gate(hb, rstd, w_pg, e, h)
    return h.reshape(batch, seq, d_model)
```
